```python
import jax
import jax.numpy as jnp
from jax import lax
import numpy as np

D_MODEL = 2048
BATCH = 8
SEQ = 2048
DEPTH = 2
DEC_BATCH = 32
DEC_SEQ = 64
PAST_LEN = 1024

CHUNK = 64
N_MIXERS = 2
A_DK = 128
A_HEADS = D_MODEL // A_DK
A_DV = D_MODEL // A_HEADS
A_FD = A_HEADS * A_DK
B_HEAD_DIM = 64
B_Q_HEADS = D_MODEL // B_HEAD_DIM
B_KV_HEADS = B_Q_HEADS // 8
B_GROUP = B_Q_HEADS // B_KV_HEADS
WINDOW = 128
W_CHUNKS = -(-WINDOW // CHUNK)
FF_DIM = -(-8 * D_MODEL // (3 * 256)) * 256
PLE_DIM = 256
RMS_EPS = 1e-6

kernel_name = 'hybrid_hgrn2_swa_sink_stream_step'


def _rms(x, g):
    xf = x.astype(jnp.float32)
    y = xf * lax.rsqrt(jnp.mean(xf * xf, axis=-1, keepdims=True) + RMS_EPS)
    return (y * g.astype(jnp.float32)).astype(x.dtype)


def _gla_scan(q, k, v, logf, s0):
    b, L, h, _ = q.shape
    dv = v.shape[-1]
    c = min(CHUNK, L)
    n = L // c

    def blocks(t):
        return t.reshape(b, n, c, h, t.shape[-1]).transpose(1, 0, 3, 2, 4)

    causal = jnp.tril(jnp.ones((c, c), dtype=bool))

    def step(S, inp):
        qc, kc, vc, gc = inp
        G = jnp.cumsum(gc, axis=2)
        diff = G[:, :, :, None, :] - G[:, :, None, :, :]
        decay = jnp.exp(jnp.where(causal[:, :, None], diff, -jnp.inf))
        A = jnp.einsum('bhtd,bhtsd,bhsd->bhts', qc, decay, kc)
        o = A @ vc + jnp.einsum('bhtd,bhde->bhte', qc * jnp.exp(G), S)
        G_last = G[:, :, -1:, :]
        S = jnp.exp(G_last[:, :, 0, :, None]) * S + jnp.einsum('bhsd,bhse->bhde', kc * jnp.exp(G_last - G), vc)
        return S, o

    S, o = lax.scan(step, s0.astype(jnp.float32), (blocks(q), blocks(k), blocks(v), blocks(logf)))
    o = o.transpose(1, 0, 3, 2, 4).reshape(b, L, h, dv)
    return o, S


def _hgrn2(xn, w_in, lb, g_norm, w_o, s0):
    b, L, _ = xn.shape
    proj = xn @ w_in
    q, f, i, g = jnp.split(proj, [A_FD, 2 * A_FD, 2 * A_FD + A_HEADS * A_DV], axis=-1)
    q = jax.nn.silu(q.astype(jnp.float32))
    forget = lb + (1.0 - lb) * jax.nn.sigmoid(f.astype(jnp.float32))
    k = 1.0 - forget
    logf = jnp.log(forget)

    def heads(t):
        return t.reshape(b, L, A_HEADS, -1)

    o, S = _gla_scan(heads(q), heads(k), heads(i.astype(jnp.float32)), heads(logf), s0)
    o = _rms(o.reshape(b, L, A_HEADS * A_DV), g_norm) * jax.nn.silu(g.astype(jnp.float32))
    return o.astype(xn.dtype) @ w_o, S


def _alibi_slopes():
    return 2.0 ** (-8.0 * jnp.arange(1, B_Q_HEADS + 1, dtype=jnp.float32) / B_Q_HEADS)


def _qkv(xn, w_qkv, q_gain, k_gain):
    b, L, _ = xn.shape
    qkv = xn @ w_qkv
    q, k, v = jnp.split(qkv, [B_Q_HEADS * B_HEAD_DIM, (B_Q_HEADS + B_KV_HEADS) * B_HEAD_DIM], axis=-1)
    q = _rms(q.reshape(b, L, B_KV_HEADS, B_GROUP, B_HEAD_DIM), q_gain)
    k = _rms(k.reshape(b, L, B_KV_HEADS, B_HEAD_DIM), k_gain)
    v = v.reshape(b, L, B_KV_HEADS, B_HEAD_DIM)
    return q, k, v


def _sink_attend(q, k, v, bias, sinks):
    s = jnp.einsum('bnqkgd,bnskd->bnkgqs', q, k).astype(jnp.float32) * (B_HEAD_DIM ** -0.5) + bias
    sink = sinks.astype(jnp.float32)[:, :, None]
    m = jnp.maximum(s.max(-1), sink)
    p = jnp.exp(s - m[..., None])
    den = p.sum(-1) + jnp.exp(sink - m)
    w = (p / den[..., None]).astype(v.dtype)
    return jnp.einsum('bnkgqs,bnskd->bnqkgd', w, v)


def _swa_prompt(xn, w_qkv, q_gain, k_gain, sinks, w_o):
    b, L, _ = xn.shape
    n = L // CHUNK
    q, k, v = _qkv(xn, w_qkv, q_gain, k_gain)
    qb = q.reshape(b, n, CHUNK, B_KV_HEADS, B_GROUP, B_HEAD_DIM)

    def band(t):
        tp = jnp.pad(t, ((0, 0), (W_CHUNKS * CHUNK, 0), (0, 0), (0, 0)))
        tp = tp.reshape(b, n + W_CHUNKS, CHUNK, B_KV_HEADS, B_HEAD_DIM)
        return jnp.concatenate([tp[:, w:w + n] for w in range(W_CHUNKS + 1)], axis=2)

    kb, vb = band(k), band(v)
    lk = (W_CHUNKS + 1) * CHUNK
    qi = jnp.arange(CHUNK)
    kj = jnp.arange(lk)
    dist = jnp.abs(W_CHUNKS * CHUNK + qi[:, None] - kj[None, :]).astype(jnp.float32)
    alibi = (-_alibi_slopes()[:, None, None] * dist).reshape(B_KV_HEADS, B_GROUP, CHUNK, lk)
    kpos = (jnp.arange(n)[:, None] - W_CHUNKS) * CHUNK + kj[None, :]
    bias = jnp.where((kpos >= 0)[:, None, None, None, :], alibi, -jnp.inf)
    o = _sink_attend(qb, kb, vb, bias, sinks.reshape(B_KV_HEADS, B_GROUP))
    y = o.reshape(b, L, B_Q_HEADS * B_HEAD_DIM) @ w_o
    keep = min(WINDOW, L)
    return y, k[:, L - keep:], v[:, L - keep:]


def _swa_sample(xn, ck, cv, w_qkv, q_gain, k_gain, sinks, w_o):
    b, L, _ = xn.shape
    q, k, v = _qkv(xn, w_qkv, q_gain, k_gain)
    nc = ck.shape[1]
    k_all = jnp.concatenate([ck.astype(k.dtype), k], axis=1)[:, None]
    v_all = jnp.concatenate([cv.astype(v.dtype), v], axis=1)[:, None]
    qpos = PAST_LEN + jnp.arange(L)
    kpos = jnp.concatenate([PAST_LEN - nc + jnp.arange(nc), qpos])
    gap = qpos[:, None] // CHUNK - kpos[None, :] // CHUNK
    visible = (gap >= 0) & (gap <= W_CHUNKS)
    dist = jnp.abs(qpos[:, None] - kpos[None, :]).astype(jnp.float32)
    alibi = (-_alibi_slopes()[:, None, None] * dist).reshape(B_KV_HEADS, B_GROUP, L, nc + L)
    bias = jnp.where(visible, alibi, -jnp.inf)
    o = _sink_attend(q[:, None], k_all, v_all, bias, sinks.reshape(B_KV_HEADS, B_GROUP))
    y = o.reshape(b, L, B_Q_HEADS * B_HEAD_DIM) @ w_o
    return y, k, v


def _ffn_ple(h, p_i, n_ffn, w_gu, w_down, n_ple, w_ple_proj, w_ple_gate):
    gate, up = jnp.split(_rms(h, n_ffn) @ w_gu, 2, axis=-1)
    h = h + (jax.nn.silu(gate) * up) @ w_down
    g = jax.nn.sigmoid(_rms(h, n_ple) @ w_ple_gate)
    return h + (p_i.astype(h.dtype) @ w_ple_proj) * g


def setup_inputs(seed: int = 0) -> dict:
    key = jax.random.key(seed)
    k = jax.random.split(key, 24)
    f32 = jnp.float32

    def nrm(kk, shape, scale=1.0):
        return jax.random.normal(kk, shape, f32) * scale

    def gain(kk, shape):
        return 1.0 + 0.05 * jax.random.normal(kk, shape, f32)

    n_a = (DEPTH + 1) // 2
    n_b = DEPTH // 2
    n_cache = min(WINDOW, PAST_LEN)
    return {
        'x_prompt': nrm(k[0], (BATCH, SEQ, D_MODEL)),
        'x_sample': nrm(k[1], (DEC_BATCH, DEC_SEQ, D_MODEL)),
        'state_hgrn': nrm(k[2], (n_a, DEC_BATCH, A_HEADS, A_DK, A_DV), 0.5),
        'cache_k': nrm(k[3], (n_b, DEC_BATCH, n_cache, B_KV_HEADS, B_HEAD_DIM)),
        'cache_v': nrm(k[4], (n_b, DEC_BATCH, n_cache, B_KV_HEADS, B_HEAD_DIM)),
        'p_prompt': nrm(k[5], (DEPTH, BATCH, SEQ, PLE_DIM)),
        'p_sample': nrm(k[6], (DEPTH, DEC_BATCH, DEC_SEQ, PLE_DIM)),
        'norm_mix': gain(k[7], (DEPTH, D_MODEL)),
        'norm_ffn': gain(k[8], (DEPTH, D_MODEL)),
        'norm_ple': gain(k[9], (DEPTH, D_MODEL)),
        'a_w_in': nrm(k[10], (n_a, D_MODEL, 2 * A_FD + 2 * A_HEADS * A_DV), D_MODEL ** -0.5),
        'a_lb_logits': nrm(k[11], (DEPTH + 1, A_FD), 0.5),
        'a_g_norm': gain(k[12], (n_a, A_HEADS * A_DV)),
        'a_w_o': nrm(k[13], (n_a, A_HEADS * A_DV, D_MODEL), (A_HEADS * A_DV) ** -0.5),
        'b_w_qkv': nrm(k[14], (n_b, D_MODEL, (B_Q_HEADS + 2 * B_KV_HEADS) * B_HEAD_DIM), D_MODEL ** -0.5),
        'b_q_norm': gain(k[15], (n_b, B_HEAD_DIM)),
        'b_k_norm': gain(k[16], (n_b, B_HEAD_DIM)),
        'b_sinks': nrm(k[17], (n_b, B_Q_HEADS), 0.5),
        'b_w_o': nrm(k[18], (n_b, B_Q_HEADS * B_HEAD_DIM, D_MODEL), (B_Q_HEADS * B_HEAD_DIM) ** -0.5),
        'f_w_gu': nrm(k[19], (DEPTH, D_MODEL, 2 * FF_DIM), D_MODEL ** -0.5),
        'f_w_down': nrm(k[20], (DEPTH, FF_DIM, D_MODEL), FF_DIM ** -0.5),
        'ple_w_proj': nrm(k[21], (DEPTH, PLE_DIM, D_MODEL), PLE_DIM ** -0.5),
        'ple_w_gate': nrm(k[22], (DEPTH, D_MODEL, D_MODEL), D_MODEL ** -0.5),
    }


def reference(x_prompt, x_sample, state_hgrn, cache_k, cache_v, p_prompt, p_sample,
              norm_mix, norm_ffn, norm_ple, a_w_in, a_lb_logits, a_g_norm, a_w_o,
              b_w_qkv, b_q_norm, b_k_norm, b_sinks, b_w_o, f_w_gu, f_w_down,
              ple_w_proj, ple_w_gate):
    lower_bounds = jnp.cumsum(jax.nn.softmax(a_lb_logits.astype(jnp.float32), axis=0), axis=0)
    hp, hs = x_prompt, x_sample
    st_p, st_s, kp_l, vp_l, ks_l, vs_l = [], [], [], [], [], []
    for i in range(DEPTH):
        j = i // N_MIXERS
        xp = _rms(hp, norm_mix[i])
        xs = _rms(hs, norm_mix[i])
        if i % N_MIXERS == 0:
            s0 = jnp.zeros((hp.shape[0], A_HEADS, A_DK, A_DV), jnp.float32)
            mp, sp = _hgrn2(xp, a_w_in[j], lower_bounds[i], a_g_norm[j], a_w_o[j], s0)
            ms, ss = _hgrn2(xs, a_w_in[j], lower_bounds[i], a_g_norm[j], a_w_o[j], state_hgrn[j])
            st_p.append(sp)
            st_s.append(ss)
        else:
            mp, kp, vp = _swa_prompt(xp, b_w_qkv[j], b_q_norm[j], b_k_norm[j], b_sinks[j], b_w_o[j])
            ms, kn, vn = _swa_sample(xs, cache_k[j], cache_v[j], b_w_qkv[j], b_q_norm[j], b_k_norm[j],
                                     b_sinks[j], b_w_o[j])
            kp_l.append(kp)
            vp_l.append(vp)
            ks_l.append(kn)
            vs_l.append(vn)
        hp = hp + mp
        hs = hs + ms
        hp = _ffn_ple(hp, p_prompt[i], norm_ffn[i], f_w_gu[i], f_w_down[i], norm_ple[i], ple_w_proj[i], ple_w_gate[i])
        hs = _ffn_ple(hs, p_sample[i], norm_ffn[i], f_w_gu[i], f_w_down[i], norm_ple[i], ple_w_proj[i], ple_w_gate[i])
    state_hgrn_prompt = jnp.stack(st_p)
    state_hgrn_sample = jnp.stack(st_s)
    cache_k_prompt = jnp.stack(kp_l)
    cache_v_prompt = jnp.stack(vp_l)
    cache_k_sample = jnp.stack(ks_l)
    cache_v_sample = jnp.stack(vs_l)
    return (hp, hs, state_hgrn_prompt, state_hgrn_sample, cache_k_prompt, cache_v_prompt, cache_k_sample, cache_v_sample)
```

```python
import functools

import numpy as np
import jax
import jax.numpy as jnp
from jax import lax
from jax.experimental import pallas as pl
from jax.experimental.pallas import tpu as pltpu

F32 = jnp.float32
BF16 = jnp.bfloat16

RMS_EPS = 1e-6
CHUNK = 64
HEAD_DK = 128
ATT_HD = 64
ATT_GROUP = 8
ATT_PREV = 2
GLA_LEVELS = (64, 32, 16, 8, 4, 2)

NT_DIMS = (((1,), (1,)), ((), ()))
TN_DIMS = (((0,), (0,)), ((), ()))

VMEM_LIMIT_BYTES = 56 * 1024 * 1024


def _params(*sem):
    return pltpu.CompilerParams(dimension_semantics=sem, vmem_limit_bytes=VMEM_LIMIT_BYTES)


def _tile(n, pref):
    t = pref
    while t > 8 and n % t:
        t //= 2
    assert n % t == 0, (n, pref)
    return t


def _sigmoid(x):
    return 1.0 / (1.0 + jnp.exp(-x))


def _rms_rows(x, w):
    ms = jnp.mean(x * x, axis=-1, keepdims=True)
    return x * lax.rsqrt(ms + RMS_EPS) * w


def _dot(a, b):
    return jnp.dot(a, b, preferred_element_type=F32)


def _hgrn_in_kernel(x_ref, nw_ref, wq_ref, wf_ref, wi_ref, wg_ref, lbl_ref,
                    q_ref, k_ref, v_ref, lf_ref, gs_ref, xn_ref):
    @pl.when(pl.program_id(1) == 0)
    def _():
        xn_ref[...] = _rms_rows(x_ref[...], nw_ref[...]).astype(BF16)

    xn = xn_ref[...]
    q = _dot(xn, wq_ref[...])
    q_ref[...] = (q * _sigmoid(q)).astype(BF16)
    logits = lbl_ref[...]
    ex = jnp.exp(logits - jnp.max(logits, axis=0, keepdims=True))
    lb = ex[0:1] / jnp.sum(ex, axis=0, keepdims=True)
    f = _dot(xn, wf_ref[...])
    forget = lb + (1.0 - lb) * _sigmoid(f)
    k_ref[...] = (1.0 - forget).astype(BF16)
    lf_ref[...] = jnp.log(forget)
    v_ref[...] = _dot(xn, wi_ref[...]).astype(BF16)
    g = _dot(xn, wg_ref[...])
    gs_ref[...] = (g * _sigmoid(g)).astype(BF16)


def _hgrn_in(h, norm_w, w_in, lb_logits):
    T, D = h.shape
    tm = _tile(T, 1024)
    tn = 256
    nb = D // tn
    row = pl.BlockSpec((tm, D), lambda i, j: (i, 0))
    out = pl.BlockSpec((tm, tn), lambda i, j: (i, j))

    def wspec(g):
        return pl.BlockSpec((D, tn), lambda i, j, g=g: (0, j + g * nb))

    return pl.pallas_call(
        _hgrn_in_kernel,
        grid=(T // tm, nb),
        in_specs=[row, pl.BlockSpec((1, D), lambda i, j: (0, 0)),
                  wspec(0), wspec(1), wspec(2), wspec(3),
                  pl.BlockSpec((lb_logits.shape[0], tn), lambda i, j: (0, j))],
        out_specs=[out] * 5,
        out_shape=[jax.ShapeDtypeStruct((T, D), BF16)] * 2
        + [jax.ShapeDtypeStruct((T, D), BF16), jax.ShapeDtypeStruct((T, D), F32),
           jax.ShapeDtypeStruct((T, D), BF16)],
        scratch_shapes=[pltpu.VMEM((tm, D), BF16)],
        compiler_params=_params("parallel", "arbitrary"),
        name="hgrn_in",
    )(h, norm_w, w_in, w_in, w_in, w_in, lb_logits)


def _gla_constants():
    c = CHUNK
    t = np.arange(c)[:, None]
    s = np.arange(c)[None, :]
    zs = [s <= t, s > t]
    level = np.full((c, c), -1, np.int32)
    for li, p in enumerate(GLA_LEVELS, start=1):
        half = p // 2
        mid = (t // p) * p + half - 1
        upper = (t % p) >= half
        zs.append(np.where(upper, (s > mid) & (s <= t), (s > t) & (s <= mid)))
        level[(t // p == s // p) & upper & ((s % p) < half)] = li
    np.fill_diagonal(level, 0)
    z = np.concatenate(zs, axis=0).astype(np.float32)
    return np.concatenate([z, z, z], axis=1), level


def _gla_kernel(q_ref, k_ref, v_ref, lf_ref, s0_ref, z_ref, lvl_ref, o_ref, sout_ref,
                st_ref, e_ref, *, n_prompt_chunks, chunks_per_seq, n_heads):
    i = pl.program_id(0)
    is_prompt = i < n_prompt_chunks
    c = jnp.where(is_prompt, lax.rem(i, chunks_per_seq), 0)
    last = jnp.logical_or(jnp.logical_not(is_prompt), c == chunks_per_seq - 1)

    @pl.when(jnp.logical_and(is_prompt, c == 0))
    def _():
        st_ref[...] = jnp.zeros_like(st_ref)

    @pl.when(jnp.logical_not(is_prompt))
    def _():
        for h in range(n_heads):
            st_ref[h] = s0_ref[0, h].T

    g = lf_ref[...]
    g_hi = g.astype(BF16)
    r1 = g - g_hi.astype(F32)
    g_mid = r1.astype(BF16)
    g_lo = (r1 - g_mid.astype(F32)).astype(BF16)
    gcat = jnp.concatenate([g_hi, g_mid, g_lo], axis=0)
    e_ref[...] = jnp.exp(_dot(z_ref[...], gcat))

    row = lax.broadcasted_iota(jnp.int32, (CHUNK, HEAD_DK), 0)
    lvl = lvl_ref[...]
    for h in range(n_heads):
        sl = slice(h * HEAD_DK, (h + 1) * HEAD_DK)
        qb = q_ref[:, sl]
        kb = k_ref[:, sl]
        vb = v_ref[:, sl]
        qf = qb.astype(F32)
        kf = kb.astype(F32)
        st = st_ref[h]
        q_hat = (qf * e_ref[0:CHUNK, sl]).astype(BF16)
        k_hat = (kf * e_ref[CHUNK:2 * CHUNK, sl]).astype(BF16)
        o = lax.dot_general(q_hat, st.astype(BF16), NT_DIMS, preferred_element_type=F32)
        a = lax.dot_general(qb, kb, NT_DIMS, preferred_element_type=F32)
        a = jnp.where(lvl == 0, a, 0.0)
        for li, p in enumerate(GLA_LEVELS, start=1):
            on_q_side = (row & (p // 2)) != 0
            x = (jnp.where(on_q_side, qf, kf) * e_ref[(li + 1) * CHUNK:(li + 2) * CHUNK, sl]).astype(BF16)
            a = jnp.where(lvl == li, lax.dot_general(x, x, NT_DIMS, preferred_element_type=F32), a)
        o_ref[:, sl] = o + _dot(a.astype(BF16), vb)
        decay_all = e_ref[CHUNK - 1:CHUNK, sl]
        st_ref[h] = st * decay_all + lax.dot_general(vb, k_hat, TN_DIMS, preferred_element_type=F32)

    @pl.when(last)
    def _():
        for h in range(n_heads):
            sout_ref[0, h] = st_ref[h].T


def _gla(q, k, v, lf, s0, n_prompt_seq, chunks_per_seq):
    T, D = q.shape
    n_heads = D // HEAD_DK
    n_chunks = T // CHUNK
    ncp = n_prompt_seq * chunks_per_seq
    n_seq = n_prompt_seq + (n_chunks - ncp)
    zcat, level = _gla_constants()
    blk = pl.BlockSpec((CHUNK, D), lambda i: (i, 0))
    st_blk = (1, n_heads, HEAD_DK, HEAD_DK)

    def seq_of(i):
        return jnp.where(i < ncp, i // chunks_per_seq, i - ncp + n_prompt_seq)

    kern = functools.partial(_gla_kernel, n_prompt_chunks=ncp, chunks_per_seq=chunks_per_seq,
                             n_heads=n_heads)
    return pl.pallas_call(
        kern,
        grid=(n_chunks,),
        in_specs=[blk, blk, blk, blk,
                  pl.BlockSpec(st_blk, lambda i: (jnp.maximum(i - ncp, 0), 0, 0, 0)),
                  pl.BlockSpec(zcat.shape, lambda i: (0, 0)),
                  pl.BlockSpec(level.shape, lambda i: (0, 0))],
        out_specs=[blk, pl.BlockSpec(st_blk, lambda i: (seq_of(i), 0, 0, 0))],
        out_shape=[jax.ShapeDtypeStruct((T, D), F32),
                   jax.ShapeDtypeStruct((n_seq, n_heads, HEAD_DK, HEAD_DK), F32)],
        scratch_shapes=[pltpu.VMEM((n_heads, HEAD_DK, HEAD_DK), F32),
                        pltpu.VMEM((zcat.shape[0], D), F32)],
        compiler_params=_params("arbitrary"),
        name="gla_scan",
    )(q, k, v, lf, s0, jnp.asarray(zcat, BF16), jnp.asarray(level))


def _out_proj_kernel(*refs, gated):
    if gated:
        o_ref, gs_ref, gn_ref, w_ref, h_ref, nf_ref, h1_ref, xn_ref = refs
        a = (_rms_rows(o_ref[...], gn_ref[...]) * gs_ref[...].astype(F32)).astype(BF16)
    else:
        a_ref, w_ref, h_ref, nf_ref, h1_ref, xn_ref = refs
        a = a_ref[...]
    h1 = h_ref[...] + _dot(a, w_ref[...])
    h1_ref[...] = h1
    xn_ref[...] = _rms_rows(h1, nf_ref[...]).astype(BF16)


def _out_proj(a_inputs, w, h, norm_ffn, gated):
    T, D = h.shape
    tm = _tile(T, 256)
    row = pl.BlockSpec((tm, D), lambda i: (i, 0))
    vec = pl.BlockSpec((1, D), lambda i: (0, 0))
    wspec = pl.BlockSpec((D, D), lambda i: (0, 0))
    if gated:
        in_specs = [row, row, vec, wspec, row, vec]
    else:
        in_specs = [row, wspec, row, vec]
    return pl.pallas_call(
        functools.partial(_out_proj_kernel, gated=gated),
        grid=(T // tm,),
        in_specs=in_specs,
        out_specs=[row, row],
        out_shape=[jax.ShapeDtypeStruct((T, D), F32), jax.ShapeDtypeStruct((T, D), BF16)],
        compiler_params=_params("parallel"),
        name="out_proj",
    )(*a_inputs, w, h, norm_ffn)


def _ffn_kernel(xn_ref, h_ref, wg_ref, wu_ref, wd_ref, np_ref, h2_ref, xn2_ref, acc_ref):
    j = pl.program_id(1)
    xn = xn_ref[...]
    gate = _dot(xn, wg_ref[...])
    up = _dot(xn, wu_ref[...])
    part = _dot((gate * _sigmoid(gate) * up).astype(BF16), wd_ref[...])

    @pl.when(j == 0)
    def _():
        acc_ref[...] = part

    @pl.when(j > 0)
    def _():
        acc_ref[...] += part

    @pl.when(j == pl.num_programs(1) - 1)
    def _():
        h2 = h_ref[...] + acc_ref[...]
        h2_ref[...] = h2
        xn2_ref[...] = _rms_rows(h2, np_ref[...]).astype(BF16)


def _ffn(xn, h, w_gu, w_down, norm_ple):
    T, D = h.shape
    ff = w_down.shape[0]
    tm = _tile(T, 512)
    tf = 512
    nf = ff // tf
    row = pl.BlockSpec((tm, D), lambda i, j: (i, 0))
    return pl.pallas_call(
        _ffn_kernel,
        grid=(T // tm, nf),
        in_specs=[row, row,
                  pl.BlockSpec((D, tf), lambda i, j: (0, j)),
                  pl.BlockSpec((D, tf), lambda i, j: (0, j + nf)),
                  pl.BlockSpec((tf, D), lambda i, j: (j, 0)),
                  pl.BlockSpec((1, D), lambda i, j: (0, 0))],
        out_specs=[row, row],
        out_shape=[jax.ShapeDtypeStruct((T, D), F32), jax.ShapeDtypeStruct((T, D), BF16)],
        scratch_shapes=[pltpu.VMEM((tm, D), F32)],
        compiler_params=_params("parallel", "arbitrary"),
        name="ffn",
    )(xn, h, w_gu, w_gu, w_down, norm_ple)


def _ple_kernel(*refs, emit_norm):
    if emit_norm:
        xn_ref, h_ref, p_ref, wg_ref, wp_ref, nm_ref, h3_ref, xn3_ref = refs
    else:
        xn_ref, h_ref, p_ref, wg_ref, wp_ref, h3_ref = refs
    gate = _sigmoid(_dot(xn_ref[...], wg_ref[...]))
    proj = _dot(p_ref[...].astype(BF16), wp_ref[...])
    h3 = h_ref[...] + proj * gate
    h3_ref[...] = h3
    if emit_norm:
        xn3_ref[...] = _rms_rows(h3, nm_ref[...]).astype(BF16)


def _ple(xn, h, p, w_gate, w_proj, next_norm):
    T, D = h.shape
    pd = p.shape[1]
    tm = _tile(T, 256)
    row = pl.BlockSpec((tm, D), lambda i: (i, 0))
    emit_norm = next_norm is not None
    in_specs = [row, row, pl.BlockSpec((tm, pd), lambda i: (i, 0)),
                pl.BlockSpec((D, D), lambda i: (0, 0)), pl.BlockSpec((pd, D), lambda i: (0, 0))]
    args = [xn, h, p, w_gate, w_proj]
    out_specs = [row]
    out_shape = [jax.ShapeDtypeStruct((T, D), F32)]
    if emit_norm:
        in_specs.append(pl.BlockSpec((1, D), lambda i: (0, 0)))
        args.append(next_norm)
        out_specs.append(row)
        out_shape.append(jax.ShapeDtypeStruct((T, D), BF16))
    return pl.pallas_call(
        functools.partial(_ple_kernel, emit_norm=emit_norm),
        grid=(T // tm,),
        in_specs=in_specs,
        out_specs=out_specs,
        out_shape=out_shape,
        compiler_params=_params("parallel"),
        name="ple",
    )(*args)


def _qkv_kernel(xn_ref, w_ref, seg_ref, gain_ref, mask_ref, out_ref):
    x = _dot(xn_ref[...], w_ref[...])
    sq = x * x
    sq_hi = sq.astype(BF16)
    sq_lo = (sq - sq_hi.astype(F32)).astype(BF16)
    seg = seg_ref[...]
    ms = (_dot(sq_hi, seg) + _dot(sq_lo, seg)) * (1.0 / ATT_HD)
    normed = x * lax.rsqrt(ms + RMS_EPS) * gain_ref[0]
    out_ref[...] = jnp.where(mask_ref[0] > 0.5, normed, x)


def _qkv(xn, w_qkv, q_gain, k_gain):
    T, D = xn.shape
    n_out = w_qkv.shape[1]
    tn = ATT_GROUP * ATT_HD
    n_tiles = n_out // tn
    n_q_tiles = D // tn
    assert n_tiles == n_q_tiles + 1
    kv_w = (n_out - D) // 2
    tm = _tile(T, 512)
    seg = np.kron(np.eye(tn // ATT_HD, dtype=np.float32), np.ones((ATT_HD, ATT_HD), np.float32))
    q_row = jnp.tile(q_gain.astype(F32), tn // ATT_HD)
    kv_row = jnp.concatenate([jnp.tile(k_gain.astype(F32), kv_w // ATT_HD), jnp.ones((tn - kv_w,), F32)])
    gain = jnp.stack([q_row] * n_q_tiles + [kv_row])[:, None, :]
    mask_np = np.ones((n_tiles, 1, tn), np.float32)
    mask_np[-1, 0, kv_w:] = 0.0
    return pl.pallas_call(
        _qkv_kernel,
        grid=(T // tm, n_tiles),
        in_specs=[pl.BlockSpec((tm, D), lambda i, j: (i, 0)),
                  pl.BlockSpec((D, tn), lambda i, j: (0, j)),
                  pl.BlockSpec((tn, tn), lambda i, j: (0, 0)),
                  pl.BlockSpec((1, 1, tn), lambda i, j: (j, 0, 0)),
                  pl.BlockSpec((1, 1, tn), lambda i, j: (j, 0, 0))],
        out_specs=pl.BlockSpec((tm, tn), lambda i, j: (i, j)),
        out_shape=jax.ShapeDtypeStruct((T, n_out), F32),
        compiler_params=_params("parallel", "parallel"),
        name="qkv_proj",
    )(xn, w_qkv, jnp.asarray(seg, BF16), gain, jnp.asarray(mask_np))


def _alibi_slopes(n_heads):
    return [float(np.float32(2.0) ** np.float32(-8.0 * (i + 1) / n_heads)) for i in range(n_heads)]


def _attn_kernel(sink_ref, q_ref, kvc_ref, kv1_ref, kv2_ref, ck_ref, cv_ref, o_ref,
                 *, n_prompt_chunks, chunks_per_seq, n_kv_heads):
    i = pl.program_id(0)
    is_sample = i >= n_prompt_chunks
    c = lax.rem(i, chunks_per_seq)
    first_valid = jnp.where(is_sample, 0, jnp.maximum(ATT_PREV - c, 0) * CHUNK)
    kvw = n_kv_heads * ATT_HD
    n_keys = (ATT_PREV + 1) * CHUNK
    slopes = _alibi_slopes(n_kv_heads * ATT_GROUP)

    kvc = kvc_ref[...]
    kv1 = kv1_ref[...]
    kv2 = kv2_ref[...]
    k_prev = jnp.where(is_sample, ck_ref[0], jnp.concatenate([kv2[:, :kvw], kv1[:, :kvw]], axis=0))
    v_prev = jnp.where(is_sample, cv_ref[0], jnp.concatenate([kv2[:, kvw:], kv1[:, kvw:]], axis=0))
    k_all = jnp.concatenate([k_prev, kvc[:, :kvw]], axis=0)
    v_all = jnp.concatenate([v_prev, kvc[:, kvw:]], axis=0)

    low = lax.broadcasted_iota(jnp.int32, (n_keys, 2 * ATT_HD), 1) < ATT_HD
    qi = lax.broadcasted_iota(jnp.int32, (CHUNK, n_keys), 0)
    kj = lax.broadcasted_iota(jnp.int32, (CHUNK, n_keys), 1)
    dist = jnp.abs(ATT_PREV * CHUNK + qi - kj).astype(F32)
    valid = kj >= first_valid
    pairs = ATT_GROUP // 2
    qw = ATT_GROUP * ATT_HD

    def split_halves(t, odd):
        rolled = pltpu.roll(t, ATT_HD, 1)
        in_low, in_high = (rolled, t) if odd else (t, rolled)
        zero = jnp.zeros_like(t)
        return jnp.where(low, in_low, zero).astype(BF16), jnp.where(low, zero, in_high).astype(BF16)

    for kh in range(n_kv_heads):
        col = (kh // 2) * 2 * ATT_HD
        k_lo, k_hi = split_halves(k_all[:, col:col + 2 * ATT_HD], kh % 2)
        v_lo, v_hi = split_halves(v_all[:, col:col + 2 * ATT_HD], kh % 2)
        qs = jnp.concatenate(
            [q_ref[:, kh * qw + p * 2 * ATT_HD: kh * qw + (p + 1) * 2 * ATT_HD] for p in range(pairs)],
            axis=0).astype(BF16)
        acc = None
        for parity, (kx, vx) in enumerate(((k_lo, v_lo), (k_hi, v_hi))):
            s_all = lax.dot_general(qs, kx, NT_DIMS, preferred_element_type=F32) * (ATT_HD ** -0.5)
            probs, inv = [], []
            for p in range(pairs):
                head = kh * ATT_GROUP + 2 * p + parity
                s = s_all[p * CHUNK:(p + 1) * CHUNK] + dist * (-slopes[head])
                s = jnp.where(valid, s, -jnp.inf)
                sink = sink_ref[head]
                m = jnp.maximum(jnp.max(s, axis=-1, keepdims=True), sink)
                e = jnp.exp(s - m)
                den = jnp.sum(e, axis=-1, keepdims=True) + jnp.exp(sink - m)
                probs.append(e.astype(BF16))
                inv.append(1.0 / den)
            part = _dot(jnp.concatenate(probs, axis=0), vx) * jnp.concatenate(inv, axis=0)
            acc = part if acc is None else acc + part
        for p in range(pairs):
            o_ref[:, kh * qw + p * 2 * ATT_HD: kh * qw + (p + 1) * 2 * ATT_HD] = (
                acc[p * CHUNK:(p + 1) * CHUNK].astype(BF16))


def _attention(qkvn, cache_k, cache_v, sinks, d_model, n_prompt_seq, chunks_per_seq):
    T = qkvn.shape[0]
    n_chunks = T // CHUNK
    ncp = n_prompt_seq * chunks_per_seq
    kvw2 = qkvn.shape[1] - d_model
    n_kv_heads = kvw2 // (2 * ATT_HD)
    kv_col = d_model // kvw2
    assert d_model % kvw2 == 0
    n_cache = cache_k.shape[1]
    assert n_cache == ATT_PREV * CHUNK

    def prev_map(back):
        def index(i):
            ok = jnp.logical_and(i < ncp, lax.rem(i, chunks_per_seq) >= back)
            return (jnp.where(ok, i - back, i), kv_col)
        return index

    kv_blk = (CHUNK, kvw2)
    cache_spec = pl.BlockSpec((1, n_cache, kvw2 // 2), lambda i: (jnp.maximum(i - ncp, 0), 0, 0))
    kern = functools.partial(_attn_kernel, n_prompt_chunks=ncp, chunks_per_seq=chunks_per_seq,
                             n_kv_heads=n_kv_heads)
    return pl.pallas_call(
        kern,
        grid=(n_chunks,),
        in_specs=[pl.BlockSpec(memory_space=pltpu.SMEM),
                  pl.BlockSpec((CHUNK, d_model), lambda i: (i, 0)),
                  pl.BlockSpec(kv_blk, lambda i: (i, kv_col)),
                  pl.BlockSpec(kv_blk, prev_map(1)),
                  pl.BlockSpec(kv_blk, prev_map(2)),
                  cache_spec, cache_spec],
        out_specs=pl.BlockSpec((CHUNK, d_model), lambda i: (i, 0)),
        out_shape=jax.ShapeDtypeStruct((T, d_model), BF16),
        compiler_params=_params("parallel"),
        name="swa_attention",
    )(sinks, qkvn, qkvn, qkvn, qkvn, cache_k, cache_v)


def kernel(x_prompt, x_sample, state_hgrn, cache_k, cache_v, p_prompt, p_sample, norm_mix, norm_ffn,
           norm_ple, a_w_in, a_lb_logits, a_g_norm, a_w_o, b_w_qkv, b_q_norm, b_k_norm, b_sinks, b_w_o,
           f_w_gu, f_w_down, ple_w_proj, ple_w_gate):
    B, L, D = x_prompt.shape
    Bs, Ls, _ = x_sample.shape
    assert L % CHUNK == 0 and Ls == CHUNK
    tp, ts = B * L, Bs * Ls
    cps = L // CHUNK
    depth = norm_mix.shape[0]
    assert depth == 2

    h = jnp.concatenate([x_prompt.reshape(tp, D), x_sample.reshape(ts, D)], axis=0)
    p_all = jnp.concatenate([p_prompt.reshape(depth, tp, -1), p_sample.reshape(depth, ts, -1)], axis=1)
    bf = lambda w: w.astype(BF16)
    vec = lambda w: w.reshape(1, -1).astype(F32)

    def ffn_ple(layer, h1, xn1, next_norm):
        h2, xn2 = _ffn(xn1, h1, bf(f_w_gu[layer]), bf(f_w_down[layer]), vec(norm_ple[layer]))
        return _ple(xn2, h2, p_all[layer], bf(ple_w_gate[layer]), bf(ple_w_proj[layer]), next_norm)

    q, k, v, lf, gs = _hgrn_in(h, vec(norm_mix[0]), bf(a_w_in[0]), a_lb_logits.astype(F32))
    o, states = _gla(q, k, v, lf, state_hgrn[0].astype(F32), B, cps)
    h1, xn1 = _out_proj((o, gs, vec(a_g_norm[0])), bf(a_w_o[0]), h, vec(norm_ffn[0]), gated=True)
    h, xn = ffn_ple(0, h1, xn1, vec(norm_mix[1]))

    qkvn = _qkv(xn, bf(b_w_qkv[0]), b_q_norm[0], b_k_norm[0])
    n_kv = cache_k.shape[3]
    kvw = n_kv * ATT_HD
    att = _attention(qkvn, cache_k[0].reshape(Bs, -1, kvw).astype(F32),
                     cache_v[0].reshape(Bs, -1, kvw).astype(F32), b_sinks[0].astype(F32), D, B, cps)
    h1, xn1 = _out_proj((att,), bf(b_w_o[0]), h, vec(norm_ffn[1]), gated=False)
    (h,) = ffn_ple(1, h1, xn1, None)

    keep = min(ATT_PREV * CHUNK, L)
    kn = qkvn[:, D:D + kvw]
    vn = qkvn[:, D + kvw:]
    heads = lambda t, b, l: t.reshape(b, l, n_kv, ATT_HD)
    return (h[:tp].reshape(B, L, D),
            h[tp:].reshape(Bs, Ls, D),
            states[:B][None],
            states[B:][None],
            heads(kn[:tp], B, L)[:, L - keep:][None],
            heads(vn[:tp], B, L)[:, L - keep:][None],
            heads(kn[tp:], Bs, Ls)[None],
            heads(vn[tp:], Bs, Ls)[None])
```

```python
import functools

import numpy as np
import jax
import jax.numpy as jnp
from jax import lax
from jax.experimental import pallas as pl
from jax.experimental.pallas import tpu as pltpu

F32 = jnp.float32
BF16 = jnp.bfloat16

RMS_EPS = 1e-6
CHUNK = 64
HEAD_DK = 128
ATT_HD = 64
ATT_GROUP = 8
ATT_PREV = 2
GLA_LEVELS = (64, 32, 16, 8, 4, 2)
GLA_SAFE_EXP = 60.0
ATT_SAFE_EXP = 70.0

NT_DIMS = (((1,), (1,)), ((), ()))
TN_DIMS = (((0,), (0,)), ((), ()))

VMEM_LIMIT_BYTES = 56 * 1024 * 1024


def _params(*sem):
    return pltpu.CompilerParams(dimension_semantics=sem, vmem_limit_bytes=VMEM_LIMIT_BYTES)


def _tile(n, pref):
    t = pref
    while t > 8 and n % t:
        t //= 2
    assert n % t == 0, (n, pref)
    return t


def _split_rows(tm, n_p, width, sample_buffers=None):
    mode = {} if sample_buffers is None else dict(pipeline_mode=pl.Buffered(sample_buffers))
    return (pl.BlockSpec((tm, width), lambda i, *_: (jnp.minimum(i, n_p - 1), 0)),
            pl.BlockSpec((tm, width), lambda i, *_: (jnp.maximum(i - n_p, 0), 0), **mode))


def _sigmoid(x):
    return 1.0 / (1.0 + jnp.exp(-x))


def _rms_rows(x, w):
    ms = jnp.mean(x * x, axis=-1, keepdims=True)
    return x * lax.rsqrt(ms + RMS_EPS) * w


def _dot(a, b):
    return jnp.dot(a, b, preferred_element_type=F32)


def _hgrn_in_kernel(xp_ref, xs_ref, nw_ref, wq_ref, wf_ref, wi_ref, wg_ref, lbl_ref,
                    q_ref, k_ref, v_ref, lf_ref, gs_ref, xn_ref, *, n_p):
    first = pl.program_id(1) == 0
    is_prompt = pl.program_id(0) < n_p

    @pl.when(jnp.logical_and(first, is_prompt))
    def _():
        xn_ref[...] = _rms_rows(xp_ref[...], nw_ref[...]).astype(BF16)

    @pl.when(jnp.logical_and(first, jnp.logical_not(is_prompt)))
    def _():
        xn_ref[...] = _rms_rows(xs_ref[...], nw_ref[...]).astype(BF16)

    xn = xn_ref[...]
    q = _dot(xn, wq_ref[...])
    q_ref[...] = (q * _sigmoid(q)).astype(BF16)
    logits = lbl_ref[...]
    ex = jnp.exp(logits - jnp.max(logits, axis=0, keepdims=True))
    lb = ex[0:1] / jnp.sum(ex, axis=0, keepdims=True)
    f = _dot(xn, wf_ref[...])
    forget = lb + (1.0 - lb) * _sigmoid(f)
    k_ref[...] = (1.0 - forget).astype(BF16)
    lf_ref[...] = jnp.log(forget)
    v_ref[...] = _dot(xn, wi_ref[...]).astype(BF16)
    g = _dot(xn, wg_ref[...])
    gs_ref[...] = (g * _sigmoid(g)).astype(BF16)


def _hgrn_in(xp, xs, norm_w, w_in, lb_logits):
    (tp, D), ts = xp.shape, xs.shape[0]
    T = tp + ts
    tm = min(_tile(tp, 1024), _tile(ts, 1024))
    tn = 256
    nb = D // tn
    out = pl.BlockSpec((tm, tn), lambda i, j: (i, j))

    def wspec(g):
        return pl.BlockSpec((D, tn), lambda i, j, g=g: (0, j + g * nb))

    return pl.pallas_call(
        functools.partial(_hgrn_in_kernel, n_p=tp // tm),
        grid=(T // tm, nb),
        in_specs=[*_split_rows(tm, tp // tm, D, sample_buffers=1), pl.BlockSpec((1, D), lambda i, j: (0, 0)),
                  wspec(0), wspec(1), wspec(2), wspec(3),
                  pl.BlockSpec((lb_logits.shape[0], tn), lambda i, j: (0, j))],
        out_specs=[out] * 5,
        out_shape=[jax.ShapeDtypeStruct((T, D), BF16)] * 2
        + [jax.ShapeDtypeStruct((T, D), BF16), jax.ShapeDtypeStruct((T, D), F32),
           jax.ShapeDtypeStruct((T, D), BF16)],
        scratch_shapes=[pltpu.VMEM((tm, D), BF16)],
        compiler_params=_params("parallel", "arbitrary"),
        name="hgrn_in",
    )(xp, xs, norm_w, w_in, w_in, w_in, w_in, lb_logits)


def _gla_constants():
    c = CHUNK
    t = np.arange(c)[:, None]
    s = np.arange(c)[None, :]
    zs = [s <= t, s > t]
    level = np.full((c, c), -1, np.int32)
    for li, p in enumerate(GLA_LEVELS, start=1):
        half = p // 2
        mid = (t // p) * p + half - 1
        upper = (t % p) >= half
        zs.append(np.where(upper, (s > mid) & (s <= t), (s > t) & (s <= mid)))
        level[(t // p == s // p) & upper & ((s % p) < half)] = li
    np.fill_diagonal(level, 0)
    z = np.concatenate(zs, axis=0).astype(np.float32)
    zf = np.concatenate([zs[0], zs[0]], axis=0).astype(np.float32)
    zf[c:] *= -1.0
    three = lambda m: np.concatenate([m, m, m], axis=1)
    return three(z), three(zf), level


def _gla_kernel(q_ref, k_ref, v_ref, lf_ref, s0_ref, z_ref, zf_ref, lvl_ref, o_ref, sp_ref, ss_ref,
                st_ref, e_ref, *, n_prompt_chunks, chunks_per_seq, n_heads):
    i = pl.program_id(0)
    is_prompt = i < n_prompt_chunks
    c = jnp.where(is_prompt, lax.rem(i, chunks_per_seq), 0)

    @pl.when(jnp.logical_and(is_prompt, c == 0))
    def _():
        st_ref[...] = jnp.zeros_like(st_ref)

    @pl.when(jnp.logical_not(is_prompt))
    def _():
        for h in range(n_heads):
            st_ref[h] = s0_ref[0, h].T

    g = lf_ref[...]
    g_hi = g.astype(BF16)
    r1 = g - g_hi.astype(F32)
    g_mid = r1.astype(BF16)
    g_lo = (r1 - g_mid.astype(F32)).astype(BF16)
    gcat = jnp.concatenate([g_hi, g_mid, g_lo], axis=0)
    lvl = lvl_ref[...]

    mild = jnp.min(jnp.sum(g, axis=0, keepdims=True)) >= -GLA_SAFE_EXP

    @pl.when(mild)
    def _():
        e_ref[0:2 * CHUNK, :] = jnp.exp(_dot(zf_ref[...], gcat))
        sls = [slice(h * HEAD_DK, (h + 1) * HEAD_DK) for h in range(n_heads)]
        q_hat = [(q_ref[:, sl].astype(F32) * e_ref[0:CHUNK, sl]).astype(BF16) for sl in sls]
        k_bar = [(k_ref[:, sl].astype(F32) * e_ref[CHUNK:2 * CHUNK, sl]).astype(BF16) for sl in sls]
        a = [lax.dot_general(q_hat[h], k_bar[h], NT_DIMS, preferred_element_type=F32) for h in range(n_heads)]
        o = [lax.dot_general(q_hat[h], st_ref[h].astype(BF16), NT_DIMS, preferred_element_type=F32)
             for h in range(n_heads)]
        for h, sl in enumerate(sls):
            o_ref[:, sl] = o[h] + _dot(jnp.where(lvl >= 0, a[h], 0.0).astype(BF16), v_ref[:, sl])
        for h, sl in enumerate(sls):
            decay_all = e_ref[CHUNK - 1:CHUNK, sl]
            k_hat = (k_ref[:, sl].astype(F32) * (e_ref[CHUNK:2 * CHUNK, sl] * decay_all)).astype(BF16)
            st_ref[h] = st_ref[h] * decay_all + lax.dot_general(v_ref[:, sl], k_hat, TN_DIMS,
                                                                 preferred_element_type=F32)

    @pl.when(jnp.logical_not(mild))
    def _():
        e_ref[...] = jnp.exp(_dot(z_ref[...], gcat))
        row = lax.broadcasted_iota(jnp.int32, (CHUNK, HEAD_DK), 0)
        for h in range(n_heads):
            sl = slice(h * HEAD_DK, (h + 1) * HEAD_DK)
            qb = q_ref[:, sl]
            kb = k_ref[:, sl]
            vb = v_ref[:, sl]
            qf = qb.astype(F32)
            kf = kb.astype(F32)
            st = st_ref[h]
            q_hat = (qf * e_ref[0:CHUNK, sl]).astype(BF16)
            k_hat = (kf * e_ref[CHUNK:2 * CHUNK, sl]).astype(BF16)
            o = lax.dot_general(q_hat, st.astype(BF16), NT_DIMS, preferred_element_type=F32)
            a = lax.dot_general(qb, kb, NT_DIMS, preferred_element_type=F32)
            a = jnp.where(lvl == 0, a, 0.0)
            for li, p in enumerate(GLA_LEVELS, start=1):
                on_q_side = (row & (p // 2)) != 0
                x = (jnp.where(on_q_side, qf, kf)
                     * e_ref[(li + 1) * CHUNK:(li + 2) * CHUNK, sl]).astype(BF16)
                a = jnp.where(lvl == li, lax.dot_general(x, x, NT_DIMS, preferred_element_type=F32), a)
            o_ref[:, sl] = o + _dot(a.astype(BF16), vb)
            decay_all = e_ref[CHUNK - 1:CHUNK, sl]
            st_ref[h] = st * decay_all + lax.dot_general(vb, k_hat, TN_DIMS, preferred_element_type=F32)

    @pl.when(jnp.logical_and(is_prompt, c == chunks_per_seq - 1))
    def _():
        for h in range(n_heads):
            sp_ref[0, h] = st_ref[h].T

    @pl.when(jnp.logical_not(is_prompt))
    def _():
        for h in range(n_heads):
            ss_ref[0, h] = st_ref[h].T


def _gla(q, k, v, lf, s0, n_prompt_seq, chunks_per_seq):
    T, D = q.shape
    n_heads = D // HEAD_DK
    n_chunks = T // CHUNK
    ncp = n_prompt_seq * chunks_per_seq
    zcat, zfast, level = _gla_constants()
    blk = pl.BlockSpec((CHUNK, D), lambda i: (i, 0))
    st_blk = (1, n_heads, HEAD_DK, HEAD_DK)

    def prompt_seq(i):
        return (jnp.minimum(i // chunks_per_seq, n_prompt_seq - 1), 0, 0, 0)

    kern = functools.partial(_gla_kernel, n_prompt_chunks=ncp, chunks_per_seq=chunks_per_seq,
                             n_heads=n_heads)
    return pl.pallas_call(
        kern,
        grid=(n_chunks,),
        in_specs=[blk, blk, blk, blk,
                  pl.BlockSpec(st_blk, lambda i: (jnp.maximum(i - ncp, 0), 0, 0, 0)),
                  pl.BlockSpec(zcat.shape, lambda i: (0, 0)),
                  pl.BlockSpec(zfast.shape, lambda i: (0, 0)),
                  pl.BlockSpec(level.shape, lambda i: (0, 0))],
        out_specs=[blk, pl.BlockSpec(st_blk, prompt_seq),
                   pl.BlockSpec(st_blk, lambda i: (jnp.maximum(i - ncp, 0), 0, 0, 0))],
        out_shape=[jax.ShapeDtypeStruct((T, D), F32),
                   jax.ShapeDtypeStruct((n_prompt_seq, n_heads, HEAD_DK, HEAD_DK), F32),
                   jax.ShapeDtypeStruct((n_chunks - ncp, n_heads, HEAD_DK, HEAD_DK), F32)],
        scratch_shapes=[pltpu.VMEM((n_heads, HEAD_DK, HEAD_DK), F32),
                        pltpu.VMEM((zcat.shape[0], D), F32)],
        compiler_params=_params("arbitrary"),
        name="gla_scan",
    )(q, k, v, lf, s0, jnp.asarray(zcat, BF16), jnp.asarray(zfast, BF16), jnp.asarray(level))


def _out_proj_kernel(*refs, gated, n_p):
    if gated:
        o_ref, gs_ref, gn_ref, w_ref, *h_refs, nf_ref, h1_ref, xn_ref = refs
        a = (_rms_rows(o_ref[...], gn_ref[...]) * gs_ref[...].astype(F32)).astype(BF16)
    else:
        a_ref, w_ref, *h_refs, nf_ref, h1_ref, xn_ref = refs
        a = a_ref[...]
    if len(h_refs) == 2:
        h = jnp.where(pl.program_id(0) < n_p, h_refs[0][...], h_refs[1][...])
    else:
        h = h_refs[0][...]
    h1 = h + _dot(a, w_ref[...])
    h1_ref[...] = h1
    xn_ref[...] = _rms_rows(h1, nf_ref[...]).astype(BF16)


def _out_proj(a_inputs, w, hs, norm_ffn, gated):
    T, D = a_inputs[0].shape
    tm = min(_tile(h.shape[0], 256) for h in hs)
    row = pl.BlockSpec((tm, D), lambda i: (i, 0))
    vec = pl.BlockSpec((1, D), lambda i: (0, 0))
    wspec = pl.BlockSpec((D, D), lambda i: (0, 0))
    n_p = hs[0].shape[0] // tm
    h_specs = [row] if len(hs) == 1 else list(_split_rows(tm, n_p, D))
    if gated:
        in_specs = [row, row, vec, wspec, *h_specs, vec]
    else:
        in_specs = [row, wspec, *h_specs, vec]
    return pl.pallas_call(
        functools.partial(_out_proj_kernel, gated=gated, n_p=n_p),
        grid=(T // tm,),
        in_specs=in_specs,
        out_specs=[row, row],
        out_shape=[jax.ShapeDtypeStruct((T, D), F32), jax.ShapeDtypeStruct((T, D), BF16)],
        compiler_params=_params("parallel"),
        name="out_proj",
    )(*a_inputs, w, *hs, norm_ffn)


def _ffn_kernel(xn_ref, h_ref, wg_ref, wu_ref, wd_ref, np_ref, h2_ref, xn2_ref):
    j = pl.program_id(1)

    @pl.when(j == 0)
    def _():
        h2_ref[...] = h_ref[...]

    xn = xn_ref[...]
    gate = _dot(xn, wg_ref[...])
    up = _dot(xn, wu_ref[...])
    h2_ref[...] += _dot((gate * _sigmoid(gate) * up).astype(BF16), wd_ref[...])

    @pl.when(j == pl.num_programs(1) - 1)
    def _():
        xn2_ref[...] = _rms_rows(h2_ref[...], np_ref[...]).astype(BF16)


def _ffn(xn, h, w_gu, w_down, norm_ple):
    T, D = h.shape
    ff = w_down.shape[0]
    tm = _tile(T, 512)
    tf = 512
    nf = ff // tf
    row = pl.BlockSpec((tm, D), lambda i, j: (i, 0))
    return pl.pallas_call(
        _ffn_kernel,
        grid=(T // tm, nf),
        in_specs=[row, row,
                  pl.BlockSpec((D, tf), lambda i, j: (0, j)),
                  pl.BlockSpec((D, tf), lambda i, j: (0, j + nf)),
                  pl.BlockSpec((tf, D), lambda i, j: (j, 0)),
                  pl.BlockSpec((1, D), lambda i, j: (0, 0))],
        out_specs=[row, row],
        out_shape=[jax.ShapeDtypeStruct((T, D), F32), jax.ShapeDtypeStruct((T, D), BF16)],
        compiler_params=_params("parallel", "arbitrary"),
        name="ffn",
    )(xn, h, w_gu, w_gu, w_down, norm_ple)


def _ple_kernel(*refs, last_layer, n_p):
    if last_layer:
        xn_ref, h_ref, pp_ref, ps_ref, wg_ref, wp_ref, yp_ref, ys_ref = refs
    else:
        xn_ref, h_ref, pp_ref, ps_ref, wg_ref, wp_ref, nm_ref, h3_ref, xn3_ref = refs
    is_prompt = pl.program_id(0) < n_p
    gate = _sigmoid(_dot(xn_ref[...], wg_ref[...]))
    p = jnp.where(is_prompt, pp_ref[0], ps_ref[0])
    h3 = h_ref[...] + _dot(p.astype(BF16), wp_ref[...]) * gate
    if last_layer:
        @pl.when(is_prompt)
        def _():
            yp_ref[...] = h3

        @pl.when(jnp.logical_not(is_prompt))
        def _():
            ys_ref[...] = h3
    else:
        h3_ref[...] = h3
        xn3_ref[...] = _rms_rows(h3, nm_ref[...]).astype(BF16)


def _ple(xn, h, p_prompt, p_sample, layer, w_gate, w_proj, next_norm):
    T, D = h.shape
    tp, ts, pd = p_prompt.shape[1], p_sample.shape[1], p_prompt.shape[2]
    tm = min(_tile(tp, 256), _tile(ts, 256))
    n_p = tp // tm
    row = pl.BlockSpec((tm, D), lambda i: (i, 0))
    last_layer = next_norm is None
    in_specs = [row, row,
                pl.BlockSpec((1, tm, pd), lambda i: (layer, jnp.minimum(i, n_p - 1), 0)),
                pl.BlockSpec((1, tm, pd), lambda i: (layer, jnp.maximum(i - n_p, 0), 0)),
                pl.BlockSpec((D, D), lambda i: (0, 0)), pl.BlockSpec((pd, D), lambda i: (0, 0))]
    args = [xn, h, p_prompt, p_sample, w_gate, w_proj]
    if last_layer:
        out_specs = list(_split_rows(tm, n_p, D))
        out_shape = [jax.ShapeDtypeStruct((tp, D), F32), jax.ShapeDtypeStruct((ts, D), F32)]
    else:
        in_specs.append(pl.BlockSpec((1, D), lambda i: (0, 0)))
        args.append(next_norm)
        out_specs = [row, row]
        out_shape = [jax.ShapeDtypeStruct((T, D), F32), jax.ShapeDtypeStruct((T, D), BF16)]
    return pl.pallas_call(
        functools.partial(_ple_kernel, last_layer=last_layer, n_p=n_p),
        grid=(T // tm,),
        in_specs=in_specs,
        out_specs=out_specs,
        out_shape=out_shape,
        compiler_params=_params("arbitrary"),
        name="ple",
    )(*args)


def _qkv_kernel(xn_ref, w_ref, seg_ref, gain_ref, q_ref, kv_ref, *, n_q_tiles, k_width):
    j = pl.program_id(1)
    x = _dot(xn_ref[...], w_ref[...])
    sq = x * x
    sq_hi = sq.astype(BF16)
    sq_lo = (sq - sq_hi.astype(F32)).astype(BF16)
    seg = seg_ref[...]
    ms = (_dot(sq_hi, seg) + _dot(sq_lo, seg)) * (1.0 / ATT_HD)
    normed = x * lax.rsqrt(ms + RMS_EPS) * gain_ref[0]

    @pl.when(j < n_q_tiles)
    def _():
        q_ref[...] = (normed * (ATT_HD ** -0.5)).astype(BF16)

    @pl.when(j == n_q_tiles)
    def _():
        lane = lax.broadcasted_iota(jnp.int32, x.shape, 1)
        kv_ref[...] = jnp.where(lane < k_width, normed, x)


def _qkv(xn, w_qkv, q_gain, k_gain):
    T, D = xn.shape
    n_out = w_qkv.shape[1]
    tn = ATT_GROUP * ATT_HD
    n_tiles = n_out // tn
    n_q_tiles = D // tn
    assert n_tiles == n_q_tiles + 1
    kv_w = (n_out - D) // 2
    tm = _tile(T, 512)
    seg = np.kron(np.eye(tn // ATT_HD, dtype=np.float32), np.ones((ATT_HD, ATT_HD), np.float32))
    q_row = jnp.tile(q_gain.astype(F32), tn // ATT_HD)
    kv_row = jnp.concatenate([jnp.tile(k_gain.astype(F32), kv_w // ATT_HD), jnp.ones((tn - kv_w,), F32)])
    gain = jnp.stack([q_row] * n_q_tiles + [kv_row])[:, None, :]
    return pl.pallas_call(
        functools.partial(_qkv_kernel, n_q_tiles=n_q_tiles, k_width=kv_w),
        grid=(T // tm, n_tiles),
        in_specs=[pl.BlockSpec((tm, D), lambda i, j: (i, 0)),
                  pl.BlockSpec((D, tn), lambda i, j: (0, j)),
                  pl.BlockSpec((tn, tn), lambda i, j: (0, 0)),
                  pl.BlockSpec((1, 1, tn), lambda i, j: (j, 0, 0))],
        out_specs=[pl.BlockSpec((tm, tn), lambda i, j: (i, jnp.minimum(j, n_q_tiles - 1))),
                   pl.BlockSpec((tm, tn), lambda i, j: (i, 0))],
        out_shape=[jax.ShapeDtypeStruct((T, D), BF16), jax.ShapeDtypeStruct((T, tn), F32)],
        compiler_params=_params("parallel", "arbitrary"),
        name="qkv_proj",
    )(xn, w_qkv, jnp.asarray(seg, BF16), gain)


def _alibi_slopes(n_heads):
    return [float(np.float32(2.0) ** np.float32(-8.0 * (i + 1) / n_heads)) for i in range(n_heads)]


def _attn_kernel(scal_ref, q_ref, kvc_ref, kv1_ref, kv2_ref, ck_ref, cv_ref, o_ref,
                 *, n_prompt_chunks, chunks_per_seq, n_kv_heads):
    i = pl.program_id(0)
    is_sample = i >= n_prompt_chunks
    c = lax.rem(i, chunks_per_seq)
    first_valid = jnp.where(is_sample, 0, jnp.maximum(ATT_PREV - c, 0) * CHUNK)
    kvw = n_kv_heads * ATT_HD
    n_keys = (ATT_PREV + 1) * CHUNK
    n_heads = n_kv_heads * ATT_GROUP
    slopes = _alibi_slopes(n_heads)
    pairs = ATT_GROUP // 2
    qw = ATT_GROUP * ATT_HD
    lanes = 2 * ATT_HD

    kvc = kvc_ref[...]
    kv1 = kv1_ref[...]
    kv2 = kv2_ref[...]
    k_prev = jnp.where(is_sample, ck_ref[0], jnp.concatenate([kv2[:, :kvw], kv1[:, :kvw]], axis=0))
    v_prev = jnp.where(is_sample, cv_ref[0], jnp.concatenate([kv2[:, kvw:], kv1[:, kvw:]], axis=0))
    k_all = jnp.concatenate([k_prev, kvc[:, :kvw]], axis=0)
    v_all = jnp.concatenate([v_prev, kvc[:, kvw:]], axis=0)

    low = lax.broadcasted_iota(jnp.int32, (n_keys, lanes), 1) < ATT_HD
    qi = lax.broadcasted_iota(jnp.int32, (CHUNK, n_keys), 0)
    kj = lax.broadcasted_iota(jnp.int32, (CHUNK, n_keys), 1)
    dist = jnp.abs(ATT_PREV * CHUNK + qi - kj).astype(F32)
    valid = kj >= first_valid

    def split_halves(t, odd):
        rolled = pltpu.roll(t, ATT_HD, 1)
        in_low, in_high = (rolled, t) if odd else (t, rolled)
        zero = jnp.zeros_like(t)
        return jnp.where(low, in_low, zero).astype(BF16), jnp.where(low, zero, in_high).astype(BF16)

    def kv_head_operands(kh):
        col = (kh // 2) * lanes
        ks = split_halves(k_all[:, col:col + lanes], kh % 2)
        vs = split_halves(v_all[:, col:col + lanes], kh % 2)
        qs = jnp.concatenate([q_ref[:, kh * qw + p * lanes: kh * qw + (p + 1) * lanes] for p in range(pairs)],
                             axis=0)
        return qs, ks, vs

    bound = scal_ref[n_heads] * (ATT_HD ** 0.5) * jnp.max(jnp.abs(k_all))
    small = bound <= ATT_SAFE_EXP

    @pl.when(small)
    def _():
        ones = jnp.ones((n_keys, lanes), BF16)
        units = [(kh, parity) for kh in range(n_kv_heads) for parity in range(2)]
        ops = [kv_head_operands(kh) for kh in range(n_kv_heads)]
        scores = [lax.dot_general(ops[kh][0], ops[kh][1][parity], NT_DIMS, preferred_element_type=F32)
                  for kh, parity in units]
        probs = []
        for (kh, parity), s_all in zip(units, scores):
            rows = []
            for p in range(pairs):
                head = kh * ATT_GROUP + 2 * p + parity
                s = s_all[p * CHUNK:(p + 1) * CHUNK] + dist * (-slopes[head])
                rows.append(jnp.exp(jnp.where(valid, s, -jnp.inf)).astype(BF16))
            probs.append(jnp.concatenate(rows, axis=0))
        both = [_dot(pr, jnp.concatenate([ops[kh][2][parity], ones], axis=1))
                for (kh, parity), pr in zip(units, probs)]
        for kh in range(n_kv_heads):
            for p in range(pairs):
                rows = slice(p * CHUNK, (p + 1) * CHUNK)
                out = None
                for parity in range(2):
                    head = kh * ATT_GROUP + 2 * p + parity
                    b = both[2 * kh + parity]
                    den = b[rows, lanes:] + jnp.exp(jnp.full((1, lanes), scal_ref[head], F32))
                    part = b[rows, :lanes] / den
                    out = part if out is None else out + part
                o_ref[:, kh * qw + p * lanes: kh * qw + (p + 1) * lanes] = out.astype(BF16)

    @pl.when(jnp.logical_not(small))
    def _():
        for kh in range(n_kv_heads):
            qs, ks, vs = kv_head_operands(kh)
            acc = None
            for parity in range(2):
                s_all = lax.dot_general(qs, ks[parity], NT_DIMS, preferred_element_type=F32)
                probs, inv = [], []
                for p in range(pairs):
                    head = kh * ATT_GROUP + 2 * p + parity
                    s = s_all[p * CHUNK:(p + 1) * CHUNK] + dist * (-slopes[head])
                    s = jnp.where(valid, s, -jnp.inf)
                    sink = scal_ref[head]
                    m = jnp.maximum(jnp.max(s, axis=-1, keepdims=True), sink)
                    e = jnp.exp(s - m)
                    den = jnp.sum(e, axis=-1, keepdims=True) + jnp.exp(sink - m)
                    probs.append(e.astype(BF16))
                    inv.append(1.0 / den)
                part = _dot(jnp.concatenate(probs, axis=0), vs[parity]) * jnp.concatenate(inv, axis=0)
                acc = part if acc is None else acc + part
            for p in range(pairs):
                o_ref[:, kh * qw + p * lanes: kh * qw + (p + 1) * lanes] = (
                    acc[p * CHUNK:(p + 1) * CHUNK].astype(BF16))


def _attention(q, kv, cache_k, cache_v, scalars, n_prompt_seq, chunks_per_seq):
    T, d_model = q.shape
    n_chunks = T // CHUNK
    ncp = n_prompt_seq * chunks_per_seq
    kvw2 = kv.shape[1]
    n_kv_heads = kvw2 // (2 * ATT_HD)
    n_cache = cache_k.shape[1]
    assert n_cache == ATT_PREV * CHUNK

    def prev_map(back):
        def index(i):
            ok = jnp.logical_and(i < ncp, lax.rem(i, chunks_per_seq) >= back)
            return (jnp.where(ok, i - back, i), 0)
        return index

    kv_blk = (CHUNK, kvw2)
    cache_spec = pl.BlockSpec((1, n_cache, kvw2 // 2), lambda i: (jnp.maximum(i - ncp, 0), 0, 0))
    kern = functools.partial(_attn_kernel, n_prompt_chunks=ncp, chunks_per_seq=chunks_per_seq,
                             n_kv_heads=n_kv_heads)
    return pl.pallas_call(
        kern,
        grid=(n_chunks,),
        in_specs=[pl.BlockSpec(memory_space=pltpu.SMEM),
                  pl.BlockSpec((CHUNK, d_model), lambda i: (i, 0)),
                  pl.BlockSpec(kv_blk, lambda i: (i, 0)),
                  pl.BlockSpec(kv_blk, prev_map(1)),
                  pl.BlockSpec(kv_blk, prev_map(2)),
                  cache_spec, cache_spec],
        out_specs=pl.BlockSpec((CHUNK, d_model), lambda i: (i, 0)),
        out_shape=jax.ShapeDtypeStruct((T, d_model), BF16),
        compiler_params=_params("parallel"),
        name="swa_attention",
    )(scalars, q, kv, kv, kv, cache_k, cache_v)


def kernel(x_prompt, x_sample, state_hgrn, cache_k, cache_v, p_prompt, p_sample, norm_mix, norm_ffn,
           norm_ple, a_w_in, a_lb_logits, a_g_norm, a_w_o, b_w_qkv, b_q_norm, b_k_norm, b_sinks, b_w_o,
           f_w_gu, f_w_down, ple_w_proj, ple_w_gate):
    B, L, D = x_prompt.shape
    Bs, Ls, _ = x_sample.shape
    assert L % CHUNK == 0 and Ls == CHUNK
    tp, ts = B * L, Bs * Ls
    cps = L // CHUNK
    depth = norm_mix.shape[0]
    assert depth == 2

    xp, xs = x_prompt.reshape(tp, D), x_sample.reshape(ts, D)
    pp, ps = p_prompt.reshape(depth, tp, -1), p_sample.reshape(depth, ts, -1)
    bf = lambda w: w.astype(BF16)
    vec = lambda w: w.reshape(1, -1).astype(F32)

    def ffn_ple(layer, h1, xn1, next_norm):
        h2, xn2 = _ffn(xn1, h1, bf(f_w_gu[layer]), bf(f_w_down[layer]), vec(norm_ple[layer]))
        return _ple(xn2, h2, pp, ps, layer, bf(ple_w_gate[layer]), bf(ple_w_proj[layer]), next_norm)

    q, k, v, lf, gs = _hgrn_in(xp, xs, vec(norm_mix[0]), bf(a_w_in[0]), a_lb_logits.astype(F32))
    o, state_p, state_s = _gla(q, k, v, lf, state_hgrn[0].astype(F32), B, cps)
    h1, xn1 = _out_proj((o, gs, vec(a_g_norm[0])), bf(a_w_o[0]), (xp, xs), vec(norm_ffn[0]), gated=True)
    h, xn = ffn_ple(0, h1, xn1, vec(norm_mix[1]))

    qn, kvn = _qkv(xn, bf(b_w_qkv[0]), b_q_norm[0], b_k_norm[0])
    n_kv = cache_k.shape[3]
    kvw = n_kv * ATT_HD
    scalars = jnp.concatenate([b_sinks[0].astype(F32), jnp.max(jnp.abs(b_q_norm[0])).reshape(1).astype(F32)])
    att = _attention(qn, kvn, cache_k[0].reshape(Bs, -1, kvw).astype(F32),
                     cache_v[0].reshape(Bs, -1, kvw).astype(F32), scalars, B, cps)
    h1, xn1 = _out_proj((att,), bf(b_w_o[0]), (h,), vec(norm_ffn[1]), gated=False)
    yp, ys = ffn_ple(1, h1, xn1, None)

    keep = min(ATT_PREV * CHUNK, L)
    kn = kvn[:, :kvw]
    vn = kvn[:, kvw:]
    heads = lambda t, b, l: t.reshape(b, l, n_kv, ATT_HD)
    return (yp.reshape(B, L, D),
            ys.reshape(Bs, Ls, D),
            state_p[None],
            state_s[None],
            heads(kn[:tp], B, L)[:, L - keep:][None],
            heads(vn[:tp], B, L)[:, L - keep:][None],
            heads(kn[tp:], Bs, Ls)[None],
            heads(vn[tp:], Bs, Ls)[None])
```

```python
import functools

import numpy as np
import jax
import jax.numpy as jnp
from jax import lax
from jax.experimental import pallas as pl
from jax.experimental.pallas import tpu as pltpu

F32 = jnp.float32
BF16 = jnp.bfloat16

RMS_EPS = 1e-6
CHUNK = 64
HEAD_DK = 128
ATT_HD = 64
ATT_GROUP = 8
ATT_PREV = 2
GLA_LEVELS = (64, 32, 16, 8, 4, 2)
GLA_SAFE_EXP = 60.0
ATT_SAFE_EXP = 70.0

NT_DIMS = (((1,), (1,)), ((), ()))
TN_DIMS = (((0,), (0,)), ((), ()))

VMEM_LIMIT_BYTES = 56 * 1024 * 1024


def _params(*sem):
    return pltpu.CompilerParams(dimension_semantics=sem, vmem_limit_bytes=VMEM_LIMIT_BYTES)


def _tile(n, pref):
    t = pref
    while t > 8 and n % t:
        t //= 2
    assert n % t == 0, (n, pref)
    return t


def _split_rows(tm, n_p, width, sample_buffers=None):
    mode = {} if sample_buffers is None else dict(pipeline_mode=pl.Buffered(sample_buffers))
    return (pl.BlockSpec((tm, width), lambda i, *_: (jnp.minimum(i, n_p - 1), 0)),
            pl.BlockSpec((tm, width), lambda i, *_: (jnp.maximum(i - n_p, 0), 0), **mode))


def _resident(shape):
    return pl.BlockSpec(shape, lambda *_: (0,) * len(shape), pipeline_mode=pl.Buffered(1))


def _row_chunks(rows, size=256):
    size = min(size, rows)
    assert rows % size == 0
    return [slice(r, r + size) for r in range(0, rows, size)]


def _sigmoid(x):
    return 1.0 / (1.0 + jnp.exp(-x))


def _rms_rows(x, w):
    ms = jnp.mean(x * x, axis=-1, keepdims=True)
    return x * lax.rsqrt(ms + RMS_EPS) * w


def _dot(a, b):
    return jnp.dot(a, b, preferred_element_type=F32)


def _hgrn_in_kernel(xp_ref, xs_ref, nw_ref, wq_ref, wf_ref, wi_ref, wg_ref, lbl_ref,
                    q_ref, k_ref, v_ref, lf_ref, gs_ref, xn_ref, *, n_p):
    first = pl.program_id(1) == 0
    is_prompt = pl.program_id(0) < n_p

    @pl.when(jnp.logical_and(first, is_prompt))
    def _():
        xn_ref[...] = _rms_rows(xp_ref[...], nw_ref[...]).astype(BF16)

    @pl.when(jnp.logical_and(first, jnp.logical_not(is_prompt)))
    def _():
        xn_ref[...] = _rms_rows(xs_ref[...], nw_ref[...]).astype(BF16)

    xn = xn_ref[...]
    q = _dot(xn, wq_ref[...])
    q_ref[...] = (q * _sigmoid(q)).astype(BF16)
    logits = lbl_ref[...]
    ex = jnp.exp(logits - jnp.max(logits, axis=0, keepdims=True))
    lb = ex[0:1] / jnp.sum(ex, axis=0, keepdims=True)
    f = _dot(xn, wf_ref[...])
    forget = lb + (1.0 - lb) * _sigmoid(f)
    k_ref[...] = (1.0 - forget).astype(BF16)
    lf_ref[...] = jnp.log(forget)
    v_ref[...] = _dot(xn, wi_ref[...]).astype(BF16)
    g = _dot(xn, wg_ref[...])
    gs_ref[...] = (g * _sigmoid(g)).astype(BF16)


def _hgrn_in(xp, xs, norm_w, w_in, lb_logits):
    (tp, D), ts = xp.shape, xs.shape[0]
    T = tp + ts
    tm = min(_tile(tp, 1024), _tile(ts, 1024))
    tn = 256
    nb = D // tn
    out = pl.BlockSpec((tm, tn), lambda i, j: (i, j))

    def wspec(g):
        return pl.BlockSpec((D, tn), lambda i, j, g=g: (0, j + g * nb))

    return pl.pallas_call(
        functools.partial(_hgrn_in_kernel, n_p=tp // tm),
        grid=(T // tm, nb),
        in_specs=[*_split_rows(tm, tp // tm, D, sample_buffers=1), pl.BlockSpec((1, D), lambda i, j: (0, 0)),
                  wspec(0), wspec(1), wspec(2), wspec(3),
                  pl.BlockSpec((lb_logits.shape[0], tn), lambda i, j: (0, j))],
        out_specs=[out] * 5,
        out_shape=[jax.ShapeDtypeStruct((T, D), BF16)] * 2
        + [jax.ShapeDtypeStruct((T, D), BF16), jax.ShapeDtypeStruct((T, D), F32),
           jax.ShapeDtypeStruct((T, D), BF16)],
        scratch_shapes=[pltpu.VMEM((tm, D), BF16)],
        compiler_params=_params("parallel", "arbitrary"),
        name="hgrn_in",
    )(xp, xs, norm_w, w_in, w_in, w_in, w_in, lb_logits)


def _gla_constants():
    c = CHUNK
    t = np.arange(c)[:, None]
    s = np.arange(c)[None, :]
    zs = [s <= t, s > t]
    level = np.full((c, c), -1, np.int32)
    for li, p in enumerate(GLA_LEVELS, start=1):
        half = p // 2
        mid = (t // p) * p + half - 1
        upper = (t % p) >= half
        zs.append(np.where(upper, (s > mid) & (s <= t), (s > t) & (s <= mid)))
        level[(t // p == s // p) & upper & ((s % p) < half)] = li
    np.fill_diagonal(level, 0)
    z = np.concatenate(zs, axis=0).astype(np.float32)
    zf = np.concatenate([zs[0], zs[0]], axis=0).astype(np.float32)
    zf[c:] *= -1.0
    three = lambda m: np.concatenate([m, m, m], axis=1)
    return three(z), three(zf), level


def _gla_kernel(q_ref, k_ref, v_ref, lf_ref, s0_ref, z_ref, zf_ref, lvl_ref, o_ref, sp_ref, ss_ref,
                st_ref, e_ref, *, n_prompt_chunks, chunks_per_seq, n_heads):
    i = pl.program_id(0)
    is_prompt = i < n_prompt_chunks
    c = jnp.where(is_prompt, lax.rem(i, chunks_per_seq), 0)

    @pl.when(jnp.logical_and(is_prompt, c == 0))
    def _():
        st_ref[...] = jnp.zeros_like(st_ref)

    @pl.when(jnp.logical_not(is_prompt))
    def _():
        for h in range(n_heads):
            st_ref[h] = s0_ref[0, h].T

    g = lf_ref[...]
    g_hi = g.astype(BF16)
    r1 = g - g_hi.astype(F32)
    g_mid = r1.astype(BF16)
    g_lo = (r1 - g_mid.astype(F32)).astype(BF16)
    gcat = jnp.concatenate([g_hi, g_mid, g_lo], axis=0)
    lvl = lvl_ref[...]

    mild = jnp.min(jnp.sum(g, axis=0, keepdims=True)) >= -GLA_SAFE_EXP

    @pl.when(mild)
    def _():
        e_ref[0:2 * CHUNK, :] = jnp.exp(_dot(zf_ref[...], gcat))
        sls = [slice(h * HEAD_DK, (h + 1) * HEAD_DK) for h in range(n_heads)]
        q_hat = [(q_ref[:, sl].astype(F32) * e_ref[0:CHUNK, sl]).astype(BF16) for sl in sls]
        k_bar = [(k_ref[:, sl].astype(F32) * e_ref[CHUNK:2 * CHUNK, sl]).astype(BF16) for sl in sls]
        a = [lax.dot_general(q_hat[h], k_bar[h], NT_DIMS, preferred_element_type=F32) for h in range(n_heads)]
        o = [lax.dot_general(q_hat[h], st_ref[h].astype(BF16), NT_DIMS, preferred_element_type=F32)
             for h in range(n_heads)]
        for h, sl in enumerate(sls):
            o_ref[:, sl] = o[h] + _dot(jnp.where(lvl >= 0, a[h], 0.0).astype(BF16), v_ref[:, sl])
        for h, sl in enumerate(sls):
            decay_all = e_ref[CHUNK - 1:CHUNK, sl]
            k_hat = (k_ref[:, sl].astype(F32) * (e_ref[CHUNK:2 * CHUNK, sl] * decay_all)).astype(BF16)
            st_ref[h] = st_ref[h] * decay_all + lax.dot_general(v_ref[:, sl], k_hat, TN_DIMS,
                                                                 preferred_element_type=F32)

    @pl.when(jnp.logical_not(mild))
    def _():
        e_ref[...] = jnp.exp(_dot(z_ref[...], gcat))
        row = lax.broadcasted_iota(jnp.int32, (CHUNK, HEAD_DK), 0)
        for h in range(n_heads):
            sl = slice(h * HEAD_DK, (h + 1) * HEAD_DK)
            qb = q_ref[:, sl]
            kb = k_ref[:, sl]
            vb = v_ref[:, sl]
            qf = qb.astype(F32)
            kf = kb.astype(F32)
            st = st_ref[h]
            q_hat = (qf * e_ref[0:CHUNK, sl]).astype(BF16)
            k_hat = (kf * e_ref[CHUNK:2 * CHUNK, sl]).astype(BF16)
            o = lax.dot_general(q_hat, st.astype(BF16), NT_DIMS, preferred_element_type=F32)
            a = lax.dot_general(qb, kb, NT_DIMS, preferred_element_type=F32)
            a = jnp.where(lvl == 0, a, 0.0)
            for li, p in enumerate(GLA_LEVELS, start=1):
                on_q_side = (row & (p // 2)) != 0
                x = (jnp.where(on_q_side, qf, kf)
                     * e_ref[(li + 1) * CHUNK:(li + 2) * CHUNK, sl]).astype(BF16)
                a = jnp.where(lvl == li, lax.dot_general(x, x, NT_DIMS, preferred_element_type=F32), a)
            o_ref[:, sl] = o + _dot(a.astype(BF16), vb)
            decay_all = e_ref[CHUNK - 1:CHUNK, sl]
            st_ref[h] = st * decay_all + lax.dot_general(vb, k_hat, TN_DIMS, preferred_element_type=F32)

    @pl.when(jnp.logical_and(is_prompt, c == chunks_per_seq - 1))
    def _():
        for h in range(n_heads):
            sp_ref[0, h] = st_ref[h].T

    @pl.when(jnp.logical_not(is_prompt))
    def _():
        for h in range(n_heads):
            ss_ref[0, h] = st_ref[h].T


def _gla(q, k, v, lf, s0, n_prompt_seq, chunks_per_seq):
    T, D = q.shape
    n_heads = D // HEAD_DK
    n_chunks = T // CHUNK
    ncp = n_prompt_seq * chunks_per_seq
    zcat, zfast, level = _gla_constants()
    blk = pl.BlockSpec((CHUNK, D), lambda i: (i, 0))
    st_blk = (1, n_heads, HEAD_DK, HEAD_DK)

    def prompt_seq(i):
        return (jnp.minimum(i // chunks_per_seq, n_prompt_seq - 1), 0, 0, 0)

    kern = functools.partial(_gla_kernel, n_prompt_chunks=ncp, chunks_per_seq=chunks_per_seq,
                             n_heads=n_heads)
    return pl.pallas_call(
        kern,
        grid=(n_chunks,),
        in_specs=[blk, blk, blk, blk,
                  pl.BlockSpec(st_blk, lambda i: (jnp.maximum(i - ncp, 0), 0, 0, 0)),
                  pl.BlockSpec(zcat.shape, lambda i: (0, 0)),
                  pl.BlockSpec(zfast.shape, lambda i: (0, 0)),
                  pl.BlockSpec(level.shape, lambda i: (0, 0))],
        out_specs=[blk, pl.BlockSpec(st_blk, prompt_seq),
                   pl.BlockSpec(st_blk, lambda i: (jnp.maximum(i - ncp, 0), 0, 0, 0))],
        out_shape=[jax.ShapeDtypeStruct((T, D), F32),
                   jax.ShapeDtypeStruct((n_prompt_seq, n_heads, HEAD_DK, HEAD_DK), F32),
                   jax.ShapeDtypeStruct((n_chunks - ncp, n_heads, HEAD_DK, HEAD_DK), F32)],
        scratch_shapes=[pltpu.VMEM((n_heads, HEAD_DK, HEAD_DK), F32),
                        pltpu.VMEM((zcat.shape[0], D), F32)],
        compiler_params=_params("arbitrary"),
        name="gla_scan",
    )(q, k, v, lf, s0, jnp.asarray(zcat, BF16), jnp.asarray(zfast, BF16), jnp.asarray(level))


def _out_proj_kernel(*refs, gated, n_p):
    if gated:
        o_ref, gs_ref, gn_ref, w_ref, *h_refs, h1_ref = refs
    else:
        a_ref, w_ref, *h_refs, h1_ref = refs
    for r in _row_chunks(h1_ref.shape[0]):
        if gated:
            a = (_rms_rows(o_ref[r, :], gn_ref[...]) * gs_ref[r, :].astype(F32)).astype(BF16)
        else:
            a = a_ref[r, :]
        if len(h_refs) == 2:
            h = jnp.where(pl.program_id(0) < n_p, h_refs[0][r, :], h_refs[1][r, :])
        else:
            h = h_refs[0][r, :]
        h1_ref[r, :] = h + _dot(a, w_ref[...])


def _out_proj(a_inputs, w, hs, gated):
    T, D = a_inputs[0].shape
    tm = min(_tile(h.shape[0], 512) for h in hs)
    row = pl.BlockSpec((tm, D), lambda i: (i, 0))
    vec = pl.BlockSpec((1, D), lambda i: (0, 0))
    wspec = _resident((D, D))
    n_p = hs[0].shape[0] // tm
    h_specs = [row] if len(hs) == 1 else list(_split_rows(tm, n_p, D))
    if gated:
        in_specs = [row, row, vec, wspec, *h_specs]
    else:
        in_specs = [row, wspec, *h_specs]
    return pl.pallas_call(
        functools.partial(_out_proj_kernel, gated=gated, n_p=n_p),
        grid=(T // tm,),
        in_specs=in_specs,
        out_specs=row,
        out_shape=jax.ShapeDtypeStruct((T, D), F32),
        compiler_params=_params("parallel"),
        name="out_proj",
    )(*a_inputs, w, *hs)


def _ffn_kernel(h_ref, nf_ref, wg_ref, wu_ref, wd_ref, h2_ref, xn_ref):
    @pl.when(pl.program_id(1) == 0)
    def _():
        h = h_ref[...]
        xn_ref[...] = _rms_rows(h, nf_ref[...]).astype(BF16)
        h2_ref[...] = h

    xn = xn_ref[...]
    gate = _dot(xn, wg_ref[...])
    up = _dot(xn, wu_ref[...])
    h2_ref[...] += _dot((gate * _sigmoid(gate) * up).astype(BF16), wd_ref[...])


def _ffn(h, norm_ffn, w_gu, w_down):
    T, D = h.shape
    ff = w_down.shape[0]
    tm = _tile(T, 1024)
    tf = 512
    nf = ff // tf
    row = pl.BlockSpec((tm, D), lambda i, j: (i, 0))
    return pl.pallas_call(
        _ffn_kernel,
        grid=(T // tm, nf),
        in_specs=[row, pl.BlockSpec((1, D), lambda i, j: (0, 0)),
                  pl.BlockSpec((D, tf), lambda i, j: (0, j)),
                  pl.BlockSpec((D, tf), lambda i, j: (0, j + nf)),
                  pl.BlockSpec((tf, D), lambda i, j: (j, 0))],
        out_specs=row,
        out_shape=jax.ShapeDtypeStruct((T, D), F32),
        scratch_shapes=[pltpu.VMEM((tm, D), BF16)],
        compiler_params=_params("parallel", "arbitrary"),
        name="ffn",
    )(h, norm_ffn, w_gu, w_gu, w_down)


def _ple_kernel(h_ref, np_ref, *refs, n_p):
    *p_refs, wg_ref, wp_ref, out_ref = refs
    for r in _row_chunks(h_ref.shape[0]):
        h = h_ref[r, :]
        gate = _sigmoid(_dot(_rms_rows(h, np_ref[...]).astype(BF16), wg_ref[...]))
        if len(p_refs) == 2:
            p = jnp.where(pl.program_id(0) < n_p, p_refs[0][0, r, :], p_refs[1][0, r, :])
        else:
            p = p_refs[0][0, r, :]
        out_ref[r, :] = h + _dot(p.astype(BF16), wp_ref[...]) * gate


def _ple_call(h, norm_ple, ps, layer, w_gate, w_proj, tm, first_block, n_blocks, n_p):
    D = h.shape[1]
    pd = ps[0].shape[2]
    if len(ps) == 2:
        p_specs = [pl.BlockSpec((1, tm, pd), lambda i: (layer, jnp.minimum(i, n_p - 1), 0)),
                   pl.BlockSpec((1, tm, pd), lambda i: (layer, jnp.maximum(i - n_p, 0), 0))]
    else:
        p_specs = [pl.BlockSpec((1, tm, pd), lambda i: (layer, i, 0))]
    return pl.pallas_call(
        functools.partial(_ple_kernel, n_p=n_p),
        grid=(n_blocks,),
        in_specs=[pl.BlockSpec((tm, D), lambda i: (i + first_block, 0)),
                  pl.BlockSpec((1, D), lambda i: (0, 0)),
                  *p_specs, _resident((D, D)), _resident((pd, D))],
        out_specs=pl.BlockSpec((tm, D), lambda i: (i, 0)),
        out_shape=jax.ShapeDtypeStruct((n_blocks * tm, D), F32),
        compiler_params=_params("parallel"),
        name="ple",
    )(h, norm_ple, *ps, w_gate, w_proj)


def _ple(h, norm_ple, p_prompt, p_sample, layer, w_gate, w_proj, split_output):
    tp, ts = p_prompt.shape[1], p_sample.shape[1]
    tm = min(_tile(tp, 512), _tile(ts, 512))
    n_p, n_s = tp // tm, ts // tm
    args = (h, norm_ple)
    if split_output:
        return (_ple_call(*args, (p_prompt,), layer, w_gate, w_proj, tm, 0, n_p, n_p),
                _ple_call(*args, (p_sample,), layer, w_gate, w_proj, tm, n_p, n_s, n_p))
    return _ple_call(*args, (p_prompt, p_sample), layer, w_gate, w_proj, tm, 0, n_p + n_s, n_p)


def _qkv_kernel(h_ref, nm_ref, w_ref, seg_ref, gain_ref, q_ref, kv_ref, xn_ref, *, n_q_tiles, k_width):
    j = pl.program_id(1)

    @pl.when(j == 0)
    def _():
        xn_ref[...] = _rms_rows(h_ref[...], nm_ref[...]).astype(BF16)

    x = _dot(xn_ref[...], w_ref[...])
    ms = _dot((x * x).astype(BF16), seg_ref[...]) * (1.0 / ATT_HD)
    normed = x * lax.rsqrt(ms + RMS_EPS) * gain_ref[0]

    @pl.when(j < n_q_tiles)
    def _():
        q_ref[...] = (normed * (ATT_HD ** -0.5)).astype(BF16)

    @pl.when(j == n_q_tiles)
    def _():
        lane = lax.broadcasted_iota(jnp.int32, x.shape, 1)
        kv_ref[...] = jnp.where(lane < k_width, normed, x)


def _qkv(h, norm_mix, w_qkv, q_gain, k_gain):
    T, D = h.shape
    n_out = w_qkv.shape[1]
    tn = ATT_GROUP * ATT_HD
    n_tiles = n_out // tn
    n_q_tiles = D // tn
    assert n_tiles == n_q_tiles + 1
    kv_w = (n_out - D) // 2
    tm = _tile(T, 1024)
    seg = np.kron(np.eye(tn // ATT_HD, dtype=np.float32), np.ones((ATT_HD, ATT_HD), np.float32))
    q_row = jnp.tile(q_gain.astype(F32), tn // ATT_HD)
    kv_row = jnp.concatenate([jnp.tile(k_gain.astype(F32), kv_w // ATT_HD), jnp.ones((tn - kv_w,), F32)])
    gain = jnp.stack([q_row] * n_q_tiles + [kv_row])[:, None, :]
    return pl.pallas_call(
        functools.partial(_qkv_kernel, n_q_tiles=n_q_tiles, k_width=kv_w),
        grid=(T // tm, n_tiles),
        in_specs=[pl.BlockSpec((tm, D), lambda i, j: (i, 0)),
                  pl.BlockSpec((1, D), lambda i, j: (0, 0)),
                  pl.BlockSpec((D, tn), lambda i, j: (0, j)),
                  pl.BlockSpec((tn, tn), lambda i, j: (0, 0)),
                  pl.BlockSpec((1, 1, tn), lambda i, j: (j, 0, 0))],
        out_specs=[pl.BlockSpec((tm, tn), lambda i, j: (i, jnp.minimum(j, n_q_tiles - 1))),
                   pl.BlockSpec((tm, tn), lambda i, j: (i, 0))],
        out_shape=[jax.ShapeDtypeStruct((T, D), BF16), jax.ShapeDtypeStruct((T, tn), F32)],
        scratch_shapes=[pltpu.VMEM((tm, D), BF16)],
        compiler_params=_params("parallel", "arbitrary"),
        name="qkv_proj",
    )(h, norm_mix, w_qkv, jnp.asarray(seg, BF16), gain)


def _alibi_slopes(n_heads):
    return [float(np.float32(2.0) ** np.float32(-8.0 * (i + 1) / n_heads)) for i in range(n_heads)]


def _attn_kernel(scal_ref, q_ref, kvc_ref, kv1_ref, kv2_ref, ck_ref, cv_ref, o_ref,
                 *, n_prompt_chunks, chunks_per_seq, n_kv_heads):
    i = pl.program_id(0)
    is_sample = i >= n_prompt_chunks
    c = lax.rem(i, chunks_per_seq)
    first_valid = jnp.where(is_sample, 0, jnp.maximum(ATT_PREV - c, 0) * CHUNK)
    kvw = n_kv_heads * ATT_HD
    n_keys = (ATT_PREV + 1) * CHUNK
    n_heads = n_kv_heads * ATT_GROUP
    slopes = _alibi_slopes(n_heads)
    pairs = ATT_GROUP // 2
    qw = ATT_GROUP * ATT_HD
    lanes = 2 * ATT_HD

    kvc = kvc_ref[...]
    kv1 = kv1_ref[...]
    kv2 = kv2_ref[...]
    k_prev = jnp.where(is_sample, ck_ref[0], jnp.concatenate([kv2[:, :kvw], kv1[:, :kvw]], axis=0))
    v_prev = jnp.where(is_sample, cv_ref[0], jnp.concatenate([kv2[:, kvw:], kv1[:, kvw:]], axis=0))
    k_all = jnp.concatenate([k_prev, kvc[:, :kvw]], axis=0)
    v_all = jnp.concatenate([v_prev, kvc[:, kvw:]], axis=0)

    low = lax.broadcasted_iota(jnp.int32, (n_keys, lanes), 1) < ATT_HD
    qi = lax.broadcasted_iota(jnp.int32, (CHUNK, n_keys), 0)
    kj = lax.broadcasted_iota(jnp.int32, (CHUNK, n_keys), 1)
    dist = jnp.abs(ATT_PREV * CHUNK + qi - kj).astype(F32)
    valid = kj >= first_valid

    def split_halves(t, odd):
        rolled = pltpu.roll(t, ATT_HD, 1)
        in_low, in_high = (rolled, t) if odd else (t, rolled)
        zero = jnp.zeros_like(t)
        return jnp.where(low, in_low, zero).astype(BF16), jnp.where(low, zero, in_high).astype(BF16)

    def kv_head_operands(kh):
        col = (kh // 2) * lanes
        ks = split_halves(k_all[:, col:col + lanes], kh % 2)
        vs = split_halves(v_all[:, col:col + lanes], kh % 2)
        qs = jnp.concatenate([q_ref[:, kh * qw + p * lanes: kh * qw + (p + 1) * lanes] for p in range(pairs)],
                             axis=0)
        return qs, ks, vs

    bound = scal_ref[n_heads] * (ATT_HD ** 0.5) * jnp.max(jnp.abs(k_all))
    small = bound <= ATT_SAFE_EXP

    @pl.when(small)
    def _():
        ones = jnp.ones((n_keys, lanes), BF16)
        units = [(kh, parity) for kh in range(n_kv_heads) for parity in range(2)]
        ops = [kv_head_operands(kh) for kh in range(n_kv_heads)]
        scores = [lax.dot_general(ops[kh][0], ops[kh][1][parity], NT_DIMS, preferred_element_type=F32)
                  for kh, parity in units]
        probs = []
        for (kh, parity), s_all in zip(units, scores):
            rows = []
            for p in range(pairs):
                head = kh * ATT_GROUP + 2 * p + parity
                s = s_all[p * CHUNK:(p + 1) * CHUNK] + dist * (-slopes[head])
                rows.append(jnp.exp(jnp.where(valid, s, -jnp.inf)).astype(BF16))
            probs.append(jnp.concatenate(rows, axis=0))
        both = [_dot(pr, jnp.concatenate([ops[kh][2][parity], ones], axis=1))
                for (kh, parity), pr in zip(units, probs)]
        for kh in range(n_kv_heads):
            for p in range(pairs):
                rows = slice(p * CHUNK, (p + 1) * CHUNK)
                out = None
                for parity in range(2):
                    head = kh * ATT_GROUP + 2 * p + parity
                    b = both[2 * kh + parity]
                    den = b[rows, lanes:] + jnp.exp(jnp.full((1, lanes), scal_ref[head], F32))
                    part = b[rows, :lanes] / den
                    out = part if out is None else out + part
                o_ref[:, kh * qw + p * lanes: kh * qw + (p + 1) * lanes] = out.astype(BF16)

    @pl.when(jnp.logical_not(small))
    def _():
        for kh in range(n_kv_heads):
            qs, ks, vs = kv_head_operands(kh)
            acc = None
            for parity in range(2):
                s_all = lax.dot_general(qs, ks[parity], NT_DIMS, preferred_element_type=F32)
                probs, inv = [], []
                for p in range(pairs):
                    head = kh * ATT_GROUP + 2 * p + parity
                    s = s_all[p * CHUNK:(p + 1) * CHUNK] + dist * (-slopes[head])
                    s = jnp.where(valid, s, -jnp.inf)
                    sink = scal_ref[head]
                    m = jnp.maximum(jnp.max(s, axis=-1, keepdims=True), sink)
                    e = jnp.exp(s - m)
                    den = jnp.sum(e, axis=-1, keepdims=True) + jnp.exp(sink - m)
                    probs.append(e.astype(BF16))
                    inv.append(1.0 / den)
                part = _dot(jnp.concatenate(probs, axis=0), vs[parity]) * jnp.concatenate(inv, axis=0)
                acc = part if acc is None else acc + part
            for p in range(pairs):
                o_ref[:, kh * qw + p * lanes: kh * qw + (p + 1) * lanes] = (
                    acc[p * CHUNK:(p + 1) * CHUNK].astype(BF16))


def _attention(q, kv, cache_k, cache_v, scalars, n_prompt_seq, chunks_per_seq):
    T, d_model = q.shape
    n_chunks = T // CHUNK
    ncp = n_prompt_seq * chunks_per_seq
    kvw2 = kv.shape[1]
    n_kv_heads = kvw2 // (2 * ATT_HD)
    n_cache = cache_k.shape[1]
    assert n_cache == ATT_PREV * CHUNK

    def prev_map(back):
        def index(i):
            ok = jnp.logical_and(i < ncp, lax.rem(i, chunks_per_seq) >= back)
            return (jnp.where(ok, i - back, i), 0)
        return index

    kv_blk = (CHUNK, kvw2)
    cache_spec = pl.BlockSpec((1, n_cache, kvw2 // 2), lambda i: (jnp.maximum(i - ncp, 0), 0, 0))
    kern = functools.partial(_attn_kernel, n_prompt_chunks=ncp, chunks_per_seq=chunks_per_seq,
                             n_kv_heads=n_kv_heads)
    return pl.pallas_call(
        kern,
        grid=(n_chunks,),
        in_specs=[pl.BlockSpec(memory_space=pltpu.SMEM),
                  pl.BlockSpec((CHUNK, d_model), lambda i: (i, 0)),
                  pl.BlockSpec(kv_blk, lambda i: (i, 0)),
                  pl.BlockSpec(kv_blk, prev_map(1)),
                  pl.BlockSpec(kv_blk, prev_map(2)),
                  cache_spec, cache_spec],
        out_specs=pl.BlockSpec((CHUNK, d_model), lambda i: (i, 0)),
        out_shape=jax.ShapeDtypeStruct((T, d_model), BF16),
        compiler_params=_params("parallel"),
        name="swa_attention",
    )(scalars, q, kv, kv, kv, cache_k, cache_v)


def kernel(x_prompt, x_sample, state_hgrn, cache_k, cache_v, p_prompt, p_sample, norm_mix, norm_ffn,
           norm_ple, a_w_in, a_lb_logits, a_g_norm, a_w_o, b_w_qkv, b_q_norm, b_k_norm, b_sinks, b_w_o,
           f_w_gu, f_w_down, ple_w_proj, ple_w_gate):
    B, L, D = x_prompt.shape
    Bs, Ls, _ = x_sample.shape
    assert L % CHUNK == 0 and Ls == CHUNK
    tp, ts = B * L, Bs * Ls
    cps = L // CHUNK
    depth = norm_mix.shape[0]
    assert depth == 2

    xp, xs = x_prompt.reshape(tp, D), x_sample.reshape(ts, D)
    pp, ps = p_prompt.reshape(depth, tp, -1), p_sample.reshape(depth, ts, -1)
    bf = lambda w: w.astype(BF16)
    vec = lambda w: w.reshape(1, -1).astype(F32)

    def ffn_ple(layer, h1):
        h2 = _ffn(h1, vec(norm_ffn[layer]), bf(f_w_gu[layer]), bf(f_w_down[layer]))
        return _ple(h2, vec(norm_ple[layer]), pp, ps, layer, bf(ple_w_gate[layer]), bf(ple_w_proj[layer]),
                    split_output=layer == depth - 1)

    q, k, v, lf, gs = _hgrn_in(xp, xs, vec(norm_mix[0]), bf(a_w_in[0]), a_lb_logits.astype(F32))
    o, state_p, state_s = _gla(q, k, v, lf, state_hgrn[0].astype(F32), B, cps)
    h1 = _out_proj((o, gs, vec(a_g_norm[0])), bf(a_w_o[0]), (xp, xs), gated=True)
    h = ffn_ple(0, h1)

    qn, kvn = _qkv(h, vec(norm_mix[1]), bf(b_w_qkv[0]), b_q_norm[0], b_k_norm[0])
    n_kv = cache_k.shape[3]
    kvw = n_kv * ATT_HD
    scalars = jnp.concatenate([b_sinks[0].astype(F32), jnp.max(jnp.abs(b_q_norm[0])).reshape(1).astype(F32)])
    att = _attention(qn, kvn, cache_k[0].reshape(Bs, -1, kvw).astype(F32),
                     cache_v[0].reshape(Bs, -1, kvw).astype(F32), scalars, B, cps)
    h1 = _out_proj((att,), bf(b_w_o[0]), (h,), gated=False)
    yp, ys = ffn_ple(1, h1)

    keep = min(ATT_PREV * CHUNK, L)
    kn = kvn[:, :kvw]
    vn = kvn[:, kvw:]
    heads = lambda t, b, l: t.reshape(b, l, n_kv, ATT_HD)
    return (yp.reshape(B, L, D),
            ys.reshape(Bs, Ls, D),
            state_p[None],
            state_s[None],
            heads(kn[:tp], B, L)[:, L - keep:][None],
            heads(vn[:tp], B, L)[:, L - keep:][None],
            heads(kn[tp:], Bs, Ls)[None],
            heads(vn[tp:], Bs, Ls)[None])
```

```python
import functools

import numpy as np
import jax
import jax.numpy as jnp
from jax import lax
from jax.experimental import pallas as pl
from jax.experimental.pallas import tpu as pltpu

F32 = jnp.float32
BF16 = jnp.bfloat16

RMS_EPS = 1e-6
CHUNK = 64
HEAD_DK = 128
ATT_HD = 64
ATT_GROUP = 8
ATT_PREV = 2
GLA_LEVELS = (64, 32, 16, 8, 4, 2)
GLA_SAFE_EXP = 60.0
ATT_SAFE_EXP = 70.0

NT_DIMS = (((1,), (1,)), ((), ()))
TN_DIMS = (((0,), (0,)), ((), ()))

VMEM_LIMIT_BYTES = 60 * 1024 * 1024


def _params(*sem):
    return pltpu.CompilerParams(dimension_semantics=sem, vmem_limit_bytes=VMEM_LIMIT_BYTES)


def _tile(n, pref):
    t = pref
    while t > 8 and n % t:
        t //= 2
    assert n % t == 0, (n, pref)
    return t


def _split_rows(tm, n_p, width, sample_buffers=None):
    mode = {} if sample_buffers is None else dict(pipeline_mode=pl.Buffered(sample_buffers))
    return (pl.BlockSpec((tm, width), lambda i, *_: (jnp.minimum(i, n_p - 1), 0)),
            pl.BlockSpec((tm, width), lambda i, *_: (jnp.maximum(i - n_p, 0), 0), **mode))


def _resident(shape):
    return pl.BlockSpec(shape, lambda *_: (0,) * len(shape), pipeline_mode=pl.Buffered(1))


def _row_chunks(rows, size=256):
    size = min(size, rows)
    assert rows % size == 0
    return [slice(r, r + size) for r in range(0, rows, size)]


def _side_cast_specs(side, n_steps, step_of):
    in_specs, out_specs, out_shape = [], [], []
    for arr, idx in side:
        _, rows, cols = arr.shape
        rb = 16
        while rows // rb > n_steps:
            rb *= 2
        assert rows % rb == 0
        last = rows // rb - 1
        in_specs.append(pl.BlockSpec(
            (1, rb, cols), lambda *g, idx=idx, last=last: (idx, jnp.minimum(step_of(*g), last), 0)))
        out_specs.append(pl.BlockSpec((rb, cols), lambda *g, last=last: (jnp.minimum(step_of(*g), last), 0)))
        out_shape.append(jax.ShapeDtypeStruct((rows, cols), BF16))
    return in_specs, out_specs, out_shape


def _side_cast(in_refs, out_refs):
    for src, dst in zip(in_refs, out_refs):
        dst[...] = src[0].astype(BF16)


def _sigmoid(x):
    return 1.0 / (1.0 + jnp.exp(-x))


def _rms_rows(x, w):
    ms = jnp.mean(x * x, axis=-1, keepdims=True)
    return x * lax.rsqrt(ms + RMS_EPS) * w


def _dot(a, b):
    return jnp.dot(a, b, preferred_element_type=F32)


def _hgrn_in_kernel(xp_ref, xs_ref, nw_ref, wq_ref, wf_ref, wi_ref, wg_ref, lbl_ref, *refs, n_p, n_side):
    side_in, refs = refs[:n_side], refs[n_side:]
    q_ref, k_ref, v_ref, lf_ref, gs_ref = refs[:5]
    side_out, (xn_ref,) = refs[5:5 + n_side], refs[5 + n_side:]
    first = pl.program_id(1) == 0
    is_prompt = pl.program_id(0) < n_p

    @pl.when(jnp.logical_and(first, is_prompt))
    def _():
        xn_ref[...] = _rms_rows(xp_ref[...], nw_ref[...]).astype(BF16)

    @pl.when(jnp.logical_and(first, jnp.logical_not(is_prompt)))
    def _():
        xn_ref[...] = _rms_rows(xs_ref[...], nw_ref[...]).astype(BF16)

    xn = xn_ref[...]
    q = _dot(xn, wq_ref[...])
    q_ref[...] = (q * _sigmoid(q)).astype(BF16)
    logits = lbl_ref[...]
    ex = jnp.exp(logits - jnp.max(logits, axis=0, keepdims=True))
    lb = ex[0:1] / jnp.sum(ex, axis=0, keepdims=True)
    f = _dot(xn, wf_ref[...])
    forget = lb + (1.0 - lb) * _sigmoid(f)
    k_ref[...] = (1.0 - forget).astype(BF16)
    lf_ref[...] = jnp.log(forget)
    v_ref[...] = _dot(xn, wi_ref[...]).astype(BF16)
    g = _dot(xn, wg_ref[...])
    gs_ref[...] = (g * _sigmoid(g)).astype(BF16)
    _side_cast(side_in, side_out)


def _hgrn_in(xp, xs, norm_w, w_in, lb_logits, side):
    (tp, D), ts = xp.shape, xs.shape[0]
    T = tp + ts
    tm = min(_tile(tp, 1024), _tile(ts, 1024))
    tn = 256
    nb = D // tn
    out = pl.BlockSpec((tm, tn), lambda i, j: (i, j))

    def wspec(g):
        return pl.BlockSpec((D, tn), lambda i, j, g=g: (0, j + g * nb))

    side_in, side_out, side_shape = _side_cast_specs(side, (T // tm) * nb, lambda i, j: i * nb + j)
    res = pl.pallas_call(
        functools.partial(_hgrn_in_kernel, n_p=tp // tm, n_side=len(side)),
        grid=(T // tm, nb),
        in_specs=[*_split_rows(tm, tp // tm, D, sample_buffers=1), pl.BlockSpec((1, D), lambda i, j: (0, 0)),
                  wspec(0), wspec(1), wspec(2), wspec(3),
                  pl.BlockSpec((lb_logits.shape[0], tn), lambda i, j: (0, j)), *side_in],
        out_specs=[out] * 5 + side_out,
        out_shape=[jax.ShapeDtypeStruct((T, D), BF16)] * 2
        + [jax.ShapeDtypeStruct((T, D), BF16), jax.ShapeDtypeStruct((T, D), F32),
           jax.ShapeDtypeStruct((T, D), BF16)] + side_shape,
        scratch_shapes=[pltpu.VMEM((tm, D), BF16)],
        compiler_params=_params("arbitrary", "arbitrary"),
        name="hgrn_in",
    )(xp, xs, norm_w, w_in, w_in, w_in, w_in, lb_logits, *[a for a, _ in side])
    return res[:5], res[5:]


def _gla_constants():
    c = CHUNK
    t = np.arange(c)[:, None]
    s = np.arange(c)[None, :]
    zs = [s <= t, s > t]
    level = np.full((c, c), -1, np.int32)
    for li, p in enumerate(GLA_LEVELS, start=1):
        half = p // 2
        mid = (t // p) * p + half - 1
        upper = (t % p) >= half
        zs.append(np.where(upper, (s > mid) & (s <= t), (s > t) & (s <= mid)))
        level[(t // p == s // p) & upper & ((s % p) < half)] = li
    np.fill_diagonal(level, 0)
    z = np.concatenate(zs, axis=0).astype(np.float32)
    zf = np.concatenate([zs[0], zs[0]], axis=0).astype(np.float32)
    zf[c:] *= -1.0
    three = lambda m: np.concatenate([m, m, m], axis=1)
    return three(z), three(zf), level


def _gla_kernel(q_ref, k_ref, v_ref, lf_ref, s0_ref, z_ref, zf_ref, lvl_ref, o_ref, sp_ref, ss_ref,
                st_ref, e_ref, *, n_prompt_chunks, chunks_per_seq, n_heads):
    i = pl.program_id(0)
    is_prompt = i < n_prompt_chunks
    c = jnp.where(is_prompt, lax.rem(i, chunks_per_seq), 0)

    @pl.when(jnp.logical_and(is_prompt, c == 0))
    def _():
        st_ref[...] = jnp.zeros_like(st_ref)

    @pl.when(jnp.logical_not(is_prompt))
    def _():
        for h in range(n_heads):
            st_ref[h] = s0_ref[0, h].T

    g = lf_ref[...]
    g_hi = g.astype(BF16)
    r1 = g - g_hi.astype(F32)
    g_mid = r1.astype(BF16)
    g_lo = (r1 - g_mid.astype(F32)).astype(BF16)
    gcat = jnp.concatenate([g_hi, g_mid, g_lo], axis=0)
    lvl = lvl_ref[...]

    mild = jnp.min(jnp.sum(g, axis=0, keepdims=True)) >= -GLA_SAFE_EXP

    @pl.when(mild)
    def _():
        e_ref[0:2 * CHUNK, :] = jnp.exp(_dot(zf_ref[...], gcat))
        sls = [slice(h * HEAD_DK, (h + 1) * HEAD_DK) for h in range(n_heads)]
        q_hat = [(q_ref[:, sl].astype(F32) * e_ref[0:CHUNK, sl]).astype(BF16) for sl in sls]
        k_bar = [(k_ref[:, sl].astype(F32) * e_ref[CHUNK:2 * CHUNK, sl]).astype(BF16) for sl in sls]
        a = [lax.dot_general(q_hat[h], k_bar[h], NT_DIMS, preferred_element_type=F32) for h in range(n_heads)]
        o = [lax.dot_general(q_hat[h], st_ref[h].astype(BF16), NT_DIMS, preferred_element_type=F32)
             for h in range(n_heads)]
        for h, sl in enumerate(sls):
            o_ref[:, sl] = (o[h] + _dot(jnp.where(lvl >= 0, a[h], 0.0).astype(BF16), v_ref[:, sl])).astype(BF16)
        for h, sl in enumerate(sls):
            decay_all = e_ref[CHUNK - 1:CHUNK, sl]
            k_hat = (k_ref[:, sl].astype(F32) * (e_ref[CHUNK:2 * CHUNK, sl] * decay_all)).astype(BF16)
            st_ref[h] = st_ref[h] * decay_all + lax.dot_general(v_ref[:, sl], k_hat, TN_DIMS,
                                                                 preferred_element_type=F32)

    @pl.when(jnp.logical_not(mild))
    def _():
        e_ref[...] = jnp.exp(_dot(z_ref[...], gcat))
        row = lax.broadcasted_iota(jnp.int32, (CHUNK, HEAD_DK), 0)
        for h in range(n_heads):
            sl = slice(h * HEAD_DK, (h + 1) * HEAD_DK)
            qb = q_ref[:, sl]
            kb = k_ref[:, sl]
            vb = v_ref[:, sl]
            qf = qb.astype(F32)
            kf = kb.astype(F32)
            st = st_ref[h]
            q_hat = (qf * e_ref[0:CHUNK, sl]).astype(BF16)
            k_hat = (kf * e_ref[CHUNK:2 * CHUNK, sl]).astype(BF16)
            o = lax.dot_general(q_hat, st.astype(BF16), NT_DIMS, preferred_element_type=F32)
            a = lax.dot_general(qb, kb, NT_DIMS, preferred_element_type=F32)
            a = jnp.where(lvl == 0, a, 0.0)
            for li, p in enumerate(GLA_LEVELS, start=1):
                on_q_side = (row & (p // 2)) != 0
                x = (jnp.where(on_q_side, qf, kf)
                     * e_ref[(li + 1) * CHUNK:(li + 2) * CHUNK, sl]).astype(BF16)
                a = jnp.where(lvl == li, lax.dot_general(x, x, NT_DIMS, preferred_element_type=F32), a)
            o_ref[:, sl] = (o + _dot(a.astype(BF16), vb)).astype(BF16)
            decay_all = e_ref[CHUNK - 1:CHUNK, sl]
            st_ref[h] = st * decay_all + lax.dot_general(vb, k_hat, TN_DIMS, preferred_element_type=F32)

    @pl.when(jnp.logical_and(is_prompt, c == chunks_per_seq - 1))
    def _():
        for h in range(n_heads):
            sp_ref[0, h] = st_ref[h].T

    @pl.when(jnp.logical_not(is_prompt))
    def _():
        for h in range(n_heads):
            ss_ref[0, h] = st_ref[h].T


def _gla(q, k, v, lf, s0, n_prompt_seq, chunks_per_seq):
    T, D = q.shape
    n_heads = D // HEAD_DK
    n_chunks = T // CHUNK
    ncp = n_prompt_seq * chunks_per_seq
    zcat, zfast, level = _gla_constants()
    blk = pl.BlockSpec((CHUNK, D), lambda i: (i, 0))
    st_blk = (1, n_heads, HEAD_DK, HEAD_DK)

    def prompt_seq(i):
        return (jnp.minimum(i // chunks_per_seq, n_prompt_seq - 1), 0, 0, 0)

    kern = functools.partial(_gla_kernel, n_prompt_chunks=ncp, chunks_per_seq=chunks_per_seq,
                             n_heads=n_heads)
    return pl.pallas_call(
        kern,
        grid=(n_chunks,),
        in_specs=[blk, blk, blk, blk,
                  pl.BlockSpec(st_blk, lambda i: (jnp.maximum(i - ncp, 0), 0, 0, 0)),
                  pl.BlockSpec(zcat.shape, lambda i: (0, 0)),
                  pl.BlockSpec(zfast.shape, lambda i: (0, 0)),
                  pl.BlockSpec(level.shape, lambda i: (0, 0))],
        out_specs=[blk, pl.BlockSpec(st_blk, prompt_seq),
                   pl.BlockSpec(st_blk, lambda i: (jnp.maximum(i - ncp, 0), 0, 0, 0))],
        out_shape=[jax.ShapeDtypeStruct((T, D), BF16),
                   jax.ShapeDtypeStruct((n_prompt_seq, n_heads, HEAD_DK, HEAD_DK), F32),
                   jax.ShapeDtypeStruct((n_chunks - ncp, n_heads, HEAD_DK, HEAD_DK), F32)],
        scratch_shapes=[pltpu.VMEM((n_heads, HEAD_DK, HEAD_DK), F32),
                        pltpu.VMEM((zcat.shape[0], D), F32)],
        compiler_params=_params("arbitrary"),
        name="gla_scan",
    )(q, k, v, lf, s0, jnp.asarray(zcat, BF16), jnp.asarray(zfast, BF16), jnp.asarray(level))


def _out_proj_kernel(*refs, gated, n_p):
    if gated:
        o_ref, gs_ref, gn_ref, w_ref, *h_refs, h1_ref = refs
    else:
        a_ref, w_ref, *h_refs, h1_ref = refs
    for r in _row_chunks(h1_ref.shape[0]):
        if gated:
            a = (_rms_rows(o_ref[r, :].astype(F32), gn_ref[...]) * gs_ref[r, :].astype(F32)).astype(BF16)
        else:
            a = a_ref[r, :]
        if len(h_refs) == 2:
            h = jnp.where(pl.program_id(0) < n_p, h_refs[0][r, :], h_refs[1][r, :])
        else:
            h = h_refs[0][r, :]
        h1_ref[r, :] = h + _dot(a, w_ref[...])


def _out_proj(a_inputs, w, hs, gated):
    T, D = a_inputs[0].shape
    tm = min(_tile(h.shape[0], 512) for h in hs)
    row = pl.BlockSpec((tm, D), lambda i: (i, 0))
    vec = pl.BlockSpec((1, D), lambda i: (0, 0))
    wspec = _resident((D, D))
    n_p = hs[0].shape[0] // tm
    h_specs = [row] if len(hs) == 1 else list(_split_rows(tm, n_p, D))
    if gated:
        in_specs = [row, row, vec, wspec, *h_specs]
    else:
        in_specs = [row, wspec, *h_specs]
    return pl.pallas_call(
        functools.partial(_out_proj_kernel, gated=gated, n_p=n_p),
        grid=(T // tm,),
        in_specs=in_specs,
        out_specs=row,
        out_shape=jax.ShapeDtypeStruct((T, D), F32),
        compiler_params=_params("parallel"),
        name="out_proj",
    )(*a_inputs, w, *hs)


def _ffn_kernel(h_ref, nf_ref, wg_ref, wu_ref, wd_ref, *refs, n_side):
    side_in, h2_ref = refs[:n_side], refs[n_side]
    side_out, (xn_ref,) = refs[n_side + 1:2 * n_side + 1], refs[2 * n_side + 1:]

    @pl.when(pl.program_id(1) == 0)
    def _():
        h = h_ref[...]
        xn_ref[...] = _rms_rows(h, nf_ref[...]).astype(BF16)
        h2_ref[...] = h

    xn = xn_ref[...]
    gate = _dot(xn, wg_ref[...])
    up = _dot(xn, wu_ref[...])
    h2_ref[...] += _dot((gate * _sigmoid(gate) * up).astype(BF16), wd_ref[...])
    _side_cast(side_in, side_out)


def _ffn(h, norm_ffn, w_gu, w_down, side=()):
    T, D = h.shape
    ff = w_down.shape[0]
    tm = _tile(T, 1024)
    tf = 512
    nf = ff // tf
    row = pl.BlockSpec((tm, D), lambda i, j: (i, 0))
    side_in, side_out, side_shape = _side_cast_specs(side, (T // tm) * nf, lambda i, j: i * nf + j)
    res = pl.pallas_call(
        functools.partial(_ffn_kernel, n_side=len(side)),
        grid=(T // tm, nf),
        in_specs=[row, pl.BlockSpec((1, D), lambda i, j: (0, 0)),
                  pl.BlockSpec((D, tf), lambda i, j: (0, j)),
                  pl.BlockSpec((D, tf), lambda i, j: (0, j + nf)),
                  pl.BlockSpec((tf, D), lambda i, j: (j, 0)), *side_in],
        out_specs=[row] + side_out,
        out_shape=[jax.ShapeDtypeStruct((T, D), F32)] + side_shape,
        scratch_shapes=[pltpu.VMEM((tm, D), BF16)],
        compiler_params=_params("arbitrary", "arbitrary"),
        name="ffn",
    )(h, norm_ffn, w_gu, w_gu, w_down, *[a for a, _ in side])
    return res[0], res[1:]


def _ple_kernel(h_ref, np_ref, *refs, n_p):
    *p_refs, wg_ref, wp_ref, out_ref = refs
    for r in _row_chunks(h_ref.shape[0]):
        h = h_ref[r, :]
        gate = _sigmoid(_dot(_rms_rows(h, np_ref[...]).astype(BF16), wg_ref[...]))
        if len(p_refs) == 2:
            p = jnp.where(pl.program_id(0) < n_p, p_refs[0][0, r, :], p_refs[1][0, r, :])
        else:
            p = p_refs[0][0, r, :]
        out_ref[r, :] = h + _dot(p.astype(BF16), wp_ref[...]) * gate


def _ple_call(h, norm_ple, ps, layer, w_gate, w_proj, tm, first_block, n_blocks, n_p):
    D = h.shape[1]
    pd = ps[0].shape[2]
    if len(ps) == 2:
        p_specs = [pl.BlockSpec((1, tm, pd), lambda i: (layer, jnp.minimum(i, n_p - 1), 0)),
                   pl.BlockSpec((1, tm, pd), lambda i: (layer, jnp.maximum(i - n_p, 0), 0))]
    else:
        p_specs = [pl.BlockSpec((1, tm, pd), lambda i: (layer, i, 0))]
    return pl.pallas_call(
        functools.partial(_ple_kernel, n_p=n_p),
        grid=(n_blocks,),
        in_specs=[pl.BlockSpec((tm, D), lambda i: (i + first_block, 0)),
                  pl.BlockSpec((1, D), lambda i: (0, 0)),
                  *p_specs, _resident((D, D)), _resident((pd, D))],
        out_specs=pl.BlockSpec((tm, D), lambda i: (i, 0)),
        out_shape=jax.ShapeDtypeStruct((n_blocks * tm, D), F32),
        compiler_params=_params("parallel"),
        name="ple",
    )(h, norm_ple, *ps, w_gate, w_proj)


def _ple(h, norm_ple, p_prompt, p_sample, layer, w_gate, w_proj, split_output):
    tp, ts = p_prompt.shape[1], p_sample.shape[1]
    tm = min(_tile(tp, 512), _tile(ts, 512))
    n_p, n_s = tp // tm, ts // tm
    args = (h, norm_ple)
    if split_output:
        return (_ple_call(*args, (p_prompt,), layer, w_gate, w_proj, tm, 0, n_p, n_p),
                _ple_call(*args, (p_sample,), layer, w_gate, w_proj, tm, n_p, n_s, n_p))
    return _ple_call(*args, (p_prompt, p_sample), layer, w_gate, w_proj, tm, 0, n_p + n_s, n_p)


def _qkv_kernel(h_ref, nm_ref, w_ref, seg_ref, gain_ref, q_ref, kv_ref, xn_ref, *, n_q_tiles, k_width):
    j = pl.program_id(1)

    @pl.when(j == 0)
    def _():
        xn_ref[...] = _rms_rows(h_ref[...], nm_ref[...]).astype(BF16)

    x = _dot(xn_ref[...], w_ref[...])
    ms = _dot((x * x).astype(BF16), seg_ref[...]) * (1.0 / ATT_HD)
    normed = x * lax.rsqrt(ms + RMS_EPS) * gain_ref[0]

    @pl.when(j < n_q_tiles)
    def _():
        q_ref[...] = (normed * (ATT_HD ** -0.5)).astype(BF16)

    @pl.when(j == n_q_tiles)
    def _():
        lane = lax.broadcasted_iota(jnp.int32, x.shape, 1)
        kv_ref[...] = jnp.where(lane < k_width, normed, x)


def _qkv(h, norm_mix, w_qkv, q_gain, k_gain):
    T, D = h.shape
    n_out = w_qkv.shape[1]
    tn = ATT_GROUP * ATT_HD
    n_tiles = n_out // tn
    n_q_tiles = D // tn
    assert n_tiles == n_q_tiles + 1
    kv_w = (n_out - D) // 2
    tm = _tile(T, 1024)
    seg = np.kron(np.eye(tn // ATT_HD, dtype=np.float32), np.ones((ATT_HD, ATT_HD), np.float32))
    q_row = jnp.tile(q_gain.astype(F32), tn // ATT_HD)
    kv_row = jnp.concatenate([jnp.tile(k_gain.astype(F32), kv_w // ATT_HD), jnp.ones((tn - kv_w,), F32)])
    gain = jnp.stack([q_row] * n_q_tiles + [kv_row])[:, None, :]
    return pl.pallas_call(
        functools.partial(_qkv_kernel, n_q_tiles=n_q_tiles, k_width=kv_w),
        grid=(T // tm, n_tiles),
        in_specs=[pl.BlockSpec((tm, D), lambda i, j: (i, 0)),
                  pl.BlockSpec((1, D), lambda i, j: (0, 0)),
                  pl.BlockSpec((D, tn), lambda i, j: (0, j)),
                  pl.BlockSpec((tn, tn), lambda i, j: (0, 0)),
                  pl.BlockSpec((1, 1, tn), lambda i, j: (j, 0, 0))],
        out_specs=[pl.BlockSpec((tm, tn), lambda i, j: (i, jnp.minimum(j, n_q_tiles - 1))),
                   pl.BlockSpec((tm, tn), lambda i, j: (i, 0))],
        out_shape=[jax.ShapeDtypeStruct((T, D), BF16), jax.ShapeDtypeStruct((T, tn), F32)],
        scratch_shapes=[pltpu.VMEM((tm, D), BF16)],
        compiler_params=_params("parallel", "arbitrary"),
        name="qkv_proj",
    )(h, norm_mix, w_qkv, jnp.asarray(seg, BF16), gain)


def _alibi_slopes(n_heads):
    return [float(np.float32(2.0) ** np.float32(-8.0 * (i + 1) / n_heads)) for i in range(n_heads)]


def _attn_kernel(scal_ref, q_ref, kvc_ref, kv1_ref, kv2_ref, ck_ref, cv_ref, o_ref,
                 *, n_prompt_chunks, chunks_per_seq, n_kv_heads):
    i = pl.program_id(0)
    is_sample = i >= n_prompt_chunks
    c = lax.rem(i, chunks_per_seq)
    first_valid = jnp.where(is_sample, 0, jnp.maximum(ATT_PREV - c, 0) * CHUNK)
    kvw = n_kv_heads * ATT_HD
    n_keys = (ATT_PREV + 1) * CHUNK
    n_heads = n_kv_heads * ATT_GROUP
    slopes = _alibi_slopes(n_heads)
    pairs = ATT_GROUP // 2
    qw = ATT_GROUP * ATT_HD
    lanes = 2 * ATT_HD

    kvc = kvc_ref[...]
    kv1 = kv1_ref[...]
    kv2 = kv2_ref[...]
    k_prev = jnp.where(is_sample, ck_ref[0], jnp.concatenate([kv2[:, :kvw], kv1[:, :kvw]], axis=0))
    v_prev = jnp.where(is_sample, cv_ref[0], jnp.concatenate([kv2[:, kvw:], kv1[:, kvw:]], axis=0))
    k_all = jnp.concatenate([k_prev, kvc[:, :kvw]], axis=0)
    v_all = jnp.concatenate([v_prev, kvc[:, kvw:]], axis=0)

    low = lax.broadcasted_iota(jnp.int32, (n_keys, lanes), 1) < ATT_HD
    qi = lax.broadcasted_iota(jnp.int32, (CHUNK, n_keys), 0)
    kj = lax.broadcasted_iota(jnp.int32, (CHUNK, n_keys), 1)
    dist = jnp.abs(ATT_PREV * CHUNK + qi - kj).astype(F32)
    valid = kj >= first_valid

    def split_halves(t, odd):
        rolled = pltpu.roll(t, ATT_HD, 1)
        in_low, in_high = (rolled, t) if odd else (t, rolled)
        zero = jnp.zeros_like(t)
        return jnp.where(low, in_low, zero).astype(BF16), jnp.where(low, zero, in_high).astype(BF16)

    def kv_head_operands(kh):
        col = (kh // 2) * lanes
        ks = split_halves(k_all[:, col:col + lanes], kh % 2)
        vs = split_halves(v_all[:, col:col + lanes], kh % 2)
        qs = jnp.concatenate([q_ref[:, kh * qw + p * lanes: kh * qw + (p + 1) * lanes] for p in range(pairs)],
                             axis=0)
        return qs, ks, vs

    bound = scal_ref[n_heads] * (ATT_HD ** 0.5) * jnp.max(jnp.abs(k_all))
    small = bound <= ATT_SAFE_EXP

    @pl.when(small)
    def _():
        ones = jnp.ones((n_keys, lanes), BF16)
        units = [(kh, parity) for kh in range(n_kv_heads) for parity in range(2)]
        ops = [kv_head_operands(kh) for kh in range(n_kv_heads)]
        scores = [lax.dot_general(ops[kh][0], ops[kh][1][parity], NT_DIMS, preferred_element_type=F32)
                  for kh, parity in units]
        probs = []
        for (kh, parity), s_all in zip(units, scores):
            rows = []
            for p in range(pairs):
                head = kh * ATT_GROUP + 2 * p + parity
                s = s_all[p * CHUNK:(p + 1) * CHUNK] + dist * (-slopes[head])
                rows.append(jnp.exp(jnp.where(valid, s, -jnp.inf)).astype(BF16))
            probs.append(jnp.concatenate(rows, axis=0))
        both = [_dot(pr, jnp.concatenate([ops[kh][2][parity], ones], axis=1))
                for (kh, parity), pr in zip(units, probs)]
        for kh in range(n_kv_heads):
            for p in range(pairs):
                rows = slice(p * CHUNK, (p + 1) * CHUNK)
                out = None
                for parity in range(2):
                    head = kh * ATT_GROUP + 2 * p + parity
                    b = both[2 * kh + parity]
                    den = b[rows, lanes:] + jnp.exp(jnp.full((1, lanes), scal_ref[head], F32))
                    part = b[rows, :lanes] / den
                    out = part if out is None else out + part
                o_ref[:, kh * qw + p * lanes: kh * qw + (p + 1) * lanes] = out.astype(BF16)

    @pl.when(jnp.logical_not(small))
    def _():
        for kh in range(n_kv_heads):
            qs, ks, vs = kv_head_operands(kh)
            acc = None
            for parity in range(2):
                s_all = lax.dot_general(qs, ks[parity], NT_DIMS, preferred_element_type=F32)
                probs, inv = [], []
                for p in range(pairs):
                    head = kh * ATT_GROUP + 2 * p + parity
                    s = s_all[p * CHUNK:(p + 1) * CHUNK] + dist * (-slopes[head])
                    s = jnp.where(valid, s, -jnp.inf)
                    sink = scal_ref[head]
                    m = jnp.maximum(jnp.max(s, axis=-1, keepdims=True), sink)
                    e = jnp.exp(s - m)
                    den = jnp.sum(e, axis=-1, keepdims=True) + jnp.exp(sink - m)
                    probs.append(e.astype(BF16))
                    inv.append(1.0 / den)
                part = _dot(jnp.concatenate(probs, axis=0), vs[parity]) * jnp.concatenate(inv, axis=0)
                acc = part if acc is None else acc + part
            for p in range(pairs):
                o_ref[:, kh * qw + p * lanes: kh * qw + (p + 1) * lanes] = (
                    acc[p * CHUNK:(p + 1) * CHUNK].astype(BF16))


def _attention(q, kv, cache_k, cache_v, scalars, n_prompt_seq, chunks_per_seq):
    T, d_model = q.shape
    n_chunks = T // CHUNK
    ncp = n_prompt_seq * chunks_per_seq
    kvw2 = kv.shape[1]
    n_kv_heads = kvw2 // (2 * ATT_HD)
    n_cache = cache_k.shape[1]
    assert n_cache == ATT_PREV * CHUNK

    def prev_map(back):
        def index(i):
            ok = jnp.logical_and(i < ncp, lax.rem(i, chunks_per_seq) >= back)
            return (jnp.where(ok, i - back, i), 0)
        return index

    kv_blk = (CHUNK, kvw2)
    cache_spec = pl.BlockSpec((1, n_cache, kvw2 // 2), lambda i: (jnp.maximum(i - ncp, 0), 0, 0))
    kern = functools.partial(_attn_kernel, n_prompt_chunks=ncp, chunks_per_seq=chunks_per_seq,
                             n_kv_heads=n_kv_heads)
    return pl.pallas_call(
        kern,
        grid=(n_chunks,),
        in_specs=[pl.BlockSpec(memory_space=pltpu.SMEM),
                  pl.BlockSpec((CHUNK, d_model), lambda i: (i, 0)),
                  pl.BlockSpec(kv_blk, lambda i: (i, 0)),
                  pl.BlockSpec(kv_blk, prev_map(1)),
                  pl.BlockSpec(kv_blk, prev_map(2)),
                  cache_spec, cache_spec],
        out_specs=pl.BlockSpec((CHUNK, d_model), lambda i: (i, 0)),
        out_shape=jax.ShapeDtypeStruct((T, d_model), BF16),
        compiler_params=_params("parallel"),
        name="swa_attention",
    )(scalars, q, kv, kv, kv, cache_k, cache_v)


def kernel(x_prompt, x_sample, state_hgrn, cache_k, cache_v, p_prompt, p_sample, norm_mix, norm_ffn,
           norm_ple, a_w_in, a_lb_logits, a_g_norm, a_w_o, b_w_qkv, b_q_norm, b_k_norm, b_sinks, b_w_o,
           f_w_gu, f_w_down, ple_w_proj, ple_w_gate):
    B, L, D = x_prompt.shape
    Bs, Ls, _ = x_sample.shape
    assert L % CHUNK == 0 and Ls == CHUNK
    tp, ts = B * L, Bs * Ls
    cps = L // CHUNK
    depth = norm_mix.shape[0]
    assert depth == 2

    xp, xs = x_prompt.reshape(tp, D), x_sample.reshape(ts, D)
    pp, ps = p_prompt.reshape(depth, tp, -1), p_sample.reshape(depth, ts, -1)
    vec = lambda w: w.reshape(1, -1).astype(F32)

    (q, k, v, lf, gs), (w_gu0, w_down0, w_o0, w_gate0, w_proj0) = _hgrn_in(
        xp, xs, vec(norm_mix[0]), a_w_in[0].astype(BF16), a_lb_logits.astype(F32),
        side=[(f_w_gu, 0), (f_w_down, 0), (a_w_o, 0), (ple_w_gate, 0), (ple_w_proj, 0)])

    o, state_p, state_s = _gla(q, k, v, lf, state_hgrn[0].astype(F32), B, cps)
    h1 = _out_proj((o, gs, vec(a_g_norm[0])), w_o0, (xp, xs), gated=True)
    h2, (w_gu1, w_down1, w_qkv, w_o1, w_gate1, w_proj1) = _ffn(
        h1, vec(norm_ffn[0]), w_gu0, w_down0,
        side=[(f_w_gu, 1), (f_w_down, 1), (b_w_qkv, 0), (b_w_o, 0), (ple_w_gate, 1), (ple_w_proj, 1)])
    h = _ple(h2, vec(norm_ple[0]), pp, ps, 0, w_gate0, w_proj0, split_output=False)

    qn, kvn = _qkv(h, vec(norm_mix[1]), w_qkv, b_q_norm[0], b_k_norm[0])
    n_kv = cache_k.shape[3]
    kvw = n_kv * ATT_HD
    scalars = jnp.concatenate([b_sinks[0].astype(F32), jnp.max(jnp.abs(b_q_norm[0])).reshape(1).astype(F32)])
    att = _attention(qn, kvn, cache_k[0].reshape(Bs, -1, kvw).astype(F32),
                     cache_v[0].reshape(Bs, -1, kvw).astype(F32), scalars, B, cps)
    h1 = _out_proj((att,), w_o1, (h,), gated=False)
    h2, _ = _ffn(h1, vec(norm_ffn[1]), w_gu1, w_down1)
    yp, ys = _ple(h2, vec(norm_ple[1]), pp, ps, 1, w_gate1, w_proj1, split_output=True)

    keep = min(ATT_PREV * CHUNK, L)
    kv_p = kvn[:tp].reshape(B, L, 2 * kvw)[:, L - keep:]
    kv_s = kvn[tp:].reshape(Bs, Ls, 2 * kvw)
    heads = lambda t: t.reshape(*t.shape[:2], n_kv, ATT_HD)[None]
    return (yp.reshape(B, L, D), ys.reshape(Bs, Ls, D), state_p[None], state_s[None],
            heads(kv_p[..., :kvw]), heads(kv_p[..., kvw:]), heads(kv_s[..., :kvw]), heads(kv_s[..., kvw:]))
```

```python
import functools

import numpy as np
import jax
import jax.numpy as jnp
from jax import lax
from jax.experimental import pallas as pl
from jax.experimental.pallas import tpu as pltpu

F32 = jnp.float32
BF16 = jnp.bfloat16

RMS_EPS = 1e-6
CHUNK = 64
HEAD_DK = 128
ATT_HD = 64
ATT_GROUP = 8
ATT_PREV = 2
HGRN_TILE = 256
GLA_LEVELS = (64, 32, 16, 8, 4, 2)
GLA_SAFE_EXP = 60.0
ATT_SAFE_EXP = 70.0

NT_DIMS = (((1,), (1,)), ((), ()))
TN_DIMS = (((0,), (0,)), ((), ()))

VMEM_LIMIT_BYTES = 60 * 1024 * 1024


def _params(*sem):
    return pltpu.CompilerParams(dimension_semantics=sem, vmem_limit_bytes=VMEM_LIMIT_BYTES)


def _tile(n, pref):
    t = pref
    while t > 8 and n % t:
        t //= 2
    assert n % t == 0, (n, pref)
    return t


def _split_rows(tm, n_p, width, sample_buffers=None):
    mode = {} if sample_buffers is None else dict(pipeline_mode=pl.Buffered(sample_buffers))
    return (pl.BlockSpec((tm, width), lambda i, *_: (jnp.minimum(i, n_p - 1), 0)),
            pl.BlockSpec((tm, width), lambda i, *_: (jnp.maximum(i - n_p, 0), 0), **mode))


def _resident(shape):
    return pl.BlockSpec(shape, lambda *_: (0,) * len(shape), pipeline_mode=pl.Buffered(1))


def _row_chunks(rows, size=256):
    size = min(size, rows)
    assert rows % size == 0
    return [slice(r, r + size) for r in range(0, rows, size)]


def _side_cast_specs(side, n_steps, step_of):
    in_specs, out_specs, out_shape = [], [], []
    for arr, idx in side:
        _, rows, cols = arr.shape
        rb = 16
        while rows // rb > n_steps:
            rb *= 2
        assert rows % rb == 0
        last = rows // rb - 1
        in_specs.append(pl.BlockSpec(
            (1, rb, cols), lambda *g, idx=idx, last=last: (idx, jnp.minimum(step_of(*g), last), 0)))
        out_specs.append(pl.BlockSpec((rb, cols), lambda *g, last=last: (jnp.minimum(step_of(*g), last), 0)))
        out_shape.append(jax.ShapeDtypeStruct((rows, cols), BF16))
    return in_specs, out_specs, out_shape


def _side_cast(in_refs, out_refs):
    for src, dst in zip(in_refs, out_refs):
        dst[...] = src[0].astype(BF16)


def _sigmoid(x):
    return 1.0 / (1.0 + jnp.exp(-x))


def _rms_rows(x, w):
    ms = jnp.mean(x * x, axis=-1, keepdims=True)
    return x * lax.rsqrt(ms + RMS_EPS) * w


def _dot(a, b):
    return jnp.dot(a, b, preferred_element_type=F32)


def _hgrn_in_kernel(xp_ref, xs_ref, nw_ref, wq_ref, wf_ref, wi_ref, wg_ref, lbl_ref, *refs, n_p, n_side):
    side_in, refs = refs[:n_side], refs[n_side:]
    q_ref, k_ref, v_ref, lf_ref, gs_ref = refs[:5]
    side_out, (xn_ref,) = refs[5:5 + n_side], refs[5 + n_side:]
    first = pl.program_id(1) == 0
    is_prompt = pl.program_id(0) < n_p

    @pl.when(jnp.logical_and(first, is_prompt))
    def _():
        xn_ref[...] = _rms_rows(xp_ref[...], nw_ref[...]).astype(BF16)

    @pl.when(jnp.logical_and(first, jnp.logical_not(is_prompt)))
    def _():
        xn_ref[...] = _rms_rows(xs_ref[...], nw_ref[...]).astype(BF16)

    xn = xn_ref[...]
    q = _dot(xn, wq_ref[0, 0])
    q_ref[0] = (q * _sigmoid(q)).astype(BF16)
    logits = lbl_ref[...]
    ex = jnp.exp(logits - jnp.max(logits, axis=0, keepdims=True))
    lb = ex[0:1] / jnp.sum(ex, axis=0, keepdims=True)
    f = _dot(xn, wf_ref[0, 0])
    forget = lb + (1.0 - lb) * _sigmoid(f)
    k_ref[0] = (1.0 - forget).astype(BF16)
    lf_ref[0] = jnp.log(forget)
    v_ref[0] = _dot(xn, wi_ref[0, 0]).astype(BF16)
    g = _dot(xn, wg_ref[0, 0])
    gs_ref[0] = (g * _sigmoid(g)).astype(BF16)
    _side_cast(side_in, side_out)


def _hgrn_in(xp, xs, norm_w, w_in, lb_logits, side):
    (tp, D), ts = xp.shape, xs.shape[0]
    T = tp + ts
    tm = min(_tile(tp, 1024), _tile(ts, 1024))
    tn = HGRN_TILE
    nb = D // tn
    out = pl.BlockSpec((1, tm, tn), lambda i, j: (j, i, 0))

    def wspec(g):
        return pl.BlockSpec((1, 1, D, tn), lambda i, j, g=g: (g, j, 0, 0))

    side_in, side_out, side_shape = _side_cast_specs(side, (T // tm) * nb, lambda i, j: i * nb + j)
    res = pl.pallas_call(
        functools.partial(_hgrn_in_kernel, n_p=tp // tm, n_side=len(side)),
        grid=(T // tm, nb),
        in_specs=[*_split_rows(tm, tp // tm, D, sample_buffers=1), pl.BlockSpec((1, D), lambda i, j: (0, 0)),
                  wspec(0), wspec(1), wspec(2), wspec(3),
                  pl.BlockSpec((lb_logits.shape[0], tn), lambda i, j: (0, j)), *side_in],
        out_specs=[out] * 5 + side_out,
        out_shape=[jax.ShapeDtypeStruct((nb, T, tn), dt) for dt in (BF16, BF16, BF16, F32, BF16)] + side_shape,
        scratch_shapes=[pltpu.VMEM((tm, D), BF16)],
        compiler_params=_params("arbitrary", "arbitrary"),
        name="hgrn_in",
    )(xp, xs, norm_w, w_in, w_in, w_in, w_in, lb_logits, *[a for a, _ in side])
    return res[:5], res[5:]


def _gla_constants():
    c = CHUNK
    t = np.arange(c)[:, None]
    s = np.arange(c)[None, :]
    zs = [s <= t, s > t]
    level = np.full((c, c), -1, np.int32)
    for li, p in enumerate(GLA_LEVELS, start=1):
        half = p // 2
        mid = (t // p) * p + half - 1
        upper = (t % p) >= half
        zs.append(np.where(upper, (s > mid) & (s <= t), (s > t) & (s <= mid)))
        level[(t // p == s // p) & upper & ((s % p) < half)] = li
    np.fill_diagonal(level, 0)
    z = np.concatenate(zs, axis=0).astype(np.float32)
    zf = np.concatenate([zs[0], zs[0]], axis=0).astype(np.float32)
    zf[c:] *= -1.0
    three = lambda m: np.concatenate([m, m, m], axis=1)
    return three(z), three(zf), level


def _gla_kernel(q_ref, k_ref, v_ref, lf_ref, s0_ref, z_ref, zf_ref, lvl_ref, o_ref, sp_ref, ss_ref,
                st_ref, e_ref, *, n_prompt_chunks, chunks_per_seq, n_heads):
    i = pl.program_id(0)
    is_prompt = i < n_prompt_chunks
    c = jnp.where(is_prompt, lax.rem(i, chunks_per_seq), 0)

    @pl.when(jnp.logical_and(is_prompt, c == 0))
    def _():
        st_ref[...] = jnp.zeros_like(st_ref)

    @pl.when(jnp.logical_not(is_prompt))
    def _():
        for h in range(n_heads):
            st_ref[h] = s0_ref[0, h].T

    n_tiles = lf_ref.shape[0]
    g = jnp.concatenate([lf_ref[t] for t in range(n_tiles)], axis=1)
    g_hi = g.astype(BF16)
    r1 = g - g_hi.astype(F32)
    g_mid = r1.astype(BF16)
    g_lo = (r1 - g_mid.astype(F32)).astype(BF16)
    gcat = jnp.concatenate([g_hi, g_mid, g_lo], axis=0)
    lvl = lvl_ref[...]
    per_tile = lf_ref.shape[2] // HEAD_DK

    def head(ref, h):
        lane = (h % per_tile) * HEAD_DK
        return ref[h // per_tile, :, lane:lane + HEAD_DK]

    mild = jnp.min(jnp.sum(g, axis=0, keepdims=True)) >= -GLA_SAFE_EXP

    @pl.when(mild)
    def _():
        e_ref[0:2 * CHUNK, :] = jnp.exp(_dot(zf_ref[...], gcat))
        sls = [slice(h * HEAD_DK, (h + 1) * HEAD_DK) for h in range(n_heads)]
        q_hat = [(head(q_ref, h).astype(F32) * e_ref[0:CHUNK, sl]).astype(BF16) for h, sl in enumerate(sls)]
        k_bar = [(head(k_ref, h).astype(F32) * e_ref[CHUNK:2 * CHUNK, sl]).astype(BF16)
                 for h, sl in enumerate(sls)]
        a = [lax.dot_general(q_hat[h], k_bar[h], NT_DIMS, preferred_element_type=F32) for h in range(n_heads)]
        o = [lax.dot_general(q_hat[h], st_ref[h].astype(BF16), NT_DIMS, preferred_element_type=F32)
             for h in range(n_heads)]
        for h, sl in enumerate(sls):
            o_ref[:, sl] = (o[h] + _dot(jnp.where(lvl >= 0, a[h], 0.0).astype(BF16), head(v_ref, h))).astype(BF16)
        for h, sl in enumerate(sls):
            decay_all = e_ref[CHUNK - 1:CHUNK, sl]
            k_hat = (head(k_ref, h).astype(F32) * (e_ref[CHUNK:2 * CHUNK, sl] * decay_all)).astype(BF16)
            st_ref[h] = st_ref[h] * decay_all + lax.dot_general(head(v_ref, h), k_hat, TN_DIMS,
                                                                 preferred_element_type=F32)

    @pl.when(jnp.logical_not(mild))
    def _():
        e_ref[...] = jnp.exp(_dot(z_ref[...], gcat))
        row = lax.broadcasted_iota(jnp.int32, (CHUNK, HEAD_DK), 0)
        for h in range(n_heads):
            sl = slice(h * HEAD_DK, (h + 1) * HEAD_DK)
            qb = head(q_ref, h)
            kb = head(k_ref, h)
            vb = head(v_ref, h)
            qf = qb.astype(F32)
            kf = kb.astype(F32)
            st = st_ref[h]
            q_hat = (qf * e_ref[0:CHUNK, sl]).astype(BF16)
            k_hat = (kf * e_ref[CHUNK:2 * CHUNK, sl]).astype(BF16)
            o = lax.dot_general(q_hat, st.astype(BF16), NT_DIMS, preferred_element_type=F32)
            a = lax.dot_general(qb, kb, NT_DIMS, preferred_element_type=F32)
            a = jnp.where(lvl == 0, a, 0.0)
            for li, p in enumerate(GLA_LEVELS, start=1):
                on_q_side = (row & (p // 2)) != 0
                x = (jnp.where(on_q_side, qf, kf)
                     * e_ref[(li + 1) * CHUNK:(li + 2) * CHUNK, sl]).astype(BF16)
                a = jnp.where(lvl == li, lax.dot_general(x, x, NT_DIMS, preferred_element_type=F32), a)
            o_ref[:, sl] = (o + _dot(a.astype(BF16), vb)).astype(BF16)
            decay_all = e_ref[CHUNK - 1:CHUNK, sl]
            st_ref[h] = st * decay_all + lax.dot_general(vb, k_hat, TN_DIMS, preferred_element_type=F32)

    @pl.when(jnp.logical_and(is_prompt, c == chunks_per_seq - 1))
    def _():
        for h in range(n_heads):
            sp_ref[0, h] = st_ref[h].T

    @pl.when(jnp.logical_not(is_prompt))
    def _():
        for h in range(n_heads):
            ss_ref[0, h] = st_ref[h].T


def _gla(q, k, v, lf, s0, n_prompt_seq, chunks_per_seq):
    n_tiles, T, tile = q.shape
    D = n_tiles * tile
    n_heads = D // HEAD_DK
    n_chunks = T // CHUNK
    ncp = n_prompt_seq * chunks_per_seq
    zcat, zfast, level = _gla_constants()
    blk = pl.BlockSpec((CHUNK, D), lambda i: (i, 0))
    tiled = pl.BlockSpec((n_tiles, CHUNK, tile), lambda i: (0, i, 0))
    st_blk = (1, n_heads, HEAD_DK, HEAD_DK)

    def prompt_seq(i):
        return (jnp.minimum(i // chunks_per_seq, n_prompt_seq - 1), 0, 0, 0)

    kern = functools.partial(_gla_kernel, n_prompt_chunks=ncp, chunks_per_seq=chunks_per_seq,
                             n_heads=n_heads)
    return pl.pallas_call(
        kern,
        grid=(n_chunks,),
        in_specs=[tiled, tiled, tiled, tiled,
                  pl.BlockSpec(st_blk, lambda i: (jnp.maximum(i - ncp, 0), 0, 0, 0)),
                  pl.BlockSpec(zcat.shape, lambda i: (0, 0)),
                  pl.BlockSpec(zfast.shape, lambda i: (0, 0)),
                  pl.BlockSpec(level.shape, lambda i: (0, 0))],
        out_specs=[blk, pl.BlockSpec(st_blk, prompt_seq),
                   pl.BlockSpec(st_blk, lambda i: (jnp.maximum(i - ncp, 0), 0, 0, 0))],
        out_shape=[jax.ShapeDtypeStruct((T, D), BF16),
                   jax.ShapeDtypeStruct((n_prompt_seq, n_heads, HEAD_DK, HEAD_DK), F32),
                   jax.ShapeDtypeStruct((n_chunks - ncp, n_heads, HEAD_DK, HEAD_DK), F32)],
        scratch_shapes=[pltpu.VMEM((n_heads, HEAD_DK, HEAD_DK), F32),
                        pltpu.VMEM((zcat.shape[0], D), F32)],
        compiler_params=_params("arbitrary"),
        name="gla_scan",
    )(q, k, v, lf, s0, jnp.asarray(zcat, BF16), jnp.asarray(zfast, BF16), jnp.asarray(level))


def _out_proj_kernel(*refs, gated, n_p):
    if gated:
        o_ref, gs_ref, gn_ref, w_ref, *h_refs, h1_ref = refs
    else:
        a_ref, w_ref, *h_refs, h1_ref = refs
    for r in _row_chunks(h1_ref.shape[0]):
        if gated:
            gs = jnp.concatenate([gs_ref[t, r, :] for t in range(gs_ref.shape[0])], axis=1)
            a = (_rms_rows(o_ref[r, :].astype(F32), gn_ref[...]) * gs.astype(F32)).astype(BF16)
        else:
            a = a_ref[r, :]
        if len(h_refs) == 2:
            h = jnp.where(pl.program_id(0) < n_p, h_refs[0][r, :], h_refs[1][r, :])
        else:
            h = h_refs[0][r, :]
        h1_ref[r, :] = h + _dot(a, w_ref[...])


def _out_proj(a_inputs, w, hs, gated):
    T, D = a_inputs[0].shape
    tm = min(_tile(h.shape[0], 512) for h in hs)
    row = pl.BlockSpec((tm, D), lambda i: (i, 0))
    vec = pl.BlockSpec((1, D), lambda i: (0, 0))
    wspec = _resident((D, D))
    n_p = hs[0].shape[0] // tm
    h_specs = [row] if len(hs) == 1 else list(_split_rows(tm, n_p, D))
    if gated:
        n_tiles, _, tile = a_inputs[1].shape
        in_specs = [row, pl.BlockSpec((n_tiles, tm, tile), lambda i: (0, i, 0)), vec, wspec, *h_specs]
    else:
        in_specs = [row, wspec, *h_specs]
    return pl.pallas_call(
        functools.partial(_out_proj_kernel, gated=gated, n_p=n_p),
        grid=(T // tm,),
        in_specs=in_specs,
        out_specs=row,
        out_shape=jax.ShapeDtypeStruct((T, D), F32),
        compiler_params=_params("parallel"),
        name="out_proj",
    )(*a_inputs, w, *hs)


def _ffn_kernel(h_ref, nf_ref, wg_ref, wu_ref, wd_ref, *refs, n_side):
    side_in, h2_ref = refs[:n_side], refs[n_side]
    side_out, (xn_ref,) = refs[n_side + 1:2 * n_side + 1], refs[2 * n_side + 1:]

    @pl.when(pl.program_id(1) == 0)
    def _():
        h = h_ref[...]
        xn_ref[...] = _rms_rows(h, nf_ref[...]).astype(BF16)
        h2_ref[...] = h

    xn = xn_ref[...]
    gate = _dot(xn, wg_ref[...])
    up = _dot(xn, wu_ref[...])
    h2_ref[...] += _dot((gate * _sigmoid(gate) * up).astype(BF16), wd_ref[...])
    _side_cast(side_in, side_out)


def _ffn(h, norm_ffn, w_gu, w_down, side=()):
    T, D = h.shape
    ff = w_down.shape[0]
    tm = _tile(T, 1024)
    tf = 512
    nf = ff // tf
    row = pl.BlockSpec((tm, D), lambda i, j: (i, 0))
    side_in, side_out, side_shape = _side_cast_specs(side, (T // tm) * nf, lambda i, j: i * nf + j)
    res = pl.pallas_call(
        functools.partial(_ffn_kernel, n_side=len(side)),
        grid=(T // tm, nf),
        in_specs=[row, pl.BlockSpec((1, D), lambda i, j: (0, 0)),
                  pl.BlockSpec((D, tf), lambda i, j: (0, j)),
                  pl.BlockSpec((D, tf), lambda i, j: (0, j + nf)),
                  pl.BlockSpec((tf, D), lambda i, j: (j, 0)), *side_in],
        out_specs=[row] + side_out,
        out_shape=[jax.ShapeDtypeStruct((T, D), F32)] + side_shape,
        scratch_shapes=[pltpu.VMEM((tm, D), BF16)],
        compiler_params=_params("arbitrary", "arbitrary"),
        name="ffn",
    )(h, norm_ffn, w_gu, w_gu, w_down, *[a for a, _ in side])
    return res[0], res[1:]


def _ple_kernel(h_ref, np_ref, *refs, n_p):
    *p_refs, wg_ref, wp_ref, out_ref = refs
    for r in _row_chunks(h_ref.shape[0]):
        h = h_ref[r, :]
        gate = _sigmoid(_dot(_rms_rows(h, np_ref[...]).astype(BF16), wg_ref[...]))
        if len(p_refs) == 2:
            p = jnp.where(pl.program_id(0) < n_p, p_refs[0][0, r, :], p_refs[1][0, r, :])
        else:
            p = p_refs[0][0, r, :]
        out_ref[r, :] = h + _dot(p.astype(BF16), wp_ref[...]) * gate


def _ple_call(h, norm_ple, ps, layer, w_gate, w_proj, tm, first_block, n_blocks, n_p):
    D = h.shape[1]
    pd = ps[0].shape[2]
    if len(ps) == 2:
        p_specs = [pl.BlockSpec((1, tm, pd), lambda i: (layer, jnp.minimum(i, n_p - 1), 0)),
                   pl.BlockSpec((1, tm, pd), lambda i: (layer, jnp.maximum(i - n_p, 0), 0))]
    else:
        p_specs = [pl.BlockSpec((1, tm, pd), lambda i: (layer, i, 0))]
    return pl.pallas_call(
        functools.partial(_ple_kernel, n_p=n_p),
        grid=(n_blocks,),
        in_specs=[pl.BlockSpec((tm, D), lambda i: (i + first_block, 0)),
                  pl.BlockSpec((1, D), lambda i: (0, 0)),
                  *p_specs, _resident((D, D)), _resident((pd, D))],
        out_specs=pl.BlockSpec((tm, D), lambda i: (i, 0)),
        out_shape=jax.ShapeDtypeStruct((n_blocks * tm, D), F32),
        compiler_params=_params("parallel"),
        name="ple",
    )(h, norm_ple, *ps, w_gate, w_proj)


def _ple(h, norm_ple, p_prompt, p_sample, layer, w_gate, w_proj, split_output):
    tp, ts = p_prompt.shape[1], p_sample.shape[1]
    tm = min(_tile(tp, 512), _tile(ts, 512))
    n_p, n_s = tp // tm, ts // tm
    args = (h, norm_ple)
    if split_output:
        return (_ple_call(*args, (p_prompt,), layer, w_gate, w_proj, tm, 0, n_p, n_p),
                _ple_call(*args, (p_sample,), layer, w_gate, w_proj, tm, n_p, n_s, n_p))
    return _ple_call(*args, (p_prompt, p_sample), layer, w_gate, w_proj, tm, 0, n_p + n_s, n_p)


def _qkv_kernel(h_ref, nm_ref, w_ref, seg_ref, gain_ref, q_ref, kv_ref, xn_ref, *, n_q_tiles, k_width):
    j = pl.program_id(1)

    @pl.when(j == 0)
    def _():
        xn_ref[...] = _rms_rows(h_ref[...], nm_ref[...]).astype(BF16)

    x = _dot(xn_ref[...], w_ref[...])
    ms = _dot((x * x).astype(BF16), seg_ref[...]) * (1.0 / ATT_HD)
    normed = x * lax.rsqrt(ms + RMS_EPS) * gain_ref[0]

    @pl.when(j < n_q_tiles)
    def _():
        q_ref[...] = normed.astype(BF16)

    @pl.when(j == n_q_tiles)
    def _():
        lane = lax.broadcasted_iota(jnp.int32, x.shape, 1)
        kv_ref[...] = jnp.where(lane < k_width, normed, x)


def _qkv(h, norm_mix, w_qkv, q_gain, k_gain):
    T, D = h.shape
    n_out = w_qkv.shape[1]
    tn = ATT_GROUP * ATT_HD
    n_tiles = n_out // tn
    n_q_tiles = D // tn
    assert n_tiles == n_q_tiles + 1
    kv_w = (n_out - D) // 2
    tm = _tile(T, 1024)
    seg = np.kron(np.eye(tn // ATT_HD, dtype=np.float32), np.ones((ATT_HD, ATT_HD), np.float32))
    q_row = jnp.tile(q_gain.astype(F32), tn // ATT_HD) * (ATT_HD ** -0.5)
    kv_row = jnp.concatenate([jnp.tile(k_gain.astype(F32), kv_w // ATT_HD), jnp.ones((tn - kv_w,), F32)])
    gain = jnp.stack([q_row] * n_q_tiles + [kv_row])[:, None, :]
    return pl.pallas_call(
        functools.partial(_qkv_kernel, n_q_tiles=n_q_tiles, k_width=kv_w),
        grid=(T // tm, n_tiles),
        in_specs=[pl.BlockSpec((tm, D), lambda i, j: (i, 0)),
                  pl.BlockSpec((1, D), lambda i, j: (0, 0)),
                  pl.BlockSpec((D, tn), lambda i, j: (0, j)),
                  pl.BlockSpec((tn, tn), lambda i, j: (0, 0)),
                  pl.BlockSpec((1, 1, tn), lambda i, j: (j, 0, 0))],
        out_specs=[pl.BlockSpec((tm, tn), lambda i, j: (i, jnp.minimum(j, n_q_tiles - 1))),
                   pl.BlockSpec((tm, tn), lambda i, j: (i, 0))],
        out_shape=[jax.ShapeDtypeStruct((T, D), BF16), jax.ShapeDtypeStruct((T, tn), F32)],
        scratch_shapes=[pltpu.VMEM((tm, D), BF16)],
        compiler_params=_params("parallel", "arbitrary"),
        name="qkv_proj",
    )(h, norm_mix, w_qkv, jnp.asarray(seg, BF16), gain)


def _alibi_slopes(n_heads):
    return [float(np.float32(2.0) ** np.float32(-8.0 * (i + 1) / n_heads)) for i in range(n_heads)]


def _attn_kernel(scal_ref, q_ref, kvc_ref, kv1_ref, kv2_ref, ck_ref, cv_ref, o_ref,
                 *, n_prompt_chunks, chunks_per_seq, n_kv_heads):
    i = pl.program_id(0)
    is_sample = i >= n_prompt_chunks
    c = lax.rem(i, chunks_per_seq)
    first_valid = jnp.where(is_sample, 0, jnp.maximum(ATT_PREV - c, 0) * CHUNK)
    kvw = n_kv_heads * ATT_HD
    n_keys = (ATT_PREV + 1) * CHUNK
    n_heads = n_kv_heads * ATT_GROUP
    slopes = _alibi_slopes(n_heads)
    pairs = ATT_GROUP // 2
    qw = ATT_GROUP * ATT_HD
    lanes = 2 * ATT_HD

    kvc = kvc_ref[...]
    kv1 = kv1_ref[...]
    kv2 = kv2_ref[...]
    k_prev = jnp.where(is_sample, ck_ref[0], jnp.concatenate([kv2[:, :kvw], kv1[:, :kvw]], axis=0))
    v_prev = jnp.where(is_sample, cv_ref[0], jnp.concatenate([kv2[:, kvw:], kv1[:, kvw:]], axis=0))
    k_all = jnp.concatenate([k_prev, kvc[:, :kvw]], axis=0)
    v_all = jnp.concatenate([v_prev, kvc[:, kvw:]], axis=0)

    low = lax.broadcasted_iota(jnp.int32, (n_keys, lanes), 1) < ATT_HD
    qi = lax.broadcasted_iota(jnp.int32, (CHUNK, n_keys), 0)
    kj = lax.broadcasted_iota(jnp.int32, (CHUNK, n_keys), 1)
    dist = jnp.abs(ATT_PREV * CHUNK + qi - kj).astype(F32)
    valid = kj >= first_valid

    def split_halves(t, odd):
        rolled = pltpu.roll(t, ATT_HD, 1)
        in_low, in_high = (rolled, t) if odd else (t, rolled)
        zero = jnp.zeros_like(t)
        return jnp.where(low, in_low, zero).astype(BF16), jnp.where(low, zero, in_high).astype(BF16)

    def kv_head_operands(kh):
        col = (kh // 2) * lanes
        ks = split_halves(k_all[:, col:col + lanes], kh % 2)
        vs = split_halves(v_all[:, col:col + lanes], kh % 2)
        qs = jnp.concatenate([q_ref[:, kh * qw + p * lanes: kh * qw + (p + 1) * lanes] for p in range(pairs)],
                             axis=0)
        return qs, ks, vs

    bound = scal_ref[n_heads] * (ATT_HD ** 0.5) * jnp.max(jnp.abs(k_all))
    small = bound <= ATT_SAFE_EXP

    @pl.when(small)
    def _():
        ones = jnp.ones((n_keys, lanes), BF16)
        units = [(kh, parity) for kh in range(n_kv_heads) for parity in range(2)]
        ops = [kv_head_operands(kh) for kh in range(n_kv_heads)]
        scores = [lax.dot_general(ops[kh][0], ops[kh][1][parity], NT_DIMS, preferred_element_type=F32)
                  for kh, parity in units]
        probs = []
        for (kh, parity), s_all in zip(units, scores):
            rows = []
            for p in range(pairs):
                head = kh * ATT_GROUP + 2 * p + parity
                s = s_all[p * CHUNK:(p + 1) * CHUNK] + dist * (-slopes[head])
                rows.append(jnp.exp(jnp.where(valid, s, -jnp.inf)).astype(BF16))
            probs.append(jnp.concatenate(rows, axis=0))
        both = [_dot(pr, jnp.concatenate([ops[kh][2][parity], ones], axis=1))
                for (kh, parity), pr in zip(units, probs)]
        for kh in range(n_kv_heads):
            for p in range(pairs):
                rows = slice(p * CHUNK, (p + 1) * CHUNK)
                out = None
                for parity in range(2):
                    head = kh * ATT_GROUP + 2 * p + parity
                    b = both[2 * kh + parity]
                    den = b[rows, lanes:] + jnp.exp(jnp.full((1, lanes), scal_ref[head], F32))
                    part = b[rows, :lanes] / den
                    out = part if out is None else out + part
                o_ref[:, kh * qw + p * lanes: kh * qw + (p + 1) * lanes] = out.astype(BF16)

    @pl.when(jnp.logical_not(small))
    def _():
        for kh in range(n_kv_heads):
            qs, ks, vs = kv_head_operands(kh)
            acc = None
            for parity in range(2):
                s_all = lax.dot_general(qs, ks[parity], NT_DIMS, preferred_element_type=F32)
                probs, inv = [], []
                for p in range(pairs):
                    head = kh * ATT_GROUP + 2 * p + parity
                    s = s_all[p * CHUNK:(p + 1) * CHUNK] + dist * (-slopes[head])
                    s = jnp.where(valid, s, -jnp.inf)
                    sink = scal_ref[head]
                    m = jnp.maximum(jnp.max(s, axis=-1, keepdims=True), sink)
                    e = jnp.exp(s - m)
                    den = jnp.sum(e, axis=-1, keepdims=True) + jnp.exp(sink - m)
                    probs.append(e.astype(BF16))
                    inv.append(1.0 / den)
                part = _dot(jnp.concatenate(probs, axis=0), vs[parity]) * jnp.concatenate(inv, axis=0)
                acc = part if acc is None else acc + part
            for p in range(pairs):
                o_ref[:, kh * qw + p * lanes: kh * qw + (p + 1) * lanes] = (
                    acc[p * CHUNK:(p + 1) * CHUNK].astype(BF16))


def _attention(q, kv, cache_k, cache_v, scalars, n_prompt_seq, chunks_per_seq):
    T, d_model = q.shape
    n_chunks = T // CHUNK
    ncp = n_prompt_seq * chunks_per_seq
    kvw2 = kv.shape[1]
    n_kv_heads = kvw2 // (2 * ATT_HD)
    n_cache = cache_k.shape[1]
    assert n_cache == ATT_PREV * CHUNK

    def prev_map(back):
        def index(i):
            ok = jnp.logical_and(i < ncp, lax.rem(i, chunks_per_seq) >= back)
            return (jnp.where(ok, i - back, i), 0)
        return index

    kv_blk = (CHUNK, kvw2)
    cache_spec = pl.BlockSpec((1, n_cache, kvw2 // 2), lambda i: (jnp.maximum(i - ncp, 0), 0, 0))
    kern = functools.partial(_attn_kernel, n_prompt_chunks=ncp, chunks_per_seq=chunks_per_seq,
                             n_kv_heads=n_kv_heads)
    return pl.pallas_call(
        kern,
        grid=(n_chunks,),
        in_specs=[pl.BlockSpec(memory_space=pltpu.SMEM),
                  pl.BlockSpec((CHUNK, d_model), lambda i: (i, 0)),
                  pl.BlockSpec(kv_blk, lambda i: (i, 0)),
                  pl.BlockSpec(kv_blk, prev_map(1)),
                  pl.BlockSpec(kv_blk, prev_map(2)),
                  cache_spec, cache_spec],
        out_specs=pl.BlockSpec((CHUNK, d_model), lambda i: (i, 0)),
        out_shape=jax.ShapeDtypeStruct((T, d_model), BF16),
        compiler_params=_params("parallel"),
        name="swa_attention",
    )(scalars, q, kv, kv, kv, cache_k, cache_v)


def kernel(x_prompt, x_sample, state_hgrn, cache_k, cache_v, p_prompt, p_sample, norm_mix, norm_ffn,
           norm_ple, a_w_in, a_lb_logits, a_g_norm, a_w_o, b_w_qkv, b_q_norm, b_k_norm, b_sinks, b_w_o,
           f_w_gu, f_w_down, ple_w_proj, ple_w_gate):
    B, L, D = x_prompt.shape
    Bs, Ls, _ = x_sample.shape
    assert L % CHUNK == 0 and Ls == CHUNK
    tp, ts = B * L, Bs * Ls
    cps = L // CHUNK
    depth = norm_mix.shape[0]
    assert depth == 2

    xp, xs = x_prompt.reshape(tp, D), x_sample.reshape(ts, D)
    pp, ps = p_prompt.reshape(depth, tp, -1), p_sample.reshape(depth, ts, -1)
    vec = lambda w: w.reshape(1, -1).astype(F32)

    w_in = a_w_in[0].astype(BF16).reshape(D, 4, D // HGRN_TILE, HGRN_TILE).transpose(1, 2, 0, 3)
    (q, k, v, lf, gs), (w_gu0, w_down0, w_o0, w_gate0, w_proj0) = _hgrn_in(
        xp, xs, vec(norm_mix[0]), w_in, a_lb_logits.astype(F32),
        side=[(f_w_gu, 0), (f_w_down, 0), (a_w_o, 0), (ple_w_gate, 0), (ple_w_proj, 0)])

    o, state_p, state_s = _gla(q, k, v, lf, state_hgrn[0].astype(F32), B, cps)
    h1 = _out_proj((o, gs, vec(a_g_norm[0])), w_o0, (xp, xs), gated=True)
    h2, (w_gu1, w_down1, w_qkv, w_o1, w_gate1, w_proj1) = _ffn(
        h1, vec(norm_ffn[0]), w_gu0, w_down0,
        side=[(f_w_gu, 1), (f_w_down, 1), (b_w_qkv, 0), (b_w_o, 0), (ple_w_gate, 1), (ple_w_proj, 1)])
    h = _ple(h2, vec(norm_ple[0]), pp, ps, 0, w_gate0, w_proj0, split_output=False)

    qn, kvn = _qkv(h, vec(norm_mix[1]), w_qkv, b_q_norm[0], b_k_norm[0])
    n_kv = cache_k.shape[3]
    kvw = n_kv * ATT_HD
    scalars = jnp.concatenate([b_sinks[0].astype(F32), jnp.max(jnp.abs(b_q_norm[0])).reshape(1).astype(F32)])
    att = _attention(qn, kvn, cache_k[0].reshape(Bs, -1, kvw).astype(F32),
                     cache_v[0].reshape(Bs, -1, kvw).astype(F32), scalars, B, cps)
    h1 = _out_proj((att,), w_o1, (h,), gated=False)
    h2, _ = _ffn(h1, vec(norm_ffn[1]), w_gu1, w_down1)
    yp, ys = _ple(h2, vec(norm_ple[1]), pp, ps, 1, w_gate1, w_proj1, split_output=True)

    keep = min(ATT_PREV * CHUNK, L)
    kv_p = kvn[:tp].reshape(B, L, 2 * kvw)[:, L - keep:]
    kv_s = kvn[tp:].reshape(Bs, Ls, 2 * kvw)
    heads = lambda t: t.reshape(*t.shape[:2], n_kv, ATT_HD)[None]
    return (yp.reshape(B, L, D), ys.reshape(Bs, Ls, D), state_p[None], state_s[None],
            heads(kv_p[..., :kvw]), heads(kv_p[..., kvw:]), heads(kv_s[..., :kvw]), heads(kv_s[..., kvw:]))
```

```python
import functools

import numpy as np
import jax
import jax.numpy as jnp
from jax import lax
from jax.experimental import pallas as pl
from jax.experimental.pallas import tpu as pltpu

F32 = jnp.float32
BF16 = jnp.bfloat16

RMS_EPS = 1e-6
CHUNK = 64
HEAD_DK = 128
ATT_HD = 64
ATT_GROUP = 8
ATT_PREV = 2
GLA_LEVELS = (64, 32, 16, 8, 4, 2)
GLA_SAFE_EXP = 60.0
ATT_SAFE_EXP = 70.0

NT_DIMS = (((1,), (1,)), ((), ()))
TN_DIMS = (((0,), (0,)), ((), ()))

VMEM_LIMIT_BYTES = 60 * 1024 * 1024


def _params(*sem):
    return pltpu.CompilerParams(dimension_semantics=sem, vmem_limit_bytes=VMEM_LIMIT_BYTES)


def _tile(n, pref):
    t = pref
    while t > 8 and n % t:
        t //= 2
    assert n % t == 0, (n, pref)
    return t


def _split_rows(tm, n_p, width, sample_buffers=None):
    mode = {} if sample_buffers is None else dict(pipeline_mode=pl.Buffered(sample_buffers))
    return (pl.BlockSpec((tm, width), lambda i, *_: (jnp.minimum(i, n_p - 1), 0)),
            pl.BlockSpec((tm, width), lambda i, *_: (jnp.maximum(i - n_p, 0), 0), **mode))


def _resident(shape):
    return pl.BlockSpec(shape, lambda *_: (0,) * len(shape), pipeline_mode=pl.Buffered(1))


def _row_chunks(rows, size=256):
    size = min(size, rows)
    assert rows % size == 0
    return [slice(r, r + size) for r in range(0, rows, size)]


def _side_cast_specs(side, n_steps, step_of):
    in_specs, out_specs, out_shape = [], [], []
    for arr, idx in side:
        _, rows, cols = arr.shape
        rb = 16
        while rows // rb > n_steps:
            rb *= 2
        assert rows % rb == 0
        last = rows // rb - 1
        in_specs.append(pl.BlockSpec(
            (1, rb, cols), lambda *g, idx=idx, last=last: (idx, jnp.minimum(step_of(*g), last), 0)))
        out_specs.append(pl.BlockSpec((rb, cols), lambda *g, last=last: (jnp.minimum(step_of(*g), last), 0)))
        out_shape.append(jax.ShapeDtypeStruct((rows, cols), BF16))
    return in_specs, out_specs, out_shape


def _side_cast(in_refs, out_refs):
    for src, dst in zip(in_refs, out_refs):
        dst[...] = src[0].astype(BF16)


def _sigmoid(x):
    return 1.0 / (1.0 + jnp.exp(-x))


def _rms_rows(x, w):
    ms = jnp.mean(x * x, axis=-1, keepdims=True)
    return x * lax.rsqrt(ms + RMS_EPS) * w


def _dot(a, b):
    return jnp.dot(a, b, preferred_element_type=F32)


def _hgrn_in_kernel(xp_ref, xs_ref, nw_ref, wq_ref, wf_ref, wi_ref, wg_ref, lbl_ref, *refs, n_p, n_side):
    side_in, refs = refs[:n_side], refs[n_side:]
    q_ref, k_ref, v_ref, lf_ref, gs_ref = refs[:5]
    side_out, (xn_ref,) = refs[5:5 + n_side], refs[5 + n_side:]
    first = pl.program_id(1) == 0
    is_prompt = pl.program_id(0) < n_p

    @pl.when(jnp.logical_and(first, is_prompt))
    def _():
        xn_ref[...] = _rms_rows(xp_ref[...], nw_ref[...]).astype(BF16)

    @pl.when(jnp.logical_and(first, jnp.logical_not(is_prompt)))
    def _():
        xn_ref[...] = _rms_rows(xs_ref[...], nw_ref[...]).astype(BF16)

    xn = xn_ref[...]
    q = _dot(xn, wq_ref[...])
    q_ref[...] = (q * _sigmoid(q)).astype(BF16)
    logits = lbl_ref[...]
    ex = jnp.exp(logits - jnp.max(logits, axis=0, keepdims=True))
    lb = ex[0:1] / jnp.sum(ex, axis=0, keepdims=True)
    f = _dot(xn, wf_ref[...])
    forget = lb + (1.0 - lb) * _sigmoid(f)
    k_ref[...] = (1.0 - forget).astype(BF16)
    lf_ref[...] = jnp.log(forget)
    v_ref[...] = _dot(xn, wi_ref[...]).astype(BF16)
    g = _dot(xn, wg_ref[...])
    gs_ref[...] = (g * _sigmoid(g)).astype(BF16)
    _side_cast(side_in, side_out)


def _hgrn_in(xp, xs, norm_w, w_in, lb_logits, side):
    (tp, D), ts = xp.shape, xs.shape[0]
    T = tp + ts
    tm = min(_tile(tp, 1024), _tile(ts, 1024))
    tn = 256
    nb = D // tn
    out = pl.BlockSpec((tm, tn), lambda i, j: (i, j))

    def wspec(g):
        return pl.BlockSpec((D, tn), lambda i, j, g=g: (0, j + g * nb))

    side_in, side_out, side_shape = _side_cast_specs(side, (T // tm) * nb, lambda i, j: i * nb + j)
    res = pl.pallas_call(
        functools.partial(_hgrn_in_kernel, n_p=tp // tm, n_side=len(side)),
        grid=(T // tm, nb),
        in_specs=[*_split_rows(tm, tp // tm, D, sample_buffers=1), pl.BlockSpec((1, D), lambda i, j: (0, 0)),
                  wspec(0), wspec(1), wspec(2), wspec(3),
                  pl.BlockSpec((lb_logits.shape[0], tn), lambda i, j: (0, j)), *side_in],
        out_specs=[out] * 5 + side_out,
        out_shape=[jax.ShapeDtypeStruct((T, D), BF16)] * 2
        + [jax.ShapeDtypeStruct((T, D), BF16), jax.ShapeDtypeStruct((T, D), F32),
           jax.ShapeDtypeStruct((T, D), BF16)] + side_shape,
        scratch_shapes=[pltpu.VMEM((tm, D), BF16)],
        compiler_params=_params("arbitrary", "arbitrary"),
        name="hgrn_in",
    )(xp, xs, norm_w, w_in, w_in, w_in, w_in, lb_logits, *[a for a, _ in side])
    return res[:5], res[5:]


def _gla_constants():
    c = CHUNK
    t = np.arange(c)[:, None]
    s = np.arange(c)[None, :]
    zs = [s <= t, s > t]
    level = np.full((c, c), -1, np.int32)
    for li, p in enumerate(GLA_LEVELS, start=1):
        half = p // 2
        mid = (t // p) * p + half - 1
        upper = (t % p) >= half
        zs.append(np.where(upper, (s > mid) & (s <= t), (s > t) & (s <= mid)))
        level[(t // p == s // p) & upper & ((s % p) < half)] = li
    np.fill_diagonal(level, 0)
    z = np.concatenate(zs, axis=0).astype(np.float32)
    zf = np.concatenate([zs[0], zs[0]], axis=0).astype(np.float32)
    zf[c:] *= -1.0
    three = lambda m: np.concatenate([m, m, m], axis=1)
    return three(z), three(zf), level


def _gla_kernel(q_ref, k_ref, v_ref, lf_ref, lfn_ref, s0_ref, z_ref, zf_ref, lvl_ref, o_ref, sp_ref, ss_ref,
                st_ref, e_ref, dec_ref, *, n_prompt_chunks, chunks_per_seq, n_heads):
    i = pl.program_id(0)
    is_prompt = i < n_prompt_chunks
    c = jnp.where(is_prompt, lax.rem(i, chunks_per_seq), 0)

    @pl.when(jnp.logical_and(is_prompt, c == 0))
    def _():
        st_ref[...] = jnp.zeros_like(st_ref)

    @pl.when(jnp.logical_not(is_prompt))
    def _():
        for h in range(n_heads):
            st_ref[h] = s0_ref[0, h].T

    def exponents(z):
        g = lf_ref[...]
        g_hi = g.astype(BF16)
        r1 = g - g_hi.astype(F32)
        g_mid = r1.astype(BF16)
        g_lo = (r1 - g_mid.astype(F32)).astype(BF16)
        return _dot(z, jnp.concatenate([g_hi, g_mid, g_lo], axis=0))

    def strongest_decay(ref):
        return jnp.min(jnp.sum(ref[...], axis=0, keepdims=True))

    lvl = lvl_ref[...]

    @pl.when(i == 0)
    def _():
        dec_ref[0] = strongest_decay(lf_ref)

    mild = dec_ref[0] >= -GLA_SAFE_EXP

    @pl.when(mild)
    def _():
        e_ref[0:2 * CHUNK, :] = jnp.exp(exponents(zf_ref[...]))
        sls = [slice(h * HEAD_DK, (h + 1) * HEAD_DK) for h in range(n_heads)]
        q_hat = [(q_ref[:, sl].astype(F32) * e_ref[0:CHUNK, sl]).astype(BF16) for sl in sls]
        k_bar = [(k_ref[:, sl].astype(F32) * e_ref[CHUNK:2 * CHUNK, sl]).astype(BF16) for sl in sls]
        a = [lax.dot_general(q_hat[h], k_bar[h], NT_DIMS, preferred_element_type=F32) for h in range(n_heads)]
        o = [lax.dot_general(q_hat[h], st_ref[h].astype(BF16), NT_DIMS, preferred_element_type=F32)
             for h in range(n_heads)]
        for h, sl in enumerate(sls):
            o_ref[:, sl] = (o[h] + _dot(jnp.where(lvl >= 0, a[h], 0.0).astype(BF16), v_ref[:, sl])).astype(BF16)
        for h, sl in enumerate(sls):
            decay_all = e_ref[CHUNK - 1:CHUNK, sl]
            k_hat = (k_ref[:, sl].astype(F32) * (e_ref[CHUNK:2 * CHUNK, sl] * decay_all)).astype(BF16)
            st_ref[h] = st_ref[h] * decay_all + lax.dot_general(v_ref[:, sl], k_hat, TN_DIMS,
                                                                 preferred_element_type=F32)
        dec_ref[0] = strongest_decay(lfn_ref)

    @pl.when(jnp.logical_not(mild))
    def _():
        e_ref[...] = jnp.exp(exponents(z_ref[...]))
        row = lax.broadcasted_iota(jnp.int32, (CHUNK, HEAD_DK), 0)
        for h in range(n_heads):
            sl = slice(h * HEAD_DK, (h + 1) * HEAD_DK)
            qb = q_ref[:, sl]
            kb = k_ref[:, sl]
            vb = v_ref[:, sl]
            qf = qb.astype(F32)
            kf = kb.astype(F32)
            st = st_ref[h]
            q_hat = (qf * e_ref[0:CHUNK, sl]).astype(BF16)
            k_hat = (kf * e_ref[CHUNK:2 * CHUNK, sl]).astype(BF16)
            o = lax.dot_general(q_hat, st.astype(BF16), NT_DIMS, preferred_element_type=F32)
            a = lax.dot_general(qb, kb, NT_DIMS, preferred_element_type=F32)
            a = jnp.where(lvl == 0, a, 0.0)
            for li, p in enumerate(GLA_LEVELS, start=1):
                on_q_side = (row & (p // 2)) != 0
                x = (jnp.where(on_q_side, qf, kf)
                     * e_ref[(li + 1) * CHUNK:(li + 2) * CHUNK, sl]).astype(BF16)
                a = jnp.where(lvl == li, lax.dot_general(x, x, NT_DIMS, preferred_element_type=F32), a)
            o_ref[:, sl] = (o + _dot(a.astype(BF16), vb)).astype(BF16)
            decay_all = e_ref[CHUNK - 1:CHUNK, sl]
            st_ref[h] = st * decay_all + lax.dot_general(vb, k_hat, TN_DIMS, preferred_element_type=F32)
        dec_ref[0] = strongest_decay(lfn_ref)

    @pl.when(jnp.logical_and(is_prompt, c == chunks_per_seq - 1))
    def _():
        for h in range(n_heads):
            sp_ref[0, h] = st_ref[h].T

    @pl.when(jnp.logical_not(is_prompt))
    def _():
        for h in range(n_heads):
            ss_ref[0, h] = st_ref[h].T


def _gla(q, k, v, lf, s0, n_prompt_seq, chunks_per_seq):
    T, D = q.shape
    n_heads = D // HEAD_DK
    n_chunks = T // CHUNK
    ncp = n_prompt_seq * chunks_per_seq
    zcat, zfast, level = _gla_constants()
    blk = pl.BlockSpec((CHUNK, D), lambda i: (i, 0))
    st_blk = (1, n_heads, HEAD_DK, HEAD_DK)

    def prompt_seq(i):
        return (jnp.minimum(i // chunks_per_seq, n_prompt_seq - 1), 0, 0, 0)

    kern = functools.partial(_gla_kernel, n_prompt_chunks=ncp, chunks_per_seq=chunks_per_seq,
                             n_heads=n_heads)
    return pl.pallas_call(
        kern,
        grid=(n_chunks,),
        in_specs=[blk, blk, blk, blk,
                  pl.BlockSpec((CHUNK, D), lambda i: (jnp.minimum(i + 1, n_chunks - 1), 0)),
                  pl.BlockSpec(st_blk, lambda i: (jnp.maximum(i - ncp, 0), 0, 0, 0)),
                  pl.BlockSpec(zcat.shape, lambda i: (0, 0)),
                  pl.BlockSpec(zfast.shape, lambda i: (0, 0)),
                  pl.BlockSpec(level.shape, lambda i: (0, 0))],
        out_specs=[blk, pl.BlockSpec(st_blk, prompt_seq),
                   pl.BlockSpec(st_blk, lambda i: (jnp.maximum(i - ncp, 0), 0, 0, 0))],
        out_shape=[jax.ShapeDtypeStruct((T, D), BF16),
                   jax.ShapeDtypeStruct((n_prompt_seq, n_heads, HEAD_DK, HEAD_DK), F32),
                   jax.ShapeDtypeStruct((n_chunks - ncp, n_heads, HEAD_DK, HEAD_DK), F32)],
        scratch_shapes=[pltpu.VMEM((n_heads, HEAD_DK, HEAD_DK), F32),
                        pltpu.VMEM((zcat.shape[0], D), F32),
                        pltpu.SMEM((1,), F32)],
        compiler_params=_params("arbitrary"),
        name="gla_scan",
    )(q, k, v, lf, lf, s0, jnp.asarray(zcat, BF16), jnp.asarray(zfast, BF16), jnp.asarray(level))


def _out_proj_kernel(*refs, gated, n_p):
    if gated:
        o_ref, gs_ref, gn_ref, w_ref, *h_refs, h1_ref = refs
    else:
        a_ref, w_ref, *h_refs, h1_ref = refs
    for r in _row_chunks(h1_ref.shape[0]):
        if gated:
            a = (_rms_rows(o_ref[r, :].astype(F32), gn_ref[...]) * gs_ref[r, :].astype(F32)).astype(BF16)
        else:
            a = a_ref[r, :]
        if len(h_refs) == 2:
            h = jnp.where(pl.program_id(0) < n_p, h_refs[0][r, :], h_refs[1][r, :])
        else:
            h = h_refs[0][r, :]
        h1_ref[r, :] = h + _dot(a, w_ref[...])


def _out_proj(a_inputs, w, hs, gated):
    T, D = a_inputs[0].shape
    tm = min(_tile(h.shape[0], 512) for h in hs)
    row = pl.BlockSpec((tm, D), lambda i: (i, 0))
    vec = pl.BlockSpec((1, D), lambda i: (0, 0))
    wspec = _resident((D, D))
    n_p = hs[0].shape[0] // tm
    h_specs = [row] if len(hs) == 1 else list(_split_rows(tm, n_p, D))
    if gated:
        in_specs = [row, row, vec, wspec, *h_specs]
    else:
        in_specs = [row, wspec, *h_specs]
    return pl.pallas_call(
        functools.partial(_out_proj_kernel, gated=gated, n_p=n_p),
        grid=(T // tm,),
        in_specs=in_specs,
        out_specs=row,
        out_shape=jax.ShapeDtypeStruct((T, D), F32),
        compiler_params=_params("parallel"),
        name="out_proj",
    )(*a_inputs, w, *hs)


def _ffn_kernel(h_ref, nf_ref, wg_ref, wu_ref, wd_ref, *refs, n_side):
    side_in, h2_ref = refs[:n_side], refs[n_side]
    side_out, (xn_ref,) = refs[n_side + 1:2 * n_side + 1], refs[2 * n_side + 1:]

    @pl.when(pl.program_id(1) == 0)
    def _():
        h = h_ref[...]
        xn_ref[...] = _rms_rows(h, nf_ref[...]).astype(BF16)
        h2_ref[...] = h

    xn = xn_ref[...]
    gate = _dot(xn, wg_ref[...])
    up = _dot(xn, wu_ref[...])
    h2_ref[...] += _dot((gate * _sigmoid(gate) * up).astype(BF16), wd_ref[...])
    _side_cast(side_in, side_out)


def _ffn(h, norm_ffn, w_gu, w_down, side=()):
    T, D = h.shape
    ff = w_down.shape[0]
    tm = _tile(T, 1024)
    tf = 512
    nf = ff // tf
    row = pl.BlockSpec((tm, D), lambda i, j: (i, 0))
    side_in, side_out, side_shape = _side_cast_specs(side, (T // tm) * nf, lambda i, j: i * nf + j)
    res = pl.pallas_call(
        functools.partial(_ffn_kernel, n_side=len(side)),
        grid=(T // tm, nf),
        in_specs=[row, pl.BlockSpec((1, D), lambda i, j: (0, 0)),
                  pl.BlockSpec((D, tf), lambda i, j: (0, j)),
                  pl.BlockSpec((D, tf), lambda i, j: (0, j + nf)),
                  pl.BlockSpec((tf, D), lambda i, j: (j, 0)), *side_in],
        out_specs=[row] + side_out,
        out_shape=[jax.ShapeDtypeStruct((T, D), F32)] + side_shape,
        scratch_shapes=[pltpu.VMEM((tm, D), BF16)],
        compiler_params=_params("arbitrary", "arbitrary"),
        name="ffn",
    )(h, norm_ffn, w_gu, w_gu, w_down, *[a for a, _ in side])
    return res[0], res[1:]


def _ple_kernel(h_ref, np_ref, *refs, n_p):
    *p_refs, wg_ref, wp_ref, out_ref = refs
    for r in _row_chunks(h_ref.shape[0]):
        h = h_ref[r, :]
        gate = _sigmoid(_dot(_rms_rows(h, np_ref[...]).astype(BF16), wg_ref[...]))
        if len(p_refs) == 2:
            p = jnp.where(pl.program_id(0) < n_p, p_refs[0][0, r, :], p_refs[1][0, r, :])
        else:
            p = p_refs[0][0, r, :]
        out_ref[r, :] = h + _dot(p.astype(BF16), wp_ref[...]) * gate


def _ple_call(h, norm_ple, ps, layer, w_gate, w_proj, tm, first_block, n_blocks, n_p):
    D = h.shape[1]
    pd = ps[0].shape[2]
    if len(ps) == 2:
        p_specs = [pl.BlockSpec((1, tm, pd), lambda i: (layer, jnp.minimum(i, n_p - 1), 0)),
                   pl.BlockSpec((1, tm, pd), lambda i: (layer, jnp.maximum(i - n_p, 0), 0))]
    else:
        p_specs = [pl.BlockSpec((1, tm, pd), lambda i: (layer, i, 0))]
    return pl.pallas_call(
        functools.partial(_ple_kernel, n_p=n_p),
        grid=(n_blocks,),
        in_specs=[pl.BlockSpec((tm, D), lambda i: (i + first_block, 0)),
                  pl.BlockSpec((1, D), lambda i: (0, 0)),
                  *p_specs, _resident((D, D)), _resident((pd, D))],
        out_specs=pl.BlockSpec((tm, D), lambda i: (i, 0)),
        out_shape=jax.ShapeDtypeStruct((n_blocks * tm, D), F32),
        compiler_params=_params("parallel"),
        name="ple",
    )(h, norm_ple, *ps, w_gate, w_proj)


def _ple(h, norm_ple, p_prompt, p_sample, layer, w_gate, w_proj, split_output):
    tp, ts = p_prompt.shape[1], p_sample.shape[1]
    tm = min(_tile(tp, 512), _tile(ts, 512))
    n_p, n_s = tp // tm, ts // tm
    args = (h, norm_ple)
    if split_output:
        return (_ple_call(*args, (p_prompt,), layer, w_gate, w_proj, tm, 0, n_p, n_p),
                _ple_call(*args, (p_sample,), layer, w_gate, w_proj, tm, n_p, n_s, n_p))
    return _ple_call(*args, (p_prompt, p_sample), layer, w_gate, w_proj, tm, 0, n_p + n_s, n_p)


def _head_rms(x, seg, gain):
    ms = _dot((x * x).astype(BF16), seg) * (1.0 / ATT_HD)
    return x * lax.rsqrt(ms + RMS_EPS) * gain


def _kv_proj_kernel(h_ref, nm_ref, w_ref, seg_ref, gain_ref, xn_ref, kv_ref, kmax_ref, *, k_width):
    peaks = []
    for r in _row_chunks(h_ref.shape[0]):
        xn = _rms_rows(h_ref[r, :], nm_ref[...]).astype(BF16)
        xn_ref[r, :] = xn
        x = _dot(xn, w_ref[...])
        is_k = lax.broadcasted_iota(jnp.int32, x.shape, 1) < k_width
        kv = jnp.where(is_k, _head_rms(x, seg_ref[...], gain_ref[...]), x)
        kv_ref[r, :] = kv
        k_abs = jnp.where(is_k, jnp.abs(kv), 0.0)
        n_ch = x.shape[0] // CHUNK
        peak = jnp.max(k_abs.reshape(n_ch, CHUNK, x.shape[1]), axis=1)
        peaks.append(jnp.broadcast_to(jnp.max(peak, axis=1, keepdims=True), (n_ch, 128)))
    kmax_ref[0] = jnp.concatenate(peaks, axis=0)


def _q_proj_kernel(xn_ref, w_ref, seg_ref, gain_ref, q_ref):
    for r in _row_chunks(xn_ref.shape[0]):
        q_ref[r, :] = _head_rms(_dot(xn_ref[r, :], w_ref[...]), seg_ref[...], gain_ref[...]).astype(BF16)


def _qkv(h, norm_mix, w_qkv, q_gain, k_gain):
    T, D = h.shape
    n_out = w_qkv.shape[1]
    tn = ATT_GROUP * ATT_HD
    n_q_tiles = D // tn
    assert n_out == D + tn
    kv_w = (n_out - D) // 2
    tm = _tile(T, 1024)
    n_ch = tm // CHUNK
    seg = jnp.asarray(np.kron(np.eye(tn // ATT_HD, dtype=np.float32), np.ones((ATT_HD, ATT_HD), np.float32)),
                      BF16)
    q_row = (jnp.tile(q_gain.astype(F32), tn // ATT_HD) * (ATT_HD ** -0.5)).reshape(1, tn)
    kv_row = jnp.concatenate([jnp.tile(k_gain.astype(F32), kv_w // ATT_HD),
                              jnp.ones((tn - kv_w,), F32)]).reshape(1, tn)
    xn, kv, kmax = pl.pallas_call(
        functools.partial(_kv_proj_kernel, k_width=kv_w),
        grid=(T // tm,),
        in_specs=[pl.BlockSpec((tm, D), lambda i: (i, 0)),
                  pl.BlockSpec((1, D), lambda i: (0, 0)),
                  pl.BlockSpec((D, tn), lambda i: (0, n_q_tiles), pipeline_mode=pl.Buffered(1)),
                  _resident((tn, tn)), _resident((1, tn))],
        out_specs=[pl.BlockSpec((tm, D), lambda i: (i, 0)),
                   pl.BlockSpec((tm, tn), lambda i: (i, 0)),
                   pl.BlockSpec((1, n_ch, 128), lambda i: (i, 0, 0))],
        out_shape=[jax.ShapeDtypeStruct((T, D), BF16), jax.ShapeDtypeStruct((T, tn), F32),
                   jax.ShapeDtypeStruct((T // tm, n_ch, 128), F32)],
        compiler_params=_params("parallel"),
        name="kv_proj",
    )(h, norm_mix, w_qkv, seg, kv_row)
    q = pl.pallas_call(
        _q_proj_kernel,
        grid=(T // tm, n_q_tiles),
        in_specs=[pl.BlockSpec((tm, D), lambda i, j: (i, 0)),
                  pl.BlockSpec((D, tn), lambda i, j: (0, j)),
                  _resident((tn, tn)), _resident((1, tn))],
        out_specs=pl.BlockSpec((tm, tn), lambda i, j: (i, j)),
        out_shape=jax.ShapeDtypeStruct((T, D), BF16),
        compiler_params=_params("parallel", "parallel"),
        name="q_proj",
    )(xn, w_qkv, seg, q_row)
    return q, kv, kmax[:, :, 0].reshape(-1)


def _alibi_table(n_heads):
    slopes = np.float32(2.0) ** (np.float32(-8.0) * np.arange(1, n_heads + 1, dtype=np.float32) / np.float32(n_heads))
    qi = np.arange(CHUNK)[:, None]
    kj = np.arange((ATT_PREV + 1) * CHUNK)[None, :]
    dist = np.abs(ATT_PREV * CHUNK + qi - kj).astype(np.float32)
    return (-slopes[:, None, None] * dist).astype(np.float32)


def _attn_kernel(scal_ref, kmax_ref, cmax_ref, q_ref, kvc_ref, kv1_ref, kv2_ref, ck_ref, cv_ref, alibi_ref, o_ref,
                 *, n_prompt_chunks, chunks_per_seq, n_kv_heads):
    i = pl.program_id(0)
    is_sample = i >= n_prompt_chunks
    c = lax.rem(i, chunks_per_seq)
    n_missing = jnp.where(is_sample, 0, jnp.maximum(ATT_PREV - c, 0))
    first_valid = n_missing * CHUNK
    kvw = n_kv_heads * ATT_HD
    n_keys = (ATT_PREV + 1) * CHUNK
    n_heads = n_kv_heads * ATT_GROUP
    pairs = ATT_GROUP // 2
    qw = ATT_GROUP * ATT_HD
    lanes = 2 * ATT_HD

    kvc = kvc_ref[...]
    kv1 = kv1_ref[...]
    kv2 = kv2_ref[...]
    k_prev = jnp.where(is_sample, ck_ref[0], jnp.concatenate([kv2[:, :kvw], kv1[:, :kvw]], axis=0))
    v_prev = jnp.where(is_sample, cv_ref[0], jnp.concatenate([kv2[:, kvw:], kv1[:, kvw:]], axis=0))
    k_all = jnp.concatenate([k_prev, kvc[:, :kvw]], axis=0)
    key_ok = lax.broadcasted_iota(jnp.int32, (n_keys, kvw), 0) >= first_valid
    v_all = jnp.where(key_ok, jnp.concatenate([v_prev, kvc[:, kvw:]], axis=0), 0.0)
    ones = jnp.where(key_ok[:, :lanes], 1.0, 0.0).astype(BF16)

    low = lax.broadcasted_iota(jnp.int32, (n_keys, lanes), 1) < ATT_HD
    kj = lax.broadcasted_iota(jnp.int32, (CHUNK, n_keys), 1)
    valid = kj >= first_valid

    def split_halves(t, odd):
        rolled = pltpu.roll(t, ATT_HD, 1)
        in_low, in_high = (rolled, t) if odd else (t, rolled)
        zero = jnp.zeros_like(t)
        return jnp.where(low, in_low, zero).astype(BF16), jnp.where(low, zero, in_high).astype(BF16)

    def kv_head_operands(kh):
        col = (kh // 2) * lanes
        ks = split_halves(k_all[:, col:col + lanes], kh % 2)
        vs = split_halves(v_all[:, col:col + lanes], kh % 2)
        qs = jnp.concatenate([q_ref[:, kh * qw + p * lanes: kh * qw + (p + 1) * lanes] for p in range(pairs)],
                             axis=0)
        return qs, ks, vs

    k_prompt = jnp.maximum(kmax_ref[i], jnp.maximum(kmax_ref[i - jnp.minimum(1, ATT_PREV - n_missing)],
                                                    kmax_ref[i - (ATT_PREV - n_missing)]))
    k_sample = jnp.maximum(kmax_ref[i], cmax_ref[jnp.maximum(i - n_prompt_chunks, 0)])
    bound = scal_ref[n_heads] * (ATT_HD ** 0.5) * jnp.where(is_sample, k_sample, k_prompt)
    small = bound <= ATT_SAFE_EXP

    @pl.when(small)
    def _():
        units = [(kh, parity) for kh in range(n_kv_heads) for parity in range(2)]
        ops = [kv_head_operands(kh) for kh in range(n_kv_heads)]
        scores = [lax.dot_general(ops[kh][0], ops[kh][1][parity], NT_DIMS, preferred_element_type=F32)
                  for kh, parity in units]
        probs = []
        for (kh, parity), s_all in zip(units, scores):
            rows = []
            for p in range(pairs):
                head = kh * ATT_GROUP + 2 * p + parity
                rows.append(jnp.exp(s_all[p * CHUNK:(p + 1) * CHUNK] + alibi_ref[head]).astype(BF16))
            probs.append(jnp.concatenate(rows, axis=0))
        both = [_dot(pr, jnp.concatenate([ops[kh][2][parity], ones], axis=1))
                for (kh, parity), pr in zip(units, probs)]
        for kh in range(n_kv_heads):
            for p in range(pairs):
                rows = slice(p * CHUNK, (p + 1) * CHUNK)
                out = None
                for parity in range(2):
                    head = kh * ATT_GROUP + 2 * p + parity
                    b = both[2 * kh + parity]
                    den = b[rows, lanes:] + jnp.exp(jnp.full((1, lanes), scal_ref[head], F32))
                    part = b[rows, :lanes] / den
                    out = part if out is None else out + part
                o_ref[:, kh * qw + p * lanes: kh * qw + (p + 1) * lanes] = out.astype(BF16)

    @pl.when(jnp.logical_not(small))
    def _():
        for kh in range(n_kv_heads):
            qs, ks, vs = kv_head_operands(kh)
            acc = None
            for parity in range(2):
                s_all = lax.dot_general(qs, ks[parity], NT_DIMS, preferred_element_type=F32)
                probs, inv = [], []
                for p in range(pairs):
                    head = kh * ATT_GROUP + 2 * p + parity
                    s = jnp.where(valid, s_all[p * CHUNK:(p + 1) * CHUNK] + alibi_ref[head], -jnp.inf)
                    sink = scal_ref[head]
                    m = jnp.maximum(jnp.max(s, axis=-1, keepdims=True), sink)
                    e = jnp.exp(s - m)
                    den = jnp.sum(e, axis=-1, keepdims=True) + jnp.exp(sink - m)
                    probs.append(e.astype(BF16))
                    inv.append(1.0 / den)
                part = _dot(jnp.concatenate(probs, axis=0), vs[parity]) * jnp.concatenate(inv, axis=0)
                acc = part if acc is None else acc + part
            for p in range(pairs):
                o_ref[:, kh * qw + p * lanes: kh * qw + (p + 1) * lanes] = (
                    acc[p * CHUNK:(p + 1) * CHUNK].astype(BF16))


def _attention(q, kv, kmax, cache_k, cache_v, scalars, n_prompt_seq, chunks_per_seq):
    T, d_model = q.shape
    n_chunks = T // CHUNK
    ncp = n_prompt_seq * chunks_per_seq
    kvw2 = kv.shape[1]
    n_kv_heads = kvw2 // (2 * ATT_HD)
    n_cache = cache_k.shape[1]
    assert n_cache == ATT_PREV * CHUNK

    def prev_map(back):
        def index(i):
            ok = jnp.logical_and(i < ncp, lax.rem(i, chunks_per_seq) >= back)
            return (jnp.where(ok, i - back, i), 0)
        return index

    kv_blk = (CHUNK, kvw2)
    alibi = _alibi_table(n_kv_heads * ATT_GROUP)
    cmax = jnp.max(jnp.abs(cache_k), axis=(1, 2))
    smem = pl.BlockSpec(memory_space=pltpu.SMEM)
    cache_spec = pl.BlockSpec((1, n_cache, kvw2 // 2), lambda i: (jnp.maximum(i - ncp, 0), 0, 0))
    kern = functools.partial(_attn_kernel, n_prompt_chunks=ncp, chunks_per_seq=chunks_per_seq,
                             n_kv_heads=n_kv_heads)
    return pl.pallas_call(
        kern,
        grid=(n_chunks,),
        in_specs=[smem, smem, smem,
                  pl.BlockSpec((CHUNK, d_model), lambda i: (i, 0)),
                  pl.BlockSpec(kv_blk, lambda i: (i, 0)),
                  pl.BlockSpec(kv_blk, prev_map(1)),
                  pl.BlockSpec(kv_blk, prev_map(2)),
                  cache_spec, cache_spec, _resident(alibi.shape)],
        out_specs=pl.BlockSpec((CHUNK, d_model), lambda i: (i, 0)),
        out_shape=jax.ShapeDtypeStruct((T, d_model), BF16),
        compiler_params=_params("parallel"),
        name="swa_attention",
    )(scalars, kmax, cmax, q, kv, kv, kv, cache_k, cache_v, jnp.asarray(alibi))


def kernel(x_prompt, x_sample, state_hgrn, cache_k, cache_v, p_prompt, p_sample, norm_mix, norm_ffn,
           norm_ple, a_w_in, a_lb_logits, a_g_norm, a_w_o, b_w_qkv, b_q_norm, b_k_norm, b_sinks, b_w_o,
           f_w_gu, f_w_down, ple_w_proj, ple_w_gate):
    B, L, D = x_prompt.shape
    Bs, Ls, _ = x_sample.shape
    assert L % CHUNK == 0 and Ls == CHUNK
    tp, ts = B * L, Bs * Ls
    cps = L // CHUNK
    depth = norm_mix.shape[0]
    assert depth == 2

    xp, xs = x_prompt.reshape(tp, D), x_sample.reshape(ts, D)
    pp, ps = p_prompt.reshape(depth, tp, -1), p_sample.reshape(depth, ts, -1)
    vec = lambda w: w.reshape(1, -1).astype(F32)

    (q, k, v, lf, gs), (w_gu0, w_down0, w_o0, w_gate0, w_proj0) = _hgrn_in(
        xp, xs, vec(norm_mix[0]), a_w_in[0].astype(BF16), a_lb_logits.astype(F32),
        side=[(f_w_gu, 0), (f_w_down, 0), (a_w_o, 0), (ple_w_gate, 0), (ple_w_proj, 0)])

    o, state_p, state_s = _gla(q, k, v, lf, state_hgrn[0].astype(F32), B, cps)
    h1 = _out_proj((o, gs, vec(a_g_norm[0])), w_o0, (xp, xs), gated=True)
    h2, (w_gu1, w_down1, w_qkv, w_o1, w_gate1, w_proj1) = _ffn(
        h1, vec(norm_ffn[0]), w_gu0, w_down0,
        side=[(f_w_gu, 1), (f_w_down, 1), (b_w_qkv, 0), (b_w_o, 0), (ple_w_gate, 1), (ple_w_proj, 1)])
    h = _ple(h2, vec(norm_ple[0]), pp, ps, 0, w_gate0, w_proj0, split_output=False)

    qn, kvn, kmax = _qkv(h, vec(norm_mix[1]), w_qkv, b_q_norm[0], b_k_norm[0])
    n_kv = cache_k.shape[3]
    kvw = n_kv * ATT_HD
    scalars = jnp.concatenate([b_sinks[0].astype(F32), jnp.max(jnp.abs(b_q_norm[0])).reshape(1).astype(F32)])
    att = _attention(qn, kvn, kmax, cache_k[0].reshape(Bs, -1, kvw).astype(F32),
                     cache_v[0].reshape(Bs, -1, kvw).astype(F32), scalars, B, cps)
    h1 = _out_proj((att,), w_o1, (h,), gated=False)
    h2, _ = _ffn(h1, vec(norm_ffn[1]), w_gu1, w_down1)
    yp, ys = _ple(h2, vec(norm_ple[1]), pp, ps, 1, w_gate1, w_proj1, split_output=True)

    keep = min(ATT_PREV * CHUNK, L)
    kv_p = kvn[:tp].reshape(B, L, 2 * kvw)[:, L - keep:]
    kv_s = kvn[tp:].reshape(Bs, Ls, 2 * kvw)
    heads = lambda t: t.reshape(*t.shape[:2], n_kv, ATT_HD)[None]
    return (yp.reshape(B, L, D), ys.reshape(Bs, Ls, D), state_p[None], state_s[None],
            heads(kv_p[..., :kvw]), heads(kv_p[..., kvw:]), heads(kv_s[..., :kvw]), heads(kv_s[..., kvw:]))
```

```python
import functools

import numpy as np
import jax
import jax.numpy as jnp
from jax import lax
from jax.experimental import pallas as pl
from jax.experimental.pallas import tpu as pltpu

F32 = jnp.float32
BF16 = jnp.bfloat16

RMS_EPS = 1e-6
CHUNK = 64
PAIR = 2
HEAD_DK = 128
ATT_HD = 64
ATT_GROUP = 8
ATT_PREV = 2
GLA_LEVELS = (64, 32, 16, 8, 4, 2)
GLA_SAFE_EXP = 60.0
ATT_SAFE_EXP = 70.0

NT_DIMS = (((1,), (1,)), ((), ()))
TN_DIMS = (((0,), (0,)), ((), ()))

VMEM_LIMIT_BYTES = 60 * 1024 * 1024


def _params(*sem):
    return pltpu.CompilerParams(dimension_semantics=sem, vmem_limit_bytes=VMEM_LIMIT_BYTES)


def _tile(n, pref):
    t = pref
    while t > 8 and n % t:
        t //= 2
    assert n % t == 0, (n, pref)
    return t


def _split_rows(tm, n_p, width, sample_buffers=None):
    mode = {} if sample_buffers is None else dict(pipeline_mode=pl.Buffered(sample_buffers))
    return (pl.BlockSpec((tm, width), lambda i, *_: (jnp.minimum(i, n_p - 1), 0)),
            pl.BlockSpec((tm, width), lambda i, *_: (jnp.maximum(i - n_p, 0), 0), **mode))


def _resident(shape):
    return pl.BlockSpec(shape, lambda *_: (0,) * len(shape), pipeline_mode=pl.Buffered(1))


def _row_chunks(rows, size=256):
    size = min(size, rows)
    assert rows % size == 0
    return [slice(r, r + size) for r in range(0, rows, size)]


def _side_cast_specs(side, n_steps, step_of):
    in_specs, out_specs, out_shape = [], [], []
    for arr, idx in side:
        _, rows, cols = arr.shape
        rb = 16
        while rows // rb > n_steps:
            rb *= 2
        assert rows % rb == 0
        last = rows // rb - 1
        in_specs.append(pl.BlockSpec(
            (1, rb, cols), lambda *g, idx=idx, last=last: (idx, jnp.minimum(step_of(*g), last), 0)))
        out_specs.append(pl.BlockSpec((rb, cols), lambda *g, last=last: (jnp.minimum(step_of(*g), last), 0)))
        out_shape.append(jax.ShapeDtypeStruct((rows, cols), BF16))
    return in_specs, out_specs, out_shape


def _side_cast(in_refs, out_refs):
    for src, dst in zip(in_refs, out_refs):
        dst[...] = src[0].astype(BF16)


def _sigmoid(x):
    return 1.0 / (1.0 + jnp.exp(-x))


def _rms_rows(x, w):
    ms = jnp.mean(x * x, axis=-1, keepdims=True)
    return x * lax.rsqrt(ms + RMS_EPS) * w


def _dot(a, b):
    return jnp.dot(a, b, preferred_element_type=F32)


def _hgrn_in_kernel(xp_ref, xs_ref, nw_ref, wq_ref, wf_ref, wi_ref, wg_ref, lbl_ref, *refs, n_p, n_side):
    side_in, refs = refs[:n_side], refs[n_side:]
    q_ref, k_ref, v_ref, lf_ref, gs_ref = refs[:5]
    side_out, (xn_ref,) = refs[5:5 + n_side], refs[5 + n_side:]
    first = pl.program_id(1) == 0
    is_prompt = pl.program_id(0) < n_p

    @pl.when(jnp.logical_and(first, is_prompt))
    def _():
        xn_ref[...] = _rms_rows(xp_ref[...], nw_ref[...]).astype(BF16)

    @pl.when(jnp.logical_and(first, jnp.logical_not(is_prompt)))
    def _():
        xn_ref[...] = _rms_rows(xs_ref[...], nw_ref[...]).astype(BF16)

    xn = xn_ref[...]
    q = _dot(xn, wq_ref[...])
    q_ref[...] = (q * _sigmoid(q)).astype(BF16)
    logits = lbl_ref[...]
    ex = jnp.exp(logits - jnp.max(logits, axis=0, keepdims=True))
    lb = ex[0:1] / jnp.sum(ex, axis=0, keepdims=True)
    f = _dot(xn, wf_ref[...])
    forget = lb + (1.0 - lb) * _sigmoid(f)
    k_ref[...] = (1.0 - forget).astype(BF16)
    lf_ref[...] = jnp.log(forget)
    v_ref[...] = _dot(xn, wi_ref[...]).astype(BF16)
    g = _dot(xn, wg_ref[...])
    gs_ref[...] = (g * _sigmoid(g)).astype(BF16)
    _side_cast(side_in, side_out)


def _hgrn_in(xp, xs, norm_w, w_in, lb_logits, side):
    (tp, D), ts = xp.shape, xs.shape[0]
    T = tp + ts
    tm = min(_tile(tp, 1024), _tile(ts, 1024))
    tn = 256
    nb = D // tn
    out = pl.BlockSpec((tm, tn), lambda i, j: (i, j))

    def wspec(g):
        return pl.BlockSpec((D, tn), lambda i, j, g=g: (0, j + g * nb))

    side_in, side_out, side_shape = _side_cast_specs(side, (T // tm) * nb, lambda i, j: i * nb + j)
    res = pl.pallas_call(
        functools.partial(_hgrn_in_kernel, n_p=tp // tm, n_side=len(side)),
        grid=(T // tm, nb),
        in_specs=[*_split_rows(tm, tp // tm, D, sample_buffers=1), pl.BlockSpec((1, D), lambda i, j: (0, 0)),
                  wspec(0), wspec(1), wspec(2), wspec(3),
                  pl.BlockSpec((lb_logits.shape[0], tn), lambda i, j: (0, j)), *side_in],
        out_specs=[out] * 5 + side_out,
        out_shape=[jax.ShapeDtypeStruct((T, D), BF16)] * 2
        + [jax.ShapeDtypeStruct((T, D), BF16), jax.ShapeDtypeStruct((T, D), F32),
           jax.ShapeDtypeStruct((T, D), BF16)] + side_shape,
        scratch_shapes=[pltpu.VMEM((tm, D), BF16)],
        compiler_params=_params("arbitrary", "arbitrary"),
        name="hgrn_in",
    )(xp, xs, norm_w, w_in, w_in, w_in, w_in, lb_logits, *[a for a, _ in side])
    return res[:5], res[5:]


def _gla_constants():
    c = CHUNK
    t = np.arange(c)[:, None]
    s = np.arange(c)[None, :]
    zs = [s <= t, s > t]
    level = np.full((c, c), -1, np.int32)
    for li, p in enumerate(GLA_LEVELS, start=1):
        half = p // 2
        mid = (t // p) * p + half - 1
        upper = (t % p) >= half
        zs.append(np.where(upper, (s > mid) & (s <= t), (s > t) & (s <= mid)))
        level[(t // p == s // p) & upper & ((s % p) < half)] = li
    np.fill_diagonal(level, 0)
    z = np.concatenate(zs, axis=0).astype(np.float32)
    zf = np.concatenate([zs[0], zs[0]], axis=0).astype(np.float32)
    zf[c:] *= -1.0
    three = lambda m: np.concatenate([m, m, m], axis=1)
    return three(z), three(zf), level


def _gla_kernel(q_ref, k_ref, v_ref, lf_ref, lfn_ref, s0_ref, z_ref, zf_ref, lvl_ref, o_ref, sp_ref, ss_ref,
                st_ref, e_ref, dec_ref, *, n_prompt_steps, steps_per_seq, n_heads):
    m = pl.program_id(0)
    is_prompt = m < n_prompt_steps
    is_sample = jnp.logical_not(is_prompt)
    c = jnp.where(is_prompt, lax.rem(m, steps_per_seq), 0)
    chunk_rows = [slice(t * CHUNK, (t + 1) * CHUNK) for t in range(PAIR)]
    sls = [slice(h * HEAD_DK, (h + 1) * HEAD_DK) for h in range(n_heads)]
    lvl = lvl_ref[...]

    def exponents(z, rows):
        g = lf_ref[rows, :]
        g_hi = g.astype(BF16)
        r1 = g - g_hi.astype(F32)
        g_mid = r1.astype(BF16)
        g_lo = (r1 - g_mid.astype(F32)).astype(BF16)
        return _dot(z, jnp.concatenate([g_hi, g_mid, g_lo], axis=0))

    def note_next_decays(ref):
        for t, rows in enumerate(chunk_rows):
            dec_ref[t] = jnp.min(jnp.sum(ref[rows, :], axis=0, keepdims=True))

    def single_reference(rows):
        e_ref[0:2 * CHUNK, :] = jnp.exp(exponents(zf_ref[...], rows))
        q_hat = [(q_ref[rows, sl].astype(F32) * e_ref[0:CHUNK, sl]).astype(BF16) for sl in sls]
        k_bar = [(k_ref[rows, sl].astype(F32) * e_ref[CHUNK:2 * CHUNK, sl]).astype(BF16) for sl in sls]
        a = [lax.dot_general(q_hat[h], k_bar[h], NT_DIMS, preferred_element_type=F32) for h in range(n_heads)]
        o = [lax.dot_general(q_hat[h], st_ref[h].astype(BF16), NT_DIMS, preferred_element_type=F32)
             for h in range(n_heads)]
        for h, sl in enumerate(sls):
            o_ref[rows, sl] = (o[h] + _dot(jnp.where(lvl >= 0, a[h], 0.0).astype(BF16), v_ref[rows, sl])).astype(BF16)
        for h, sl in enumerate(sls):
            decay_all = e_ref[CHUNK - 1:CHUNK, sl]
            k_hat = (k_ref[rows, sl].astype(F32) * (e_ref[CHUNK:2 * CHUNK, sl] * decay_all)).astype(BF16)
            st_ref[h] = st_ref[h] * decay_all + lax.dot_general(v_ref[rows, sl], k_hat, TN_DIMS,
                                                                 preferred_element_type=F32)

    def binary_splitting(rows):
        e_ref[...] = jnp.exp(exponents(z_ref[...], rows))
        row = lax.broadcasted_iota(jnp.int32, (CHUNK, HEAD_DK), 0)
        for h, sl in enumerate(sls):
            qb = q_ref[rows, sl]
            kb = k_ref[rows, sl]
            vb = v_ref[rows, sl]
            qf = qb.astype(F32)
            kf = kb.astype(F32)
            st = st_ref[h]
            q_hat = (qf * e_ref[0:CHUNK, sl]).astype(BF16)
            k_hat = (kf * e_ref[CHUNK:2 * CHUNK, sl]).astype(BF16)
            o = lax.dot_general(q_hat, st.astype(BF16), NT_DIMS, preferred_element_type=F32)
            a = lax.dot_general(qb, kb, NT_DIMS, preferred_element_type=F32)
            a = jnp.where(lvl == 0, a, 0.0)
            for li, p in enumerate(GLA_LEVELS, start=1):
                on_q_side = (row & (p // 2)) != 0
                x = (jnp.where(on_q_side, qf, kf)
                     * e_ref[(li + 1) * CHUNK:(li + 2) * CHUNK, sl]).astype(BF16)
                a = jnp.where(lvl == li, lax.dot_general(x, x, NT_DIMS, preferred_element_type=F32), a)
            o_ref[rows, sl] = (o + _dot(a.astype(BF16), vb)).astype(BF16)
            decay_all = e_ref[CHUNK - 1:CHUNK, sl]
            st_ref[h] = st * decay_all + lax.dot_general(vb, k_hat, TN_DIMS, preferred_element_type=F32)

    @pl.when(m == 0)
    def _():
        note_next_decays(lf_ref)

    mild = [dec_ref[t] >= -GLA_SAFE_EXP for t in range(PAIR)]

    @pl.when(jnp.logical_and(is_prompt, c == 0))
    def _():
        st_ref[...] = jnp.zeros_like(st_ref)

    for t, rows in enumerate(chunk_rows):
        last = t == PAIR - 1

        @pl.when(is_sample)
        def _():
            for h in range(n_heads):
                st_ref[h] = s0_ref[t, h].T

        @pl.when(mild[t])
        def _():
            single_reference(rows)
            if last:
                note_next_decays(lfn_ref)

        @pl.when(jnp.logical_not(mild[t]))
        def _():
            binary_splitting(rows)
            if last:
                note_next_decays(lfn_ref)

        @pl.when(is_sample)
        def _():
            for h in range(n_heads):
                ss_ref[t, h] = st_ref[h].T

    @pl.when(jnp.logical_and(is_prompt, c == steps_per_seq - 1))
    def _():
        for h in range(n_heads):
            sp_ref[0, h] = st_ref[h].T


def _gla(q, k, v, lf, s0, n_prompt_seq, chunks_per_seq):
    T, D = q.shape
    n_heads = D // HEAD_DK
    rows = PAIR * CHUNK
    n_steps = T // rows
    assert chunks_per_seq % PAIR == 0 and s0.shape[0] % PAIR == 0
    steps_per_seq = chunks_per_seq // PAIR
    nps = n_prompt_seq * steps_per_seq
    zcat, zfast, level = _gla_constants()
    blk = pl.BlockSpec((rows, D), lambda i: (i, 0))
    h_blk = (n_heads, HEAD_DK, HEAD_DK)

    def prompt_seq(i):
        return (jnp.minimum(i // steps_per_seq, n_prompt_seq - 1), 0, 0, 0)

    def sample_pair(i):
        return (jnp.maximum(i - nps, 0), 0, 0, 0)

    kern = functools.partial(_gla_kernel, n_prompt_steps=nps, steps_per_seq=steps_per_seq, n_heads=n_heads)
    return pl.pallas_call(
        kern,
        grid=(n_steps,),
        in_specs=[blk, blk, blk, blk,
                  pl.BlockSpec((rows, D), lambda i: (jnp.minimum(i + 1, n_steps - 1), 0)),
                  pl.BlockSpec((PAIR, *h_blk), sample_pair),
                  pl.BlockSpec(zcat.shape, lambda i: (0, 0)),
                  pl.BlockSpec(zfast.shape, lambda i: (0, 0)),
                  pl.BlockSpec(level.shape, lambda i: (0, 0))],
        out_specs=[blk, pl.BlockSpec((1, *h_blk), prompt_seq), pl.BlockSpec((PAIR, *h_blk), sample_pair)],
        out_shape=[jax.ShapeDtypeStruct((T, D), BF16),
                   jax.ShapeDtypeStruct((n_prompt_seq, *h_blk), F32),
                   jax.ShapeDtypeStruct((s0.shape[0], *h_blk), F32)],
        scratch_shapes=[pltpu.VMEM(h_blk, F32),
                        pltpu.VMEM((zcat.shape[0], D), F32),
                        pltpu.SMEM((PAIR,), F32)],
        compiler_params=_params("arbitrary"),
        name="gla_scan",
    )(q, k, v, lf, lf, s0, jnp.asarray(zcat, BF16), jnp.asarray(zfast, BF16), jnp.asarray(level))


def _out_proj_kernel(*refs, gated, n_p):
    if gated:
        o_ref, gs_ref, gn_ref, w_ref, *h_refs, h1_ref = refs
    else:
        a_ref, w_ref, *h_refs, h1_ref = refs
    for r in _row_chunks(h1_ref.shape[0]):
        if gated:
            a = (_rms_rows(o_ref[r, :].astype(F32), gn_ref[...]) * gs_ref[r, :].astype(F32)).astype(BF16)
        else:
            a = a_ref[r, :]
        if len(h_refs) == 2:
            h = jnp.where(pl.program_id(0) < n_p, h_refs[0][r, :], h_refs[1][r, :])
        else:
            h = h_refs[0][r, :]
        h1_ref[r, :] = h + _dot(a, w_ref[...])


def _out_proj(a_inputs, w, hs, gated):
    T, D = a_inputs[0].shape
    tm = min(_tile(h.shape[0], 512) for h in hs)
    row = pl.BlockSpec((tm, D), lambda i: (i, 0))
    vec = pl.BlockSpec((1, D), lambda i: (0, 0))
    wspec = _resident((D, D))
    n_p = hs[0].shape[0] // tm
    h_specs = [row] if len(hs) == 1 else list(_split_rows(tm, n_p, D))
    if gated:
        in_specs = [row, row, vec, wspec, *h_specs]
    else:
        in_specs = [row, wspec, *h_specs]
    return pl.pallas_call(
        functools.partial(_out_proj_kernel, gated=gated, n_p=n_p),
        grid=(T // tm,),
        in_specs=in_specs,
        out_specs=row,
        out_shape=jax.ShapeDtypeStruct((T, D), F32),
        compiler_params=_params("parallel"),
        name="out_proj",
    )(*a_inputs, w, *hs)


def _ffn_kernel(h_ref, nf_ref, wg_ref, wu_ref, wd_ref, *refs, n_side):
    side_in, h2_ref = refs[:n_side], refs[n_side]
    side_out, (xn_ref,) = refs[n_side + 1:2 * n_side + 1], refs[2 * n_side + 1:]

    @pl.when(pl.program_id(1) == 0)
    def _():
        h = h_ref[...]
        xn_ref[...] = _rms_rows(h, nf_ref[...]).astype(BF16)
        h2_ref[...] = h

    xn = xn_ref[...]
    gate = _dot(xn, wg_ref[...])
    up = _dot(xn, wu_ref[...])
    h2_ref[...] += _dot((gate * _sigmoid(gate) * up).astype(BF16), wd_ref[...])
    _side_cast(side_in, side_out)


def _ffn(h, norm_ffn, w_gu, w_down, side=()):
    T, D = h.shape
    ff = w_down.shape[0]
    tm = _tile(T, 1024)
    tf = 512
    nf = ff // tf
    row = pl.BlockSpec((tm, D), lambda i, j: (i, 0))
    side_in, side_out, side_shape = _side_cast_specs(side, (T // tm) * nf, lambda i, j: i * nf + j)
    res = pl.pallas_call(
        functools.partial(_ffn_kernel, n_side=len(side)),
        grid=(T // tm, nf),
        in_specs=[row, pl.BlockSpec((1, D), lambda i, j: (0, 0)),
                  pl.BlockSpec((D, tf), lambda i, j: (0, j)),
                  pl.BlockSpec((D, tf), lambda i, j: (0, j + nf)),
                  pl.BlockSpec((tf, D), lambda i, j: (j, 0)), *side_in],
        out_specs=[row] + side_out,
        out_shape=[jax.ShapeDtypeStruct((T, D), F32)] + side_shape,
        scratch_shapes=[pltpu.VMEM((tm, D), BF16)],
        compiler_params=_params("arbitrary", "arbitrary"),
        name="ffn",
    )(h, norm_ffn, w_gu, w_gu, w_down, *[a for a, _ in side])
    return res[0], res[1:]


def _ple_kernel(h_ref, np_ref, *refs, n_p):
    *p_refs, wg_ref, wp_ref, out_ref = refs
    for r in _row_chunks(h_ref.shape[0]):
        h = h_ref[r, :]
        gate = _sigmoid(_dot(_rms_rows(h, np_ref[...]).astype(BF16), wg_ref[...]))
        if len(p_refs) == 2:
            p = jnp.where(pl.program_id(0) < n_p, p_refs[0][0, r, :], p_refs[1][0, r, :])
        else:
            p = p_refs[0][0, r, :]
        out_ref[r, :] = h + _dot(p.astype(BF16), wp_ref[...]) * gate


def _ple_call(h, norm_ple, ps, layer, w_gate, w_proj, tm, first_block, n_blocks, n_p):
    D = h.shape[1]
    pd = ps[0].shape[2]
    if len(ps) == 2:
        p_specs = [pl.BlockSpec((1, tm, pd), lambda i: (layer, jnp.minimum(i, n_p - 1), 0)),
                   pl.BlockSpec((1, tm, pd), lambda i: (layer, jnp.maximum(i - n_p, 0), 0))]
    else:
        p_specs = [pl.BlockSpec((1, tm, pd), lambda i: (layer, i, 0))]
    return pl.pallas_call(
        functools.partial(_ple_kernel, n_p=n_p),
        grid=(n_blocks,),
        in_specs=[pl.BlockSpec((tm, D), lambda i: (i + first_block, 0)),
                  pl.BlockSpec((1, D), lambda i: (0, 0)),
                  *p_specs, _resident((D, D)), _resident((pd, D))],
        out_specs=pl.BlockSpec((tm, D), lambda i: (i, 0)),
        out_shape=jax.ShapeDtypeStruct((n_blocks * tm, D), F32),
        compiler_params=_params("parallel"),
        name="ple",
    )(h, norm_ple, *ps, w_gate, w_proj)


def _ple(h, norm_ple, p_prompt, p_sample, layer, w_gate, w_proj, split_output):
    tp, ts = p_prompt.shape[1], p_sample.shape[1]
    tm = min(_tile(tp, 512), _tile(ts, 512))
    n_p, n_s = tp // tm, ts // tm
    args = (h, norm_ple)
    if split_output:
        return (_ple_call(*args, (p_prompt,), layer, w_gate, w_proj, tm, 0, n_p, n_p),
                _ple_call(*args, (p_sample,), layer, w_gate, w_proj, tm, n_p, n_s, n_p))
    return _ple_call(*args, (p_prompt, p_sample), layer, w_gate, w_proj, tm, 0, n_p + n_s, n_p)


def _qkv_kernel(h_ref, nm_ref, w_ref, seg_ref, gain_ref, q_ref, kv_ref, kmax_ref, xn_ref, *, n_q_tiles, k_width):
    j = pl.program_id(1)

    @pl.when(j == 0)
    def _():
        xn_ref[...] = _rms_rows(h_ref[...], nm_ref[...]).astype(BF16)

    x = _dot(xn_ref[...], w_ref[...])
    ms = _dot((x * x).astype(BF16), seg_ref[...]) * (1.0 / ATT_HD)
    normed = x * lax.rsqrt(ms + RMS_EPS) * gain_ref[0]

    @pl.when(j < n_q_tiles)
    def _():
        q_ref[...] = normed.astype(BF16)

    @pl.when(j == n_q_tiles)
    def _():
        is_k = lax.broadcasted_iota(jnp.int32, x.shape, 1) < k_width
        kv_ref[...] = jnp.where(is_k, normed, x)
        n_ch = x.shape[0] // CHUNK
        peak = jnp.max(jnp.where(is_k, jnp.abs(normed), 0.0).reshape(n_ch, CHUNK, x.shape[1]), axis=1)
        kmax_ref[0] = jnp.broadcast_to(jnp.max(peak, axis=1, keepdims=True), (n_ch, 128))


def _qkv(h, norm_mix, w_qkv, q_gain, k_gain):
    T, D = h.shape
    n_out = w_qkv.shape[1]
    tn = ATT_GROUP * ATT_HD
    n_q_tiles = D // tn
    assert n_out == D + tn
    kv_w = (n_out - D) // 2
    tm = _tile(T, 1024)
    n_ch = tm // CHUNK
    seg = np.kron(np.eye(tn // ATT_HD, dtype=np.float32), np.ones((ATT_HD, ATT_HD), np.float32))
    q_row = jnp.tile(q_gain.astype(F32), tn // ATT_HD) * (ATT_HD ** -0.5)
    kv_row = jnp.concatenate([jnp.tile(k_gain.astype(F32), kv_w // ATT_HD), jnp.ones((tn - kv_w,), F32)])
    gain = jnp.stack([q_row] * n_q_tiles + [kv_row])[:, None, :]
    q, kv, kmax = pl.pallas_call(
        functools.partial(_qkv_kernel, n_q_tiles=n_q_tiles, k_width=kv_w),
        grid=(T // tm, n_q_tiles + 1),
        in_specs=[pl.BlockSpec((tm, D), lambda i, j: (i, 0)),
                  pl.BlockSpec((1, D), lambda i, j: (0, 0)),
                  pl.BlockSpec((D, tn), lambda i, j: (0, j)),
                  _resident((tn, tn)),
                  pl.BlockSpec((1, 1, tn), lambda i, j: (j, 0, 0))],
        out_specs=[pl.BlockSpec((tm, tn), lambda i, j: (i, jnp.minimum(j, n_q_tiles - 1))),
                   pl.BlockSpec((tm, tn), lambda i, j: (i, 0)),
                   pl.BlockSpec((1, n_ch, 128), lambda i, j: (i, 0, 0))],
        out_shape=[jax.ShapeDtypeStruct((T, D), BF16), jax.ShapeDtypeStruct((T, tn), F32),
                   jax.ShapeDtypeStruct((T // tm, n_ch, 128), F32)],
        scratch_shapes=[pltpu.VMEM((tm, D), BF16)],
        compiler_params=_params("parallel", "arbitrary"),
        name="qkv_proj",
    )(h, norm_mix, w_qkv, jnp.asarray(seg, BF16), gain)
    return q, kv, kmax[:, :, 0].reshape(-1)


def _alibi_table(n_heads):
    slopes = np.float32(2.0) ** (np.float32(-8.0) * np.arange(1, n_heads + 1, dtype=np.float32) / np.float32(n_heads))
    qi = np.arange(CHUNK)[:, None]
    kj = np.arange((ATT_PREV + 1) * CHUNK)[None, :]
    dist = np.abs(ATT_PREV * CHUNK + qi - kj).astype(np.float32)
    return (-slopes[:, None, None] * dist).astype(np.float32)


def _attn_kernel(scal_ref, kmax_ref, cmax_ref, q_ref, kvo_ref, kvp_ref, ck_ref, cv_ref, alibi_ref, o_ref,
                 *, n_prompt_steps, steps_per_seq, n_kv_heads):
    m = pl.program_id(0)
    is_sample = m >= n_prompt_steps
    no_prev = jnp.logical_and(jnp.logical_not(is_sample), lax.rem(m, steps_per_seq) == 0)
    kvw = n_kv_heads * ATT_HD
    n_keys = (ATT_PREV + 1) * CHUNK
    n_heads = n_kv_heads * ATT_GROUP
    pairs = ATT_GROUP // 2
    qw = ATT_GROUP * ATT_HD
    lanes = 2 * ATT_HD

    kvo = kvo_ref[...]
    kvp = kvp_ref[...]
    low = lax.broadcasted_iota(jnp.int32, (n_keys, lanes), 1) < ATT_HD
    key_row = lax.broadcasted_iota(jnp.int32, (n_keys, kvw), 0)
    kj = lax.broadcasted_iota(jnp.int32, (CHUNK, n_keys), 1)

    def split_halves(t, odd):
        rolled = pltpu.roll(t, ATT_HD, 1)
        in_low, in_high = (rolled, t) if odd else (t, rolled)
        zero = jnp.zeros_like(t)
        return jnp.where(low, in_low, zero).astype(BF16), jnp.where(low, zero, in_high).astype(BF16)

    for t in range(PAIR):
        rows = slice(t * CHUNK, (t + 1) * CHUNK)
        i = m * PAIR + t
        before = jnp.concatenate([kvp, kvo], axis=0)[t * CHUNK:(t + ATT_PREV) * CHUNK]
        n_missing = jnp.where(no_prev, ATT_PREV - t, 0)
        first_valid = n_missing * CHUNK
        k_prev = jnp.where(is_sample, ck_ref[t], before[:, :kvw])
        v_prev = jnp.where(is_sample, cv_ref[t], before[:, kvw:])
        k_all = jnp.concatenate([k_prev, kvo[rows, :kvw]], axis=0)
        key_ok = key_row >= first_valid
        v_all = jnp.where(key_ok, jnp.concatenate([v_prev, kvo[rows, kvw:]], axis=0), 0.0)
        ones = jnp.where(key_ok[:, :lanes], 1.0, 0.0).astype(BF16)
        valid = kj >= first_valid

        def kv_head_operands(kh):
            col = (kh // 2) * lanes
            ks = split_halves(k_all[:, col:col + lanes], kh % 2)
            vs = split_halves(v_all[:, col:col + lanes], kh % 2)
            qs = jnp.concatenate([q_ref[rows, kh * qw + p * lanes: kh * qw + (p + 1) * lanes]
                                  for p in range(pairs)], axis=0)
            return qs, ks, vs

        k_prompt = jnp.maximum(kmax_ref[i], jnp.maximum(kmax_ref[i - jnp.minimum(1, ATT_PREV - n_missing)],
                                                        kmax_ref[i - (ATT_PREV - n_missing)]))
        k_sample = jnp.maximum(kmax_ref[i], cmax_ref[jnp.maximum(m - n_prompt_steps, 0) * PAIR + t])
        bound = scal_ref[n_heads] * (ATT_HD ** 0.5) * jnp.where(is_sample, k_sample, k_prompt)
        small = bound <= ATT_SAFE_EXP

        @pl.when(small)
        def _():
            units = [(kh, parity) for kh in range(n_kv_heads) for parity in range(2)]
            ops = [kv_head_operands(kh) for kh in range(n_kv_heads)]
            scores = [lax.dot_general(ops[kh][0], ops[kh][1][parity], NT_DIMS, preferred_element_type=F32)
                      for kh, parity in units]
            probs = []
            for (kh, parity), s_all in zip(units, scores):
                blocks = []
                for p in range(pairs):
                    head = kh * ATT_GROUP + 2 * p + parity
                    blocks.append(jnp.exp(s_all[p * CHUNK:(p + 1) * CHUNK] + alibi_ref[head]).astype(BF16))
                probs.append(jnp.concatenate(blocks, axis=0))
            both = [_dot(pr, jnp.concatenate([ops[kh][2][parity], ones], axis=1))
                    for (kh, parity), pr in zip(units, probs)]
            for kh in range(n_kv_heads):
                for p in range(pairs):
                    blk = slice(p * CHUNK, (p + 1) * CHUNK)
                    out = None
                    for parity in range(2):
                        head = kh * ATT_GROUP + 2 * p + parity
                        b = both[2 * kh + parity]
                        den = b[blk, lanes:] + jnp.exp(jnp.full((1, lanes), scal_ref[head], F32))
                        part = b[blk, :lanes] / den
                        out = part if out is None else out + part
                    o_ref[rows, kh * qw + p * lanes: kh * qw + (p + 1) * lanes] = out.astype(BF16)

        @pl.when(jnp.logical_not(small))
        def _():
            for kh in range(n_kv_heads):
                qs, ks, vs = kv_head_operands(kh)
                acc = None
                for parity in range(2):
                    s_all = lax.dot_general(qs, ks[parity], NT_DIMS, preferred_element_type=F32)
                    probs, inv = [], []
                    for p in range(pairs):
                        head = kh * ATT_GROUP + 2 * p + parity
                        s = jnp.where(valid, s_all[p * CHUNK:(p + 1) * CHUNK] + alibi_ref[head], -jnp.inf)
                        sink = scal_ref[head]
                        mx = jnp.maximum(jnp.max(s, axis=-1, keepdims=True), sink)
                        e = jnp.exp(s - mx)
                        den = jnp.sum(e, axis=-1, keepdims=True) + jnp.exp(sink - mx)
                        probs.append(e.astype(BF16))
                        inv.append(1.0 / den)
                    part = _dot(jnp.concatenate(probs, axis=0), vs[parity]) * jnp.concatenate(inv, axis=0)
                    acc = part if acc is None else acc + part
                for p in range(pairs):
                    o_ref[rows, kh * qw + p * lanes: kh * qw + (p + 1) * lanes] = (
                        acc[p * CHUNK:(p + 1) * CHUNK].astype(BF16))


def _attention(q, kv, kmax, cache_k, cache_v, scalars, n_prompt_seq, chunks_per_seq):
    T, d_model = q.shape
    rows = PAIR * CHUNK
    n_steps = T // rows
    assert chunks_per_seq % PAIR == 0 and cache_k.shape[0] % PAIR == 0 and PAIR == ATT_PREV
    steps_per_seq = chunks_per_seq // PAIR
    nps = n_prompt_seq * steps_per_seq
    kvw2 = kv.shape[1]
    n_kv_heads = kvw2 // (2 * ATT_HD)
    n_cache = cache_k.shape[1]
    assert n_cache == ATT_PREV * CHUNK

    def prev_step(i):
        inside = jnp.logical_and(i < nps, lax.rem(i, steps_per_seq) >= 1)
        return (jnp.where(inside, i - 1, i), 0)

    alibi = _alibi_table(n_kv_heads * ATT_GROUP)
    cmax = jnp.max(jnp.abs(cache_k), axis=(1, 2))
    smem = pl.BlockSpec(memory_space=pltpu.SMEM)
    cache_spec = pl.BlockSpec((PAIR, n_cache, kvw2 // 2), lambda i: (jnp.maximum(i - nps, 0), 0, 0))
    kern = functools.partial(_attn_kernel, n_prompt_steps=nps, steps_per_seq=steps_per_seq,
                             n_kv_heads=n_kv_heads)
    return pl.pallas_call(
        kern,
        grid=(n_steps,),
        in_specs=[smem, smem, smem,
                  pl.BlockSpec((rows, d_model), lambda i: (i, 0)),
                  pl.BlockSpec((rows, kvw2), lambda i: (i, 0)),
                  pl.BlockSpec((rows, kvw2), prev_step),
                  cache_spec, cache_spec, _resident(alibi.shape)],
        out_specs=pl.BlockSpec((rows, d_model), lambda i: (i, 0)),
        out_shape=jax.ShapeDtypeStruct((T, d_model), BF16),
        compiler_params=_params("parallel"),
        name="swa_attention",
    )(scalars, kmax, cmax, q, kv, kv, cache_k, cache_v, jnp.asarray(alibi))


def kernel(x_prompt, x_sample, state_hgrn, cache_k, cache_v, p_prompt, p_sample, norm_mix, norm_ffn,
           norm_ple, a_w_in, a_lb_logits, a_g_norm, a_w_o, b_w_qkv, b_q_norm, b_k_norm, b_sinks, b_w_o,
           f_w_gu, f_w_down, ple_w_proj, ple_w_gate):
    B, L, D = x_prompt.shape
    Bs, Ls, _ = x_sample.shape
    assert L % CHUNK == 0 and Ls == CHUNK
    tp, ts = B * L, Bs * Ls
    cps = L // CHUNK
    depth = norm_mix.shape[0]
    assert depth == 2

    xp, xs = x_prompt.reshape(tp, D), x_sample.reshape(ts, D)
    pp, ps = p_prompt.reshape(depth, tp, -1), p_sample.reshape(depth, ts, -1)
    vec = lambda w: w.reshape(1, -1).astype(F32)

    (q, k, v, lf, gs), (w_gu0, w_down0, w_o0, w_gate0, w_proj0) = _hgrn_in(
        xp, xs, vec(norm_mix[0]), a_w_in[0].astype(BF16), a_lb_logits.astype(F32),
        side=[(f_w_gu, 0), (f_w_down, 0), (a_w_o, 0), (ple_w_gate, 0), (ple_w_proj, 0)])

    o, state_p, state_s = _gla(q, k, v, lf, state_hgrn[0].astype(F32), B, cps)
    h1 = _out_proj((o, gs, vec(a_g_norm[0])), w_o0, (xp, xs), gated=True)
    h2, (w_gu1, w_down1, w_qkv, w_o1, w_gate1, w_proj1) = _ffn(
        h1, vec(norm_ffn[0]), w_gu0, w_down0,
        side=[(f_w_gu, 1), (f_w_down, 1), (b_w_qkv, 0), (b_w_o, 0), (ple_w_gate, 1), (ple_w_proj, 1)])
    h = _ple(h2, vec(norm_ple[0]), pp, ps, 0, w_gate0, w_proj0, split_output=False)

    qn, kvn, kmax = _qkv(h, vec(norm_mix[1]), w_qkv, b_q_norm[0], b_k_norm[0])
    n_kv = cache_k.shape[3]
    kvw = n_kv * ATT_HD
    scalars = jnp.concatenate([b_sinks[0].astype(F32), jnp.max(jnp.abs(b_q_norm[0])).reshape(1).astype(F32)])
    att = _attention(qn, kvn, kmax, cache_k[0].reshape(Bs, -1, kvw).astype(F32),
                     cache_v[0].reshape(Bs, -1, kvw).astype(F32), scalars, B, cps)
    h1 = _out_proj((att,), w_o1, (h,), gated=False)
    h2, _ = _ffn(h1, vec(norm_ffn[1]), w_gu1, w_down1)
    yp, ys = _ple(h2, vec(norm_ple[1]), pp, ps, 1, w_gate1, w_proj1, split_output=True)

    keep = min(ATT_PREV * CHUNK, L)
    kv_p = kvn[:tp].reshape(B, L, 2 * kvw)[:, L - keep:]
    kv_s = kvn[tp:].reshape(Bs, Ls, 2 * kvw)
    heads = lambda t: t.reshape(*t.shape[:2], n_kv, ATT_HD)[None]
    return (yp.reshape(B, L, D), ys.reshape(Bs, Ls, D), state_p[None], state_s[None],
            heads(kv_p[..., :kvw]), heads(kv_p[..., kvw:]), heads(kv_s[..., :kvw]), heads(kv_s[..., kvw:]))
```

```python
import functools

import numpy as np
import jax
import jax.numpy as jnp
from jax import lax
from jax.experimental import pallas as pl
from jax.experimental.pallas import tpu as pltpu

F32 = jnp.float32
BF16 = jnp.bfloat16

RMS_EPS = 1e-6
CHUNK = 64
PAIR = 4
HEAD_DK = 128
ATT_HD = 64
ATT_GROUP = 8
ATT_PREV = 2
GLA_LEVELS = (64, 32, 16, 8, 4, 2)
GLA_SAFE_EXP = 60.0
ATT_SAFE_EXP = 70.0

NT_DIMS = (((1,), (1,)), ((), ()))
TN_DIMS = (((0,), (0,)), ((), ()))

VMEM_LIMIT_BYTES = 60 * 1024 * 1024


def _params(*sem):
    return pltpu.CompilerParams(dimension_semantics=sem, vmem_limit_bytes=VMEM_LIMIT_BYTES)


def _tile(n, pref):
    t = pref
    while t > 8 and n % t:
        t //= 2
    assert n % t == 0, (n, pref)
    return t


def _split_rows(tm, n_p, width, sample_buffers=None):
    mode = {} if sample_buffers is None else dict(pipeline_mode=pl.Buffered(sample_buffers))
    return (pl.BlockSpec((tm, width), lambda i, *_: (jnp.minimum(i, n_p - 1), 0)),
            pl.BlockSpec((tm, width), lambda i, *_: (jnp.maximum(i - n_p, 0), 0), **mode))


def _resident(shape):
    return pl.BlockSpec(shape, lambda *_: (0,) * len(shape), pipeline_mode=pl.Buffered(1))


def _row_chunks(rows, size=256):
    size = min(size, rows)
    assert rows % size == 0
    return [slice(r, r + size) for r in range(0, rows, size)]


def _side_cast_specs(side, n_steps, step_of):
    in_specs, out_specs, out_shape = [], [], []
    for arr, idx in side:
        _, rows, cols = arr.shape
        rb = 16
        while rows // rb > n_steps:
            rb *= 2
        assert rows % rb == 0
        last = rows // rb - 1
        in_specs.append(pl.BlockSpec(
            (1, rb, cols), lambda *g, idx=idx, last=last: (idx, jnp.minimum(step_of(*g), last), 0)))
        out_specs.append(pl.BlockSpec((rb, cols), lambda *g, last=last: (jnp.minimum(step_of(*g), last), 0)))
        out_shape.append(jax.ShapeDtypeStruct((rows, cols), BF16))
    return in_specs, out_specs, out_shape


def _side_cast(in_refs, out_refs):
    for src, dst in zip(in_refs, out_refs):
        dst[...] = src[0].astype(BF16)


def _sigmoid(x):
    return 1.0 / (1.0 + jnp.exp(-x))


def _rms_rows(x, w):
    ms = jnp.mean(x * x, axis=-1, keepdims=True)
    return x * lax.rsqrt(ms + RMS_EPS) * w


def _dot(a, b):
    return jnp.dot(a, b, preferred_element_type=F32)


def _hgrn_in_kernel(xp_ref, xs_ref, nw_ref, wq_ref, wf_ref, wi_ref, wg_ref, lbl_ref, *refs, n_p, n_side):
    side_in, refs = refs[:n_side], refs[n_side:]
    q_ref, k_ref, v_ref, lf_ref, gs_ref = refs[:5]
    side_out, (xn_ref,) = refs[5:5 + n_side], refs[5 + n_side:]
    first = pl.program_id(1) == 0
    is_prompt = pl.program_id(0) < n_p

    @pl.when(jnp.logical_and(first, is_prompt))
    def _():
        xn_ref[...] = _rms_rows(xp_ref[...], nw_ref[...]).astype(BF16)

    @pl.when(jnp.logical_and(first, jnp.logical_not(is_prompt)))
    def _():
        xn_ref[...] = _rms_rows(xs_ref[...], nw_ref[...]).astype(BF16)

    xn = xn_ref[...]
    q = _dot(xn, wq_ref[...])
    q_ref[...] = (q * _sigmoid(q)).astype(BF16)
    logits = lbl_ref[...]
    ex = jnp.exp(logits - jnp.max(logits, axis=0, keepdims=True))
    lb = ex[0:1] / jnp.sum(ex, axis=0, keepdims=True)
    f = _dot(xn, wf_ref[...])
    forget = lb + (1.0 - lb) * _sigmoid(f)
    k_ref[...] = (1.0 - forget).astype(BF16)
    lf_ref[...] = jnp.log(forget)
    v_ref[...] = _dot(xn, wi_ref[...]).astype(BF16)
    g = _dot(xn, wg_ref[...])
    gs_ref[...] = (g * _sigmoid(g)).astype(BF16)
    _side_cast(side_in, side_out)


def _hgrn_in(xp, xs, norm_w, w_in, lb_logits, side):
    (tp, D), ts = xp.shape, xs.shape[0]
    T = tp + ts
    tm = min(_tile(tp, 1024), _tile(ts, 1024))
    tn = 256
    nb = D // tn
    out = pl.BlockSpec((tm, tn), lambda i, j: (i, j))

    def wspec(g):
        return pl.BlockSpec((D, tn), lambda i, j, g=g: (0, j + g * nb))

    side_in, side_out, side_shape = _side_cast_specs(side, (T // tm) * nb, lambda i, j: i * nb + j)
    res = pl.pallas_call(
        functools.partial(_hgrn_in_kernel, n_p=tp // tm, n_side=len(side)),
        grid=(T // tm, nb),
        in_specs=[*_split_rows(tm, tp // tm, D, sample_buffers=1), pl.BlockSpec((1, D), lambda i, j: (0, 0)),
                  wspec(0), wspec(1), wspec(2), wspec(3),
                  pl.BlockSpec((lb_logits.shape[0], tn), lambda i, j: (0, j)), *side_in],
        out_specs=[out] * 5 + side_out,
        out_shape=[jax.ShapeDtypeStruct((T, D), BF16)] * 2
        + [jax.ShapeDtypeStruct((T, D), BF16), jax.ShapeDtypeStruct((T, D), F32),
           jax.ShapeDtypeStruct((T, D), BF16)] + side_shape,
        scratch_shapes=[pltpu.VMEM((tm, D), BF16)],
        compiler_params=_params("arbitrary", "arbitrary"),
        name="hgrn_in",
    )(xp, xs, norm_w, w_in, w_in, w_in, w_in, lb_logits, *[a for a, _ in side])
    return res[:5], res[5:]


def _gla_constants():
    c = CHUNK
    t = np.arange(c)[:, None]
    s = np.arange(c)[None, :]
    zs = [s <= t, s > t]
    level = np.full((c, c), -1, np.int32)
    for li, p in enumerate(GLA_LEVELS, start=1):
        half = p // 2
        mid = (t // p) * p + half - 1
        upper = (t % p) >= half
        zs.append(np.where(upper, (s > mid) & (s <= t), (s > t) & (s <= mid)))
        level[(t // p == s // p) & upper & ((s % p) < half)] = li
    np.fill_diagonal(level, 0)
    z = np.concatenate(zs, axis=0).astype(np.float32)
    zf = np.concatenate([zs[0], zs[0]], axis=0).astype(np.float32)
    zf[c:] *= -1.0
    three = lambda m: np.concatenate([m, m, m], axis=1)
    return three(z), three(zf), level


def _gla_kernel(q_ref, k_ref, v_ref, lf_ref, lfn_ref, s0_ref, z_ref, zf_ref, lvl_ref, o_ref, sp_ref, ss_ref,
                st_ref, e_ref, dec_ref, *, n_prompt_steps, steps_per_seq, n_heads):
    m = pl.program_id(0)
    is_prompt = m < n_prompt_steps
    is_sample = jnp.logical_not(is_prompt)
    c = jnp.where(is_prompt, lax.rem(m, steps_per_seq), 0)
    chunk_rows = [slice(t * CHUNK, (t + 1) * CHUNK) for t in range(PAIR)]
    sls = [slice(h * HEAD_DK, (h + 1) * HEAD_DK) for h in range(n_heads)]
    lvl = lvl_ref[...]

    def exponents(z, rows):
        g = lf_ref[rows, :]
        g_hi = g.astype(BF16)
        r1 = g - g_hi.astype(F32)
        g_mid = r1.astype(BF16)
        g_lo = (r1 - g_mid.astype(F32)).astype(BF16)
        return _dot(z, jnp.concatenate([g_hi, g_mid, g_lo], axis=0))

    def note_next_decays(ref):
        for t, rows in enumerate(chunk_rows):
            dec_ref[t] = jnp.min(jnp.sum(ref[rows, :], axis=0, keepdims=True))

    def single_reference(rows):
        e_ref[0:2 * CHUNK, :] = jnp.exp(exponents(zf_ref[...], rows))
        q_hat = [(q_ref[rows, sl].astype(F32) * e_ref[0:CHUNK, sl]).astype(BF16) for sl in sls]
        k_bar = [(k_ref[rows, sl].astype(F32) * e_ref[CHUNK:2 * CHUNK, sl]).astype(BF16) for sl in sls]
        a = [lax.dot_general(q_hat[h], k_bar[h], NT_DIMS, preferred_element_type=F32) for h in range(n_heads)]
        o = [lax.dot_general(q_hat[h], st_ref[h].astype(BF16), NT_DIMS, preferred_element_type=F32)
             for h in range(n_heads)]
        for h, sl in enumerate(sls):
            o_ref[rows, sl] = (o[h] + _dot(jnp.where(lvl >= 0, a[h], 0.0).astype(BF16), v_ref[rows, sl])).astype(BF16)
        for h, sl in enumerate(sls):
            decay_all = e_ref[CHUNK - 1:CHUNK, sl]
            k_hat = (k_ref[rows, sl].astype(F32) * (e_ref[CHUNK:2 * CHUNK, sl] * decay_all)).astype(BF16)
            st_ref[h] = st_ref[h] * decay_all + lax.dot_general(v_ref[rows, sl], k_hat, TN_DIMS,
                                                                 preferred_element_type=F32)

    def binary_splitting(rows):
        e_ref[...] = jnp.exp(exponents(z_ref[...], rows))
        row = lax.broadcasted_iota(jnp.int32, (CHUNK, HEAD_DK), 0)
        for h, sl in enumerate(sls):
            qb = q_ref[rows, sl]
            kb = k_ref[rows, sl]
            vb = v_ref[rows, sl]
            qf = qb.astype(F32)
            kf = kb.astype(F32)
            st = st_ref[h]
            q_hat = (qf * e_ref[0:CHUNK, sl]).astype(BF16)
            k_hat = (kf * e_ref[CHUNK:2 * CHUNK, sl]).astype(BF16)
            o = lax.dot_general(q_hat, st.astype(BF16), NT_DIMS, preferred_element_type=F32)
            a = lax.dot_general(qb, kb, NT_DIMS, preferred_element_type=F32)
            a = jnp.where(lvl == 0, a, 0.0)
            for li, p in enumerate(GLA_LEVELS, start=1):
                on_q_side = (row & (p // 2)) != 0
                x = (jnp.where(on_q_side, qf, kf)
                     * e_ref[(li + 1) * CHUNK:(li + 2) * CHUNK, sl]).astype(BF16)
                a = jnp.where(lvl == li, lax.dot_general(x, x, NT_DIMS, preferred_element_type=F32), a)
            o_ref[rows, sl] = (o + _dot(a.astype(BF16), vb)).astype(BF16)
            decay_all = e_ref[CHUNK - 1:CHUNK, sl]
            st_ref[h] = st * decay_all + lax.dot_general(vb, k_hat, TN_DIMS, preferred_element_type=F32)

    @pl.when(m == 0)
    def _():
        note_next_decays(lf_ref)

    mild = [dec_ref[t] >= -GLA_SAFE_EXP for t in range(PAIR)]

    @pl.when(jnp.logical_and(is_prompt, c == 0))
    def _():
        st_ref[...] = jnp.zeros_like(st_ref)

    for t, rows in enumerate(chunk_rows):
        last = t == PAIR - 1

        @pl.when(is_sample)
        def _():
            for h in range(n_heads):
                st_ref[h] = s0_ref[t, h].T

        @pl.when(mild[t])
        def _():
            single_reference(rows)
            if last:
                note_next_decays(lfn_ref)

        @pl.when(jnp.logical_not(mild[t]))
        def _():
            binary_splitting(rows)
            if last:
                note_next_decays(lfn_ref)

        @pl.when(is_sample)
        def _():
            for h in range(n_heads):
                ss_ref[t, h] = st_ref[h].T

    @pl.when(jnp.logical_and(is_prompt, c == steps_per_seq - 1))
    def _():
        for h in range(n_heads):
            sp_ref[0, h] = st_ref[h].T


def _gla(q, k, v, lf, s0, n_prompt_seq, chunks_per_seq):
    T, D = q.shape
    n_heads = D // HEAD_DK
    rows = PAIR * CHUNK
    n_steps = T // rows
    assert chunks_per_seq % PAIR == 0 and s0.shape[0] % PAIR == 0
    steps_per_seq = chunks_per_seq // PAIR
    nps = n_prompt_seq * steps_per_seq
    zcat, zfast, level = _gla_constants()
    blk = pl.BlockSpec((rows, D), lambda i: (i, 0))
    h_blk = (n_heads, HEAD_DK, HEAD_DK)

    def prompt_seq(i):
        return (jnp.minimum(i // steps_per_seq, n_prompt_seq - 1), 0, 0, 0)

    def sample_pair(i):
        return (jnp.maximum(i - nps, 0), 0, 0, 0)

    kern = functools.partial(_gla_kernel, n_prompt_steps=nps, steps_per_seq=steps_per_seq, n_heads=n_heads)
    return pl.pallas_call(
        kern,
        grid=(n_steps,),
        in_specs=[blk, blk, blk, blk,
                  pl.BlockSpec((rows, D), lambda i: (jnp.minimum(i + 1, n_steps - 1), 0)),
                  pl.BlockSpec((PAIR, *h_blk), sample_pair),
                  pl.BlockSpec(zcat.shape, lambda i: (0, 0)),
                  pl.BlockSpec(zfast.shape, lambda i: (0, 0)),
                  pl.BlockSpec(level.shape, lambda i: (0, 0))],
        out_specs=[blk, pl.BlockSpec((1, *h_blk), prompt_seq), pl.BlockSpec((PAIR, *h_blk), sample_pair)],
        out_shape=[jax.ShapeDtypeStruct((T, D), BF16),
                   jax.ShapeDtypeStruct((n_prompt_seq, *h_blk), F32),
                   jax.ShapeDtypeStruct((s0.shape[0], *h_blk), F32)],
        scratch_shapes=[pltpu.VMEM(h_blk, F32),
                        pltpu.VMEM((zcat.shape[0], D), F32),
                        pltpu.SMEM((PAIR,), F32)],
        compiler_params=_params("arbitrary"),
        name="gla_scan",
    )(q, k, v, lf, lf, s0, jnp.asarray(zcat, BF16), jnp.asarray(zfast, BF16), jnp.asarray(level))


def _out_proj_kernel(*refs, gated, n_p):
    if gated:
        o_ref, gs_ref, gn_ref, w_ref, *h_refs, h1_ref = refs
    else:
        a_ref, w_ref, *h_refs, h1_ref = refs
    for r in _row_chunks(h1_ref.shape[0]):
        if gated:
            a = (_rms_rows(o_ref[r, :].astype(F32), gn_ref[...]) * gs_ref[r, :].astype(F32)).astype(BF16)
        else:
            a = a_ref[r, :]
        if len(h_refs) == 2:
            h = jnp.where(pl.program_id(0) < n_p, h_refs[0][r, :], h_refs[1][r, :])
        else:
            h = h_refs[0][r, :]
        h1_ref[r, :] = h + _dot(a, w_ref[...])


def _out_proj(a_inputs, w, hs, gated):
    T, D = a_inputs[0].shape
    tm = min(_tile(h.shape[0], 512) for h in hs)
    row = pl.BlockSpec((tm, D), lambda i: (i, 0))
    vec = pl.BlockSpec((1, D), lambda i: (0, 0))
    wspec = _resident((D, D))
    n_p = hs[0].shape[0] // tm
    h_specs = [row] if len(hs) == 1 else list(_split_rows(tm, n_p, D))
    if gated:
        in_specs = [row, row, vec, wspec, *h_specs]
    else:
        in_specs = [row, wspec, *h_specs]
    return pl.pallas_call(
        functools.partial(_out_proj_kernel, gated=gated, n_p=n_p),
        grid=(T // tm,),
        in_specs=in_specs,
        out_specs=row,
        out_shape=jax.ShapeDtypeStruct((T, D), F32),
        compiler_params=_params("parallel"),
        name="out_proj",
    )(*a_inputs, w, *hs)


def _ffn_kernel(h_ref, nf_ref, wg_ref, wu_ref, wd_ref, *refs, n_side):
    side_in, h2_ref = refs[:n_side], refs[n_side]
    side_out, (xn_ref,) = refs[n_side + 1:2 * n_side + 1], refs[2 * n_side + 1:]

    @pl.when(pl.program_id(1) == 0)
    def _():
        h = h_ref[...]
        xn_ref[...] = _rms_rows(h, nf_ref[...]).astype(BF16)
        h2_ref[...] = h

    xn = xn_ref[...]
    gate = _dot(xn, wg_ref[...])
    up = _dot(xn, wu_ref[...])
    h2_ref[...] += _dot((gate * _sigmoid(gate) * up).astype(BF16), wd_ref[...])
    _side_cast(side_in, side_out)


def _ffn(h, norm_ffn, w_gu, w_down, side=()):
    T, D = h.shape
    ff = w_down.shape[0]
    tm = _tile(T, 1024)
    tf = 512
    nf = ff // tf
    row = pl.BlockSpec((tm, D), lambda i, j: (i, 0))
    side_in, side_out, side_shape = _side_cast_specs(side, (T // tm) * nf, lambda i, j: i * nf + j)
    res = pl.pallas_call(
        functools.partial(_ffn_kernel, n_side=len(side)),
        grid=(T // tm, nf),
        in_specs=[row, pl.BlockSpec((1, D), lambda i, j: (0, 0)),
                  pl.BlockSpec((D, tf), lambda i, j: (0, j)),
                  pl.BlockSpec((D, tf), lambda i, j: (0, j + nf)),
                  pl.BlockSpec((tf, D), lambda i, j: (j, 0)), *side_in],
        out_specs=[row] + side_out,
        out_shape=[jax.ShapeDtypeStruct((T, D), F32)] + side_shape,
        scratch_shapes=[pltpu.VMEM((tm, D), BF16)],
        compiler_params=_params("arbitrary", "arbitrary"),
        name="ffn",
    )(h, norm_ffn, w_gu, w_gu, w_down, *[a for a, _ in side])
    return res[0], res[1:]


def _ple_kernel(h_ref, np_ref, *refs, n_p):
    *p_refs, wg_ref, wp_ref, out_ref = refs
    for r in _row_chunks(h_ref.shape[0]):
        h = h_ref[r, :]
        gate = _sigmoid(_dot(_rms_rows(h, np_ref[...]).astype(BF16), wg_ref[...]))
        if len(p_refs) == 2:
            p = jnp.where(pl.program_id(0) < n_p, p_refs[0][0, r, :], p_refs[1][0, r, :])
        else:
            p = p_refs[0][0, r, :]
        out_ref[r, :] = h + _dot(p.astype(BF16), wp_ref[...]) * gate


def _ple_call(h, norm_ple, ps, layer, w_gate, w_proj, tm, first_block, n_blocks, n_p):
    D = h.shape[1]
    pd = ps[0].shape[2]
    if len(ps) == 2:
        p_specs = [pl.BlockSpec((1, tm, pd), lambda i: (layer, jnp.minimum(i, n_p - 1), 0)),
                   pl.BlockSpec((1, tm, pd), lambda i: (layer, jnp.maximum(i - n_p, 0), 0))]
    else:
        p_specs = [pl.BlockSpec((1, tm, pd), lambda i: (layer, i, 0))]
    return pl.pallas_call(
        functools.partial(_ple_kernel, n_p=n_p),
        grid=(n_blocks,),
        in_specs=[pl.BlockSpec((tm, D), lambda i: (i + first_block, 0)),
                  pl.BlockSpec((1, D), lambda i: (0, 0)),
                  *p_specs, _resident((D, D)), _resident((pd, D))],
        out_specs=pl.BlockSpec((tm, D), lambda i: (i, 0)),
        out_shape=jax.ShapeDtypeStruct((n_blocks * tm, D), F32),
        compiler_params=_params("parallel"),
        name="ple",
    )(h, norm_ple, *ps, w_gate, w_proj)


def _ple(h, norm_ple, p_prompt, p_sample, layer, w_gate, w_proj, split_output):
    tp, ts = p_prompt.shape[1], p_sample.shape[1]
    tm = min(_tile(tp, 512), _tile(ts, 512))
    n_p, n_s = tp // tm, ts // tm
    args = (h, norm_ple)
    if split_output:
        return (_ple_call(*args, (p_prompt,), layer, w_gate, w_proj, tm, 0, n_p, n_p),
                _ple_call(*args, (p_sample,), layer, w_gate, w_proj, tm, n_p, n_s, n_p))
    return _ple_call(*args, (p_prompt, p_sample), layer, w_gate, w_proj, tm, 0, n_p + n_s, n_p)


def _qkv_kernel(h_ref, nm_ref, w_ref, seg_ref, gain_ref, q_ref, kv_ref, kmax_ref, xn_ref, *, n_q_tiles, k_width):
    j = pl.program_id(1)

    @pl.when(j == 0)
    def _():
        xn_ref[...] = _rms_rows(h_ref[...], nm_ref[...]).astype(BF16)

    x = _dot(xn_ref[...], w_ref[...])
    ms = _dot((x * x).astype(BF16), seg_ref[...]) * (1.0 / ATT_HD)
    normed = x * lax.rsqrt(ms + RMS_EPS) * gain_ref[0]

    @pl.when(j < n_q_tiles)
    def _():
        q_ref[...] = normed.astype(BF16)

    @pl.when(j == n_q_tiles)
    def _():
        is_k = lax.broadcasted_iota(jnp.int32, x.shape, 1) < k_width
        kv_ref[...] = jnp.where(is_k, normed, x)
        n_ch = x.shape[0] // CHUNK
        peak = jnp.max(jnp.where(is_k, jnp.abs(normed), 0.0).reshape(n_ch, CHUNK, x.shape[1]), axis=1)
        kmax_ref[0] = jnp.broadcast_to(jnp.max(peak, axis=1, keepdims=True), (n_ch, 128))


def _qkv(h, norm_mix, w_qkv, q_gain, k_gain):
    T, D = h.shape
    n_out = w_qkv.shape[1]
    tn = ATT_GROUP * ATT_HD
    n_q_tiles = D // tn
    assert n_out == D + tn
    kv_w = (n_out - D) // 2
    tm = _tile(T, 1024)
    n_ch = tm // CHUNK
    seg = np.kron(np.eye(tn // ATT_HD, dtype=np.float32), np.ones((ATT_HD, ATT_HD), np.float32))
    q_row = jnp.tile(q_gain.astype(F32), tn // ATT_HD) * (ATT_HD ** -0.5)
    kv_row = jnp.concatenate([jnp.tile(k_gain.astype(F32), kv_w // ATT_HD), jnp.ones((tn - kv_w,), F32)])
    gain = jnp.stack([q_row] * n_q_tiles + [kv_row])[:, None, :]
    q, kv, kmax = pl.pallas_call(
        functools.partial(_qkv_kernel, n_q_tiles=n_q_tiles, k_width=kv_w),
        grid=(T // tm, n_q_tiles + 1),
        in_specs=[pl.BlockSpec((tm, D), lambda i, j: (i, 0)),
                  pl.BlockSpec((1, D), lambda i, j: (0, 0)),
                  pl.BlockSpec((D, tn), lambda i, j: (0, j)),
                  _resident((tn, tn)),
                  pl.BlockSpec((1, 1, tn), lambda i, j: (j, 0, 0))],
        out_specs=[pl.BlockSpec((tm, tn), lambda i, j: (i, jnp.minimum(j, n_q_tiles - 1))),
                   pl.BlockSpec((tm, tn), lambda i, j: (i, 0)),
                   pl.BlockSpec((1, n_ch, 128), lambda i, j: (i, 0, 0))],
        out_shape=[jax.ShapeDtypeStruct((T, D), BF16), jax.ShapeDtypeStruct((T, tn), F32),
                   jax.ShapeDtypeStruct((T // tm, n_ch, 128), F32)],
        scratch_shapes=[pltpu.VMEM((tm, D), BF16)],
        compiler_params=_params("parallel", "arbitrary"),
        name="qkv_proj",
    )(h, norm_mix, w_qkv, jnp.asarray(seg, BF16), gain)
    return q, kv, kmax[:, :, 0].reshape(-1)


def _alibi_table(n_heads):
    slopes = np.float32(2.0) ** (np.float32(-8.0) * np.arange(1, n_heads + 1, dtype=np.float32) / np.float32(n_heads))
    qi = np.arange(CHUNK)[:, None]
    kj = np.arange((ATT_PREV + 1) * CHUNK)[None, :]
    dist = np.abs(ATT_PREV * CHUNK + qi - kj).astype(np.float32)
    return (-slopes[:, None, None] * dist).astype(np.float32)


def _attn_kernel(scal_ref, kmax_ref, cmax_ref, q_ref, kvo_ref, kvp_ref, ck_ref, cv_ref, alibi_ref, o_ref,
                 *, n_prompt_steps, steps_per_seq, n_kv_heads):
    m = pl.program_id(0)
    is_sample = m >= n_prompt_steps
    no_prev = jnp.logical_and(jnp.logical_not(is_sample), lax.rem(m, steps_per_seq) == 0)
    kvw = n_kv_heads * ATT_HD
    n_keys = (ATT_PREV + 1) * CHUNK
    n_heads = n_kv_heads * ATT_GROUP
    pairs = ATT_GROUP // 2
    qw = ATT_GROUP * ATT_HD
    lanes = 2 * ATT_HD

    kvo = kvo_ref[...]
    window = jnp.concatenate([kvp_ref[(PAIR - ATT_PREV) * CHUNK:, :], kvo], axis=0)
    low = lax.broadcasted_iota(jnp.int32, (n_keys, lanes), 1) < ATT_HD
    key_row = lax.broadcasted_iota(jnp.int32, (n_keys, kvw), 0)
    kj = lax.broadcasted_iota(jnp.int32, (CHUNK, n_keys), 1)

    def split_halves(t, odd):
        rolled = pltpu.roll(t, ATT_HD, 1)
        in_low, in_high = (rolled, t) if odd else (t, rolled)
        zero = jnp.zeros_like(t)
        return jnp.where(low, in_low, zero).astype(BF16), jnp.where(low, zero, in_high).astype(BF16)

    for t in range(PAIR):
        rows = slice(t * CHUNK, (t + 1) * CHUNK)
        i = m * PAIR + t
        before = window[t * CHUNK:(t + ATT_PREV) * CHUNK]
        n_missing = jnp.where(no_prev, max(ATT_PREV - t, 0), 0)
        first_valid = n_missing * CHUNK
        k_prev = jnp.where(is_sample, ck_ref[t], before[:, :kvw])
        v_prev = jnp.where(is_sample, cv_ref[t], before[:, kvw:])
        k_all = jnp.concatenate([k_prev, kvo[rows, :kvw]], axis=0)
        key_ok = key_row >= first_valid
        v_all = jnp.where(key_ok, jnp.concatenate([v_prev, kvo[rows, kvw:]], axis=0), 0.0)
        ones = jnp.where(key_ok[:, :lanes], 1.0, 0.0).astype(BF16)
        valid = kj >= first_valid

        def kv_head_operands(kh):
            col = (kh // 2) * lanes
            ks = split_halves(k_all[:, col:col + lanes], kh % 2)
            vs = split_halves(v_all[:, col:col + lanes], kh % 2)
            qs = jnp.concatenate([q_ref[rows, kh * qw + p * lanes: kh * qw + (p + 1) * lanes]
                                  for p in range(pairs)], axis=0)
            return qs, ks, vs

        k_prompt = jnp.maximum(kmax_ref[i], jnp.maximum(kmax_ref[i - jnp.minimum(1, ATT_PREV - n_missing)],
                                                        kmax_ref[i - (ATT_PREV - n_missing)]))
        k_sample = jnp.maximum(kmax_ref[i], cmax_ref[jnp.maximum(m - n_prompt_steps, 0) * PAIR + t])
        bound = scal_ref[n_heads] * (ATT_HD ** 0.5) * jnp.where(is_sample, k_sample, k_prompt)
        small = bound <= ATT_SAFE_EXP

        @pl.when(small)
        def _():
            units = [(kh, parity) for kh in range(n_kv_heads) for parity in range(2)]
            ops = [kv_head_operands(kh) for kh in range(n_kv_heads)]
            scores = [lax.dot_general(ops[kh][0], ops[kh][1][parity], NT_DIMS, preferred_element_type=F32)
                      for kh, parity in units]
            probs = []
            for (kh, parity), s_all in zip(units, scores):
                blocks = []
                for p in range(pairs):
                    head = kh * ATT_GROUP + 2 * p + parity
                    blocks.append(jnp.exp(s_all[p * CHUNK:(p + 1) * CHUNK] + alibi_ref[head]).astype(BF16))
                probs.append(jnp.concatenate(blocks, axis=0))
            both = [_dot(pr, jnp.concatenate([ops[kh][2][parity], ones], axis=1))
                    for (kh, parity), pr in zip(units, probs)]
            for kh in range(n_kv_heads):
                for p in range(pairs):
                    blk = slice(p * CHUNK, (p + 1) * CHUNK)
                    out = None
                    for parity in range(2):
                        head = kh * ATT_GROUP + 2 * p + parity
                        b = both[2 * kh + parity]
                        den = b[blk, lanes:] + jnp.exp(jnp.full((1, lanes), scal_ref[head], F32))
                        part = b[blk, :lanes] / den
                        out = part if out is None else out + part
                    o_ref[rows, kh * qw + p * lanes: kh * qw + (p + 1) * lanes] = out.astype(BF16)

        @pl.when(jnp.logical_not(small))
        def _():
            for kh in range(n_kv_heads):
                qs, ks, vs = kv_head_operands(kh)
                acc = None
                for parity in range(2):
                    s_all = lax.dot_general(qs, ks[parity], NT_DIMS, preferred_element_type=F32)
                    probs, inv = [], []
                    for p in range(pairs):
                        head = kh * ATT_GROUP + 2 * p + parity
                        s = jnp.where(valid, s_all[p * CHUNK:(p + 1) * CHUNK] + alibi_ref[head], -jnp.inf)
                        sink = scal_ref[head]
                        mx = jnp.maximum(jnp.max(s, axis=-1, keepdims=True), sink)
                        e = jnp.exp(s - mx)
                        den = jnp.sum(e, axis=-1, keepdims=True) + jnp.exp(sink - mx)
                        probs.append(e.astype(BF16))
                        inv.append(1.0 / den)
                    part = _dot(jnp.concatenate(probs, axis=0), vs[parity]) * jnp.concatenate(inv, axis=0)
                    acc = part if acc is None else acc + part
                for p in range(pairs):
                    o_ref[rows, kh * qw + p * lanes: kh * qw + (p + 1) * lanes] = (
                        acc[p * CHUNK:(p + 1) * CHUNK].astype(BF16))


def _attention(q, kv, kmax, cache_k, cache_v, scalars, n_prompt_seq, chunks_per_seq):
    T, d_model = q.shape
    rows = PAIR * CHUNK
    n_steps = T // rows
    assert chunks_per_seq % PAIR == 0 and cache_k.shape[0] % PAIR == 0 and PAIR >= ATT_PREV
    steps_per_seq = chunks_per_seq // PAIR
    nps = n_prompt_seq * steps_per_seq
    kvw2 = kv.shape[1]
    n_kv_heads = kvw2 // (2 * ATT_HD)
    n_cache = cache_k.shape[1]
    assert n_cache == ATT_PREV * CHUNK

    def prev_step(i):
        inside = jnp.logical_and(i < nps, lax.rem(i, steps_per_seq) >= 1)
        return (jnp.where(inside, i - 1, i), 0)

    alibi = _alibi_table(n_kv_heads * ATT_GROUP)
    cmax = jnp.max(jnp.abs(cache_k), axis=(1, 2))
    smem = pl.BlockSpec(memory_space=pltpu.SMEM)
    cache_spec = pl.BlockSpec((PAIR, n_cache, kvw2 // 2), lambda i: (jnp.maximum(i - nps, 0), 0, 0))
    kern = functools.partial(_attn_kernel, n_prompt_steps=nps, steps_per_seq=steps_per_seq,
                             n_kv_heads=n_kv_heads)
    return pl.pallas_call(
        kern,
        grid=(n_steps,),
        in_specs=[smem, smem, smem,
                  pl.BlockSpec((rows, d_model), lambda i: (i, 0)),
                  pl.BlockSpec((rows, kvw2), lambda i: (i, 0)),
                  pl.BlockSpec((rows, kvw2), prev_step),
                  cache_spec, cache_spec, _resident(alibi.shape)],
        out_specs=pl.BlockSpec((rows, d_model), lambda i: (i, 0)),
        out_shape=jax.ShapeDtypeStruct((T, d_model), BF16),
        compiler_params=_params("parallel"),
        name="swa_attention",
    )(scalars, kmax, cmax, q, kv, kv, cache_k, cache_v, jnp.asarray(alibi))


def kernel(x_prompt, x_sample, state_hgrn, cache_k, cache_v, p_prompt, p_sample, norm_mix, norm_ffn,
           norm_ple, a_w_in, a_lb_logits, a_g_norm, a_w_o, b_w_qkv, b_q_norm, b_k_norm, b_sinks, b_w_o,
           f_w_gu, f_w_down, ple_w_proj, ple_w_gate):
    B, L, D = x_prompt.shape
    Bs, Ls, _ = x_sample.shape
    assert L % CHUNK == 0 and Ls == CHUNK
    tp, ts = B * L, Bs * Ls
    cps = L // CHUNK
    depth = norm_mix.shape[0]
    assert depth == 2

    xp, xs = x_prompt.reshape(tp, D), x_sample.reshape(ts, D)
    pp, ps = p_prompt.reshape(depth, tp, -1), p_sample.reshape(depth, ts, -1)
    vec = lambda w: w.reshape(1, -1).astype(F32)

    (q, k, v, lf, gs), (w_gu0, w_down0, w_o0, w_gate0, w_proj0) = _hgrn_in(
        xp, xs, vec(norm_mix[0]), a_w_in[0].astype(BF16), a_lb_logits.astype(F32),
        side=[(f_w_gu, 0), (f_w_down, 0), (a_w_o, 0), (ple_w_gate, 0), (ple_w_proj, 0)])

    o, state_p, state_s = _gla(q, k, v, lf, state_hgrn[0].astype(F32), B, cps)
    h1 = _out_proj((o, gs, vec(a_g_norm[0])), w_o0, (xp, xs), gated=True)
    h2, (w_gu1, w_down1, w_qkv, w_o1, w_gate1, w_proj1) = _ffn(
        h1, vec(norm_ffn[0]), w_gu0, w_down0,
        side=[(f_w_gu, 1), (f_w_down, 1), (b_w_qkv, 0), (b_w_o, 0), (ple_w_gate, 1), (ple_w_proj, 1)])
    h = _ple(h2, vec(norm_ple[0]), pp, ps, 0, w_gate0, w_proj0, split_output=False)

    qn, kvn, kmax = _qkv(h, vec(norm_mix[1]), w_qkv, b_q_norm[0], b_k_norm[0])
    n_kv = cache_k.shape[3]
    kvw = n_kv * ATT_HD
    scalars = jnp.concatenate([b_sinks[0].astype(F32), jnp.max(jnp.abs(b_q_norm[0])).reshape(1).astype(F32)])
    att = _attention(qn, kvn, kmax, cache_k[0].reshape(Bs, -1, kvw).astype(F32),
                     cache_v[0].reshape(Bs, -1, kvw).astype(F32), scalars, B, cps)
    h1 = _out_proj((att,), w_o1, (h,), gated=False)
    h2, _ = _ffn(h1, vec(norm_ffn[1]), w_gu1, w_down1)
    yp, ys = _ple(h2, vec(norm_ple[1]), pp, ps, 1, w_gate1, w_proj1, split_output=True)

    keep = min(ATT_PREV * CHUNK, L)
    kv_p = kvn[:tp].reshape(B, L, 2 * kvw)[:, L - keep:]
    kv_s = kvn[tp:].reshape(Bs, Ls, 2 * kvw)
    heads = lambda t: t.reshape(*t.shape[:2], n_kv, ATT_HD)[None]
    return (yp.reshape(B, L, D), ys.reshape(Bs, Ls, D), state_p[None], state_s[None],
            heads(kv_p[..., :kvw]), heads(kv_p[..., kvw:]), heads(kv_s[..., :kvw]), heads(kv_s[..., kvw:]))
```

```python
import functools

import numpy as np
import jax
import jax.numpy as jnp
from jax import lax
from jax.experimental import pallas as pl
from jax.experimental.pallas import tpu as pltpu

F32 = jnp.float32
BF16 = jnp.bfloat16

RMS_EPS = 1e-6
CHUNK = 64
PAIR = 2
HEAD_DK = 128
ATT_HD = 64
ATT_GROUP = 8
ATT_PREV = 2
GLA_LEVELS = (64, 32, 16, 8, 4, 2)
GLA_SAFE_EXP = 60.0
ATT_SAFE_EXP = 70.0

NT_DIMS = (((1,), (1,)), ((), ()))
TN_DIMS = (((0,), (0,)), ((), ()))

VMEM_LIMIT_BYTES = 60 * 1024 * 1024


def _params(*sem):
    return pltpu.CompilerParams(dimension_semantics=sem, vmem_limit_bytes=VMEM_LIMIT_BYTES)


def _tile(n, pref):
    t = pref
    while t > 8 and n % t:
        t //= 2
    assert n % t == 0, (n, pref)
    return t


def _split_rows(tm, n_p, width, sample_buffers=None):
    mode = {} if sample_buffers is None else dict(pipeline_mode=pl.Buffered(sample_buffers))
    return (pl.BlockSpec((tm, width), lambda i, *_: (jnp.minimum(i, n_p - 1), 0)),
            pl.BlockSpec((tm, width), lambda i, *_: (jnp.maximum(i - n_p, 0), 0), **mode))


def _resident(shape):
    return pl.BlockSpec(shape, lambda *_: (0,) * len(shape), pipeline_mode=pl.Buffered(1))


def _row_chunks(rows, size=256):
    size = min(size, rows)
    assert rows % size == 0
    return [slice(r, r + size) for r in range(0, rows, size)]


def _side_cast_specs(side, n_steps, step_of):
    in_specs, out_specs, out_shape = [], [], []
    for arr, idx in side:
        _, rows, cols = arr.shape
        rb = 16
        while rows // rb > n_steps:
            rb *= 2
        assert rows % rb == 0
        last = rows // rb - 1
        in_specs.append(pl.BlockSpec(
            (1, rb, cols), lambda *g, idx=idx, last=last: (idx, jnp.minimum(step_of(*g), last), 0)))
        out_specs.append(pl.BlockSpec((rb, cols), lambda *g, last=last: (jnp.minimum(step_of(*g), last), 0)))
        out_shape.append(jax.ShapeDtypeStruct((rows, cols), BF16))
    return in_specs, out_specs, out_shape


def _side_cast(in_refs, out_refs):
    for src, dst in zip(in_refs, out_refs):
        dst[...] = src[0].astype(BF16)


def _sigmoid(x):
    return 1.0 / (1.0 + jnp.exp(-x))


def _rms_rows(x, w):
    ms = jnp.mean(x * x, axis=-1, keepdims=True)
    return x * lax.rsqrt(ms + RMS_EPS) * w


def _dot(a, b):
    return jnp.dot(a, b, preferred_element_type=F32)


def _hgrn_in_kernel(xp_ref, xs_ref, nw_ref, wq_ref, wf_ref, wi_ref, wg_ref, lbl_ref, *refs, n_p, n_side):
    side_in, refs = refs[:n_side], refs[n_side:]
    q_ref, k_ref, v_ref, lf_ref, gs_ref = refs[:5]
    side_out, (xn_ref,) = refs[5:5 + n_side], refs[5 + n_side:]
    first = pl.program_id(1) == 0
    is_prompt = pl.program_id(0) < n_p

    @pl.when(jnp.logical_and(first, is_prompt))
    def _():
        xn_ref[...] = _rms_rows(xp_ref[...], nw_ref[...]).astype(BF16)

    @pl.when(jnp.logical_and(first, jnp.logical_not(is_prompt)))
    def _():
        xn_ref[...] = _rms_rows(xs_ref[...], nw_ref[...]).astype(BF16)

    xn = xn_ref[...]
    q = _dot(xn, wq_ref[...])
    q_ref[...] = (q * _sigmoid(q)).astype(BF16)
    logits = lbl_ref[...]
    ex = jnp.exp(logits - jnp.max(logits, axis=0, keepdims=True))
    lb = ex[0:1] / jnp.sum(ex, axis=0, keepdims=True)
    f = _dot(xn, wf_ref[...])
    forget = lb + (1.0 - lb) * _sigmoid(f)
    k_ref[...] = (1.0 - forget).astype(BF16)
    lf_ref[...] = jnp.log(forget)
    v_ref[...] = _dot(xn, wi_ref[...]).astype(BF16)
    g = _dot(xn, wg_ref[...])
    gs_ref[...] = (g * _sigmoid(g)).astype(BF16)
    _side_cast(side_in, side_out)


def _hgrn_in(xp, xs, norm_w, w_in, lb_logits, side):
    (tp, D), ts = xp.shape, xs.shape[0]
    T = tp + ts
    tm = min(_tile(tp, 1024), _tile(ts, 1024))
    tn = 256
    nb = D // tn
    out = pl.BlockSpec((tm, tn), lambda i, j: (i, j))

    def wspec(g):
        return pl.BlockSpec((D, tn), lambda i, j, g=g: (0, j + g * nb))

    side_in, side_out, side_shape = _side_cast_specs(side, (T // tm) * nb, lambda i, j: i * nb + j)
    res = pl.pallas_call(
        functools.partial(_hgrn_in_kernel, n_p=tp // tm, n_side=len(side)),
        grid=(T // tm, nb),
        in_specs=[*_split_rows(tm, tp // tm, D, sample_buffers=1), pl.BlockSpec((1, D), lambda i, j: (0, 0)),
                  wspec(0), wspec(1), wspec(2), wspec(3),
                  pl.BlockSpec((lb_logits.shape[0], tn), lambda i, j: (0, j)), *side_in],
        out_specs=[out] * 5 + side_out,
        out_shape=[jax.ShapeDtypeStruct((T, D), BF16)] * 2
        + [jax.ShapeDtypeStruct((T, D), BF16), jax.ShapeDtypeStruct((T, D), F32),
           jax.ShapeDtypeStruct((T, D), BF16)] + side_shape,
        scratch_shapes=[pltpu.VMEM((tm, D), BF16)],
        compiler_params=_params("arbitrary", "arbitrary"),
        name="hgrn_in",
    )(xp, xs, norm_w, w_in, w_in, w_in, w_in, lb_logits, *[a for a, _ in side])
    return res[:5], res[5:]


def _gla_constants():
    c = CHUNK
    t = np.arange(c)[:, None]
    s = np.arange(c)[None, :]
    zs = [s <= t, s > t]
    level = np.full((c, c), -1, np.int32)
    for li, p in enumerate(GLA_LEVELS, start=1):
        half = p // 2
        mid = (t // p) * p + half - 1
        upper = (t % p) >= half
        zs.append(np.where(upper, (s > mid) & (s <= t), (s > t) & (s <= mid)))
        level[(t // p == s // p) & upper & ((s % p) < half)] = li
    np.fill_diagonal(level, 0)
    z = np.concatenate(zs, axis=0).astype(np.float32)
    zf = np.concatenate([zs[0], zs[0]], axis=0).astype(np.float32)
    zf[c:] *= -1.0
    three = lambda m: np.concatenate([m, m, m], axis=1)
    return three(z), three(zf), level


def _gla_kernel(q_ref, k_ref, v_ref, lf_ref, lfn_ref, s0_ref, z_ref, zf_ref, lvl_ref, o_ref, sp_ref, ss_ref,
                st_ref, e_ref, dec_ref, *, n_prompt_steps, steps_per_seq, n_heads):
    m = pl.program_id(0)
    is_prompt = m < n_prompt_steps
    is_sample = jnp.logical_not(is_prompt)
    c = jnp.where(is_prompt, lax.rem(m, steps_per_seq), 0)
    chunk_rows = [slice(t * CHUNK, (t + 1) * CHUNK) for t in range(PAIR)]
    sls = [slice(h * HEAD_DK, (h + 1) * HEAD_DK) for h in range(n_heads)]
    lvl = lvl_ref[...]

    def exponents(z, rows):
        g = lf_ref[rows, :]
        g_hi = g.astype(BF16)
        r1 = g - g_hi.astype(F32)
        g_mid = r1.astype(BF16)
        g_lo = (r1 - g_mid.astype(F32)).astype(BF16)
        return _dot(z, jnp.concatenate([g_hi, g_mid, g_lo], axis=0))

    def note_next_decays(ref):
        for t, rows in enumerate(chunk_rows):
            dec_ref[t] = jnp.min(jnp.sum(ref[rows, :], axis=0, keepdims=True))

    def single_reference(chunks):
        for t in chunks:
            e_ref[2 * t * CHUNK:2 * (t + 1) * CHUNK, :] = jnp.exp(exponents(zf_ref[...], chunk_rows[t]))
        eg = lambda t, sl: e_ref[2 * t * CHUNK:(2 * t + 1) * CHUNK, sl]
        en = lambda t, sl: e_ref[(2 * t + 1) * CHUNK:(2 * t + 2) * CHUNK, sl]
        heads = [(t, h) for t in chunks for h in range(n_heads)]
        q_hat = {(t, h): (q_ref[chunk_rows[t], sls[h]].astype(F32) * eg(t, sls[h])).astype(BF16) for t, h in heads}
        k_bar = {(t, h): (k_ref[chunk_rows[t], sls[h]].astype(F32) * en(t, sls[h])).astype(BF16) for t, h in heads}
        a = {th: lax.dot_general(q_hat[th], k_bar[th], NT_DIMS, preferred_element_type=F32) for th in heads}
        av = {(t, h): _dot(jnp.where(lvl >= 0, a[t, h], 0.0).astype(BF16), v_ref[chunk_rows[t], sls[h]])
              for t, h in heads}
        for t in chunks:
            rows = chunk_rows[t]
            o = [lax.dot_general(q_hat[t, h], st_ref[h].astype(BF16), NT_DIMS, preferred_element_type=F32)
                 for h in range(n_heads)]
            for h, sl in enumerate(sls):
                o_ref[rows, sl] = (o[h] + av[t, h]).astype(BF16)
            for h, sl in enumerate(sls):
                decay_all = e_ref[(2 * t + 1) * CHUNK - 1:(2 * t + 1) * CHUNK, sl]
                k_hat = (k_ref[rows, sl].astype(F32) * (en(t, sl) * decay_all)).astype(BF16)
                st_ref[h] = st_ref[h] * decay_all + lax.dot_general(v_ref[rows, sl], k_hat, TN_DIMS,
                                                                     preferred_element_type=F32)

    def binary_splitting(rows):
        e_ref[...] = jnp.exp(exponents(z_ref[...], rows))
        row = lax.broadcasted_iota(jnp.int32, (CHUNK, HEAD_DK), 0)
        for h, sl in enumerate(sls):
            qb = q_ref[rows, sl]
            kb = k_ref[rows, sl]
            vb = v_ref[rows, sl]
            qf = qb.astype(F32)
            kf = kb.astype(F32)
            st = st_ref[h]
            q_hat = (qf * e_ref[0:CHUNK, sl]).astype(BF16)
            k_hat = (kf * e_ref[CHUNK:2 * CHUNK, sl]).astype(BF16)
            o = lax.dot_general(q_hat, st.astype(BF16), NT_DIMS, preferred_element_type=F32)
            a = lax.dot_general(qb, kb, NT_DIMS, preferred_element_type=F32)
            a = jnp.where(lvl == 0, a, 0.0)
            for li, p in enumerate(GLA_LEVELS, start=1):
                on_q_side = (row & (p // 2)) != 0
                x = (jnp.where(on_q_side, qf, kf)
                     * e_ref[(li + 1) * CHUNK:(li + 2) * CHUNK, sl]).astype(BF16)
                a = jnp.where(lvl == li, lax.dot_general(x, x, NT_DIMS, preferred_element_type=F32), a)
            o_ref[rows, sl] = (o + _dot(a.astype(BF16), vb)).astype(BF16)
            decay_all = e_ref[CHUNK - 1:CHUNK, sl]
            st_ref[h] = st * decay_all + lax.dot_general(vb, k_hat, TN_DIMS, preferred_element_type=F32)

    @pl.when(m == 0)
    def _():
        note_next_decays(lf_ref)

    mild = [dec_ref[t] >= -GLA_SAFE_EXP for t in range(PAIR)]

    @pl.when(jnp.logical_and(is_prompt, c == 0))
    def _():
        st_ref[...] = jnp.zeros_like(st_ref)

    together = functools.reduce(jnp.logical_and, mild, is_prompt)

    @pl.when(together)
    def _():
        single_reference(range(PAIR))
        note_next_decays(lfn_ref)

    @pl.when(jnp.logical_not(together))
    def _():
        for t, rows in enumerate(chunk_rows):
            last = t == PAIR - 1

            @pl.when(is_sample)
            def _():
                for h in range(n_heads):
                    st_ref[h] = s0_ref[t, h].T

            @pl.when(mild[t])
            def _():
                single_reference([t])
                if last:
                    note_next_decays(lfn_ref)

            @pl.when(jnp.logical_not(mild[t]))
            def _():
                binary_splitting(rows)
                if last:
                    note_next_decays(lfn_ref)

            @pl.when(is_sample)
            def _():
                for h in range(n_heads):
                    ss_ref[t, h] = st_ref[h].T

    @pl.when(jnp.logical_and(is_prompt, c == steps_per_seq - 1))
    def _():
        for h in range(n_heads):
            sp_ref[0, h] = st_ref[h].T


def _gla(q, k, v, lf, s0, n_prompt_seq, chunks_per_seq):
    T, D = q.shape
    n_heads = D // HEAD_DK
    rows = PAIR * CHUNK
    n_steps = T // rows
    assert chunks_per_seq % PAIR == 0 and s0.shape[0] % PAIR == 0
    steps_per_seq = chunks_per_seq // PAIR
    nps = n_prompt_seq * steps_per_seq
    zcat, zfast, level = _gla_constants()
    blk = pl.BlockSpec((rows, D), lambda i: (i, 0))
    h_blk = (n_heads, HEAD_DK, HEAD_DK)

    def prompt_seq(i):
        return (jnp.minimum(i // steps_per_seq, n_prompt_seq - 1), 0, 0, 0)

    def sample_pair(i):
        return (jnp.maximum(i - nps, 0), 0, 0, 0)

    kern = functools.partial(_gla_kernel, n_prompt_steps=nps, steps_per_seq=steps_per_seq, n_heads=n_heads)
    return pl.pallas_call(
        kern,
        grid=(n_steps,),
        in_specs=[blk, blk, blk, blk,
                  pl.BlockSpec((rows, D), lambda i: (jnp.minimum(i + 1, n_steps - 1), 0)),
                  pl.BlockSpec((PAIR, *h_blk), sample_pair),
                  pl.BlockSpec(zcat.shape, lambda i: (0, 0)),
                  pl.BlockSpec(zfast.shape, lambda i: (0, 0)),
                  pl.BlockSpec(level.shape, lambda i: (0, 0))],
        out_specs=[blk, pl.BlockSpec((1, *h_blk), prompt_seq), pl.BlockSpec((PAIR, *h_blk), sample_pair)],
        out_shape=[jax.ShapeDtypeStruct((T, D), BF16),
                   jax.ShapeDtypeStruct((n_prompt_seq, *h_blk), F32),
                   jax.ShapeDtypeStruct((s0.shape[0], *h_blk), F32)],
        scratch_shapes=[pltpu.VMEM(h_blk, F32),
                        pltpu.VMEM((zcat.shape[0], D), F32),
                        pltpu.SMEM((PAIR,), F32)],
        compiler_params=_params("arbitrary"),
        name="gla_scan",
    )(q, k, v, lf, lf, s0, jnp.asarray(zcat, BF16), jnp.asarray(zfast, BF16), jnp.asarray(level))


def _out_proj_kernel(*refs, gated, n_p):
    if gated:
        o_ref, gs_ref, gn_ref, w_ref, *h_refs, h1_ref = refs
    else:
        a_ref, w_ref, *h_refs, h1_ref = refs
    for r in _row_chunks(h1_ref.shape[0]):
        if gated:
            a = (_rms_rows(o_ref[r, :].astype(F32), gn_ref[...]) * gs_ref[r, :].astype(F32)).astype(BF16)
        else:
            a = a_ref[r, :]
        if len(h_refs) == 2:
            h = jnp.where(pl.program_id(0) < n_p, h_refs[0][r, :], h_refs[1][r, :])
        else:
            h = h_refs[0][r, :]
        h1_ref[r, :] = h + _dot(a, w_ref[...])


def _out_proj(a_inputs, w, hs, gated):
    T, D = a_inputs[0].shape
    tm = min(_tile(h.shape[0], 512) for h in hs)
    row = pl.BlockSpec((tm, D), lambda i: (i, 0))
    vec = pl.BlockSpec((1, D), lambda i: (0, 0))
    wspec = _resident((D, D))
    n_p = hs[0].shape[0] // tm
    h_specs = [row] if len(hs) == 1 else list(_split_rows(tm, n_p, D))
    if gated:
        in_specs = [row, row, vec, wspec, *h_specs]
    else:
        in_specs = [row, wspec, *h_specs]
    return pl.pallas_call(
        functools.partial(_out_proj_kernel, gated=gated, n_p=n_p),
        grid=(T // tm,),
        in_specs=in_specs,
        out_specs=row,
        out_shape=jax.ShapeDtypeStruct((T, D), F32),
        compiler_params=_params("parallel"),
        name="out_proj",
    )(*a_inputs, w, *hs)


def _ffn_kernel(h_ref, nf_ref, wg_ref, wu_ref, wd_ref, *refs, n_side):
    side_in, h2_ref = refs[:n_side], refs[n_side]
    side_out, (xn_ref,) = refs[n_side + 1:2 * n_side + 1], refs[2 * n_side + 1:]

    @pl.when(pl.program_id(1) == 0)
    def _():
        h = h_ref[...]
        xn_ref[...] = _rms_rows(h, nf_ref[...]).astype(BF16)
        h2_ref[...] = h

    xn = xn_ref[...]
    gate = _dot(xn, wg_ref[...])
    up = _dot(xn, wu_ref[...])
    h2_ref[...] += _dot((gate * _sigmoid(gate) * up).astype(BF16), wd_ref[...])
    _side_cast(side_in, side_out)


def _ffn(h, norm_ffn, w_gu, w_down, side=()):
    T, D = h.shape
    ff = w_down.shape[0]
    tm = _tile(T, 1024)
    tf = 512
    nf = ff // tf
    row = pl.BlockSpec((tm, D), lambda i, j: (i, 0))
    side_in, side_out, side_shape = _side_cast_specs(side, (T // tm) * nf, lambda i, j: i * nf + j)
    res = pl.pallas_call(
        functools.partial(_ffn_kernel, n_side=len(side)),
        grid=(T // tm, nf),
        in_specs=[row, pl.BlockSpec((1, D), lambda i, j: (0, 0)),
                  pl.BlockSpec((D, tf), lambda i, j: (0, j)),
                  pl.BlockSpec((D, tf), lambda i, j: (0, j + nf)),
                  pl.BlockSpec((tf, D), lambda i, j: (j, 0)), *side_in],
        out_specs=[row] + side_out,
        out_shape=[jax.ShapeDtypeStruct((T, D), F32)] + side_shape,
        scratch_shapes=[pltpu.VMEM((tm, D), BF16)],
        compiler_params=_params("arbitrary", "arbitrary"),
        name="ffn",
    )(h, norm_ffn, w_gu, w_gu, w_down, *[a for a, _ in side])
    return res[0], res[1:]


def _ple_kernel(h_ref, np_ref, *refs, n_p):
    *p_refs, wg_ref, wp_ref, out_ref = refs
    for r in _row_chunks(h_ref.shape[0]):
        h = h_ref[r, :]
        gate = _sigmoid(_dot(_rms_rows(h, np_ref[...]).astype(BF16), wg_ref[...]))
        if len(p_refs) == 2:
            p = jnp.where(pl.program_id(0) < n_p, p_refs[0][0, r, :], p_refs[1][0, r, :])
        else:
            p = p_refs[0][0, r, :]
        out_ref[r, :] = h + _dot(p.astype(BF16), wp_ref[...]) * gate


def _ple_call(h, norm_ple, ps, layer, w_gate, w_proj, tm, first_block, n_blocks, n_p):
    D = h.shape[1]
    pd = ps[0].shape[2]
    if len(ps) == 2:
        p_specs = [pl.BlockSpec((1, tm, pd), lambda i: (layer, jnp.minimum(i, n_p - 1), 0)),
                   pl.BlockSpec((1, tm, pd), lambda i: (layer, jnp.maximum(i - n_p, 0), 0))]
    else:
        p_specs = [pl.BlockSpec((1, tm, pd), lambda i: (layer, i, 0))]
    return pl.pallas_call(
        functools.partial(_ple_kernel, n_p=n_p),
        grid=(n_blocks,),
        in_specs=[pl.BlockSpec((tm, D), lambda i: (i + first_block, 0)),
                  pl.BlockSpec((1, D), lambda i: (0, 0)),
                  *p_specs, _resident((D, D)), _resident((pd, D))],
        out_specs=pl.BlockSpec((tm, D), lambda i: (i, 0)),
        out_shape=jax.ShapeDtypeStruct((n_blocks * tm, D), F32),
        compiler_params=_params("parallel"),
        name="ple",
    )(h, norm_ple, *ps, w_gate, w_proj)


def _ple(h, norm_ple, p_prompt, p_sample, layer, w_gate, w_proj, split_output):
    tp, ts = p_prompt.shape[1], p_sample.shape[1]
    tm = min(_tile(tp, 512), _tile(ts, 512))
    n_p, n_s = tp // tm, ts // tm
    args = (h, norm_ple)
    if split_output:
        return (_ple_call(*args, (p_prompt,), layer, w_gate, w_proj, tm, 0, n_p, n_p),
                _ple_call(*args, (p_sample,), layer, w_gate, w_proj, tm, n_p, n_s, n_p))
    return _ple_call(*args, (p_prompt, p_sample), layer, w_gate, w_proj, tm, 0, n_p + n_s, n_p)


def _qkv_kernel(h_ref, nm_ref, w_ref, seg_ref, gain_ref, q_ref, kv_ref, kmax_ref, xn_ref, *, k_width):
    j = pl.program_id(1)

    @pl.when(j == 0)
    def _():
        xn_ref[...] = _rms_rows(h_ref[...], nm_ref[...]).astype(BF16)

    x = _dot(xn_ref[...], w_ref[...])
    ms = _dot((x * x).astype(BF16), seg_ref[...]) * (1.0 / ATT_HD)
    normed = x * lax.rsqrt(ms + RMS_EPS) * gain_ref[0]
    q_ref[...] = normed.astype(BF16)

    @pl.when(j == 0)
    def _():
        is_k = lax.broadcasted_iota(jnp.int32, x.shape, 1) < k_width
        kv_ref[...] = jnp.where(is_k, normed, x)
        n_ch = x.shape[0] // CHUNK
        peak = jnp.max(jnp.where(is_k, jnp.abs(normed), 0.0).reshape(n_ch, CHUNK, x.shape[1]), axis=1)
        kmax_ref[0] = jnp.broadcast_to(jnp.max(peak, axis=1, keepdims=True), (n_ch, 128))


def _qkv(h, norm_mix, w_qkv, q_gain, k_gain):
    T, D = h.shape
    n_out = w_qkv.shape[1]
    tn = ATT_GROUP * ATT_HD
    n_q_tiles = D // tn
    assert n_out == D + tn
    kv_w = (n_out - D) // 2
    tm = _tile(T, 1024)
    n_ch = tm // CHUNK
    seg = np.kron(np.eye(tn // ATT_HD, dtype=np.float32), np.ones((ATT_HD, ATT_HD), np.float32))
    q_row = jnp.tile(q_gain.astype(F32), tn // ATT_HD) * (ATT_HD ** -0.5)
    kv_row = jnp.concatenate([jnp.tile(k_gain.astype(F32), kv_w // ATT_HD), jnp.ones((tn - kv_w,), F32)])
    gain = jnp.stack([kv_row] + [q_row] * n_q_tiles)[:, None, :]
    q, kv, kmax = pl.pallas_call(
        functools.partial(_qkv_kernel, k_width=kv_w),
        grid=(T // tm, n_q_tiles + 1),
        in_specs=[pl.BlockSpec((tm, D), lambda i, j: (i, 0)),
                  pl.BlockSpec((1, D), lambda i, j: (0, 0)),
                  pl.BlockSpec((D, tn), lambda i, j: (0, jnp.where(j == 0, n_q_tiles, j - 1))),
                  _resident((tn, tn)),
                  pl.BlockSpec((1, 1, tn), lambda i, j: (j, 0, 0))],
        out_specs=[pl.BlockSpec((tm, tn), lambda i, j: (i, jnp.maximum(j - 1, 0))),
                   pl.BlockSpec((tm, tn), lambda i, j: (i, 0)),
                   pl.BlockSpec((1, n_ch, 128), lambda i, j: (i, 0, 0))],
        out_shape=[jax.ShapeDtypeStruct((T, D), BF16), jax.ShapeDtypeStruct((T, tn), F32),
                   jax.ShapeDtypeStruct((T // tm, n_ch, 128), F32)],
        scratch_shapes=[pltpu.VMEM((tm, D), BF16)],
        compiler_params=_params("parallel", "arbitrary"),
        name="qkv_proj",
    )(h, norm_mix, w_qkv, jnp.asarray(seg, BF16), gain)
    return q, kv, kmax[:, :, 0].reshape(-1)


def _alibi_table(n_heads):
    slopes = np.float32(2.0) ** (np.float32(-8.0) * np.arange(1, n_heads + 1, dtype=np.float32) / np.float32(n_heads))
    qi = np.arange(CHUNK)[:, None]
    kj = np.arange((ATT_PREV + 1) * CHUNK)[None, :]
    dist = np.abs(ATT_PREV * CHUNK + qi - kj).astype(np.float32)
    return (-slopes[:, None, None] * dist).astype(np.float32)


def _attn_kernel(scal_ref, kmax_ref, cmax_ref, q_ref, kvo_ref, kvp_ref, ck_ref, cv_ref, alibi_ref, o_ref,
                 *, n_prompt_steps, steps_per_seq, n_kv_heads):
    m = pl.program_id(0)
    is_sample = m >= n_prompt_steps
    no_prev = jnp.logical_and(jnp.logical_not(is_sample), lax.rem(m, steps_per_seq) == 0)
    kvw = n_kv_heads * ATT_HD
    n_keys = (ATT_PREV + 1) * CHUNK
    n_heads = n_kv_heads * ATT_GROUP
    pairs = ATT_GROUP // 2
    qw = ATT_GROUP * ATT_HD
    lanes = 2 * ATT_HD

    kvo = kvo_ref[...]
    kvp = kvp_ref[...]
    low = lax.broadcasted_iota(jnp.int32, (n_keys, lanes), 1) < ATT_HD
    key_row = lax.broadcasted_iota(jnp.int32, (n_keys, kvw), 0)
    kj = lax.broadcasted_iota(jnp.int32, (CHUNK, n_keys), 1)

    def split_halves(t, odd):
        rolled = pltpu.roll(t, ATT_HD, 1)
        in_low, in_high = (rolled, t) if odd else (t, rolled)
        zero = jnp.zeros_like(t)
        return jnp.where(low, in_low, zero).astype(BF16), jnp.where(low, zero, in_high).astype(BF16)

    for t in range(PAIR):
        rows = slice(t * CHUNK, (t + 1) * CHUNK)
        i = m * PAIR + t
        before = jnp.concatenate([kvp, kvo], axis=0)[t * CHUNK:(t + ATT_PREV) * CHUNK]
        n_missing = jnp.where(no_prev, ATT_PREV - t, 0)
        first_valid = n_missing * CHUNK
        k_prev = jnp.where(is_sample, ck_ref[t], before[:, :kvw])
        v_prev = jnp.where(is_sample, cv_ref[t], before[:, kvw:])
        k_all = jnp.concatenate([k_prev, kvo[rows, :kvw]], axis=0)
        key_ok = key_row >= first_valid
        v_all = jnp.where(key_ok, jnp.concatenate([v_prev, kvo[rows, kvw:]], axis=0), 0.0)
        ones = jnp.where(key_ok[:, :lanes], 1.0, 0.0).astype(BF16)
        valid = kj >= first_valid

        def kv_head_operands(kh):
            col = (kh // 2) * lanes
            ks = split_halves(k_all[:, col:col + lanes], kh % 2)
            vs = split_halves(v_all[:, col:col + lanes], kh % 2)
            qs = jnp.concatenate([q_ref[rows, kh * qw + p * lanes: kh * qw + (p + 1) * lanes]
                                  for p in range(pairs)], axis=0)
            return qs, ks, vs

        k_prompt = jnp.maximum(kmax_ref[i], jnp.maximum(kmax_ref[i - jnp.minimum(1, ATT_PREV - n_missing)],
                                                        kmax_ref[i - (ATT_PREV - n_missing)]))
        k_sample = jnp.maximum(kmax_ref[i], cmax_ref[jnp.maximum(m - n_prompt_steps, 0) * PAIR + t])
        bound = scal_ref[n_heads] * (ATT_HD ** 0.5) * jnp.where(is_sample, k_sample, k_prompt)
        small = bound <= ATT_SAFE_EXP

        @pl.when(small)
        def _():
            units = [(kh, parity) for kh in range(n_kv_heads) for parity in range(2)]
            ops = [kv_head_operands(kh) for kh in range(n_kv_heads)]
            scores = [lax.dot_general(ops[kh][0], ops[kh][1][parity], NT_DIMS, preferred_element_type=F32)
                      for kh, parity in units]
            probs = []
            for (kh, parity), s_all in zip(units, scores):
                blocks = []
                for p in range(pairs):
                    head = kh * ATT_GROUP + 2 * p + parity
                    blocks.append(jnp.exp(s_all[p * CHUNK:(p + 1) * CHUNK] + alibi_ref[head]).astype(BF16))
                probs.append(jnp.concatenate(blocks, axis=0))
            both = [_dot(pr, jnp.concatenate([ops[kh][2][parity], ones], axis=1))
                    for (kh, parity), pr in zip(units, probs)]
            for kh in range(n_kv_heads):
                for p in range(pairs):
                    blk = slice(p * CHUNK, (p + 1) * CHUNK)
                    out = None
                    for parity in range(2):
                        head = kh * ATT_GROUP + 2 * p + parity
                        b = both[2 * kh + parity]
                        den = b[blk, lanes:] + jnp.exp(jnp.full((1, lanes), scal_ref[head], F32))
                        part = b[blk, :lanes] / den
                        out = part if out is None else out + part
                    o_ref[rows, kh * qw + p * lanes: kh * qw + (p + 1) * lanes] = out.astype(BF16)

        @pl.when(jnp.logical_not(small))
        def _():
            for kh in range(n_kv_heads):
                qs, ks, vs = kv_head_operands(kh)
                acc = None
                for parity in range(2):
                    s_all = lax.dot_general(qs, ks[parity], NT_DIMS, preferred_element_type=F32)
                    probs, inv = [], []
                    for p in range(pairs):
                        head = kh * ATT_GROUP + 2 * p + parity
                        s = jnp.where(valid, s_all[p * CHUNK:(p + 1) * CHUNK] + alibi_ref[head], -jnp.inf)
                        sink = scal_ref[head]
                        mx = jnp.maximum(jnp.max(s, axis=-1, keepdims=True), sink)
                        e = jnp.exp(s - mx)
                        den = jnp.sum(e, axis=-1, keepdims=True) + jnp.exp(sink - mx)
                        probs.append(e.astype(BF16))
                        inv.append(1.0 / den)
                    part = _dot(jnp.concatenate(probs, axis=0), vs[parity]) * jnp.concatenate(inv, axis=0)
                    acc = part if acc is None else acc + part
                for p in range(pairs):
                    o_ref[rows, kh * qw + p * lanes: kh * qw + (p + 1) * lanes] = (
                        acc[p * CHUNK:(p + 1) * CHUNK].astype(BF16))


def _attention(q, kv, kmax, cache_k, cache_v, scalars, n_prompt_seq, chunks_per_seq):
    T, d_model = q.shape
    rows = PAIR * CHUNK
    n_steps = T // rows
    assert chunks_per_seq % PAIR == 0 and cache_k.shape[0] % PAIR == 0 and PAIR == ATT_PREV
    steps_per_seq = chunks_per_seq // PAIR
    nps = n_prompt_seq * steps_per_seq
    kvw2 = kv.shape[1]
    n_kv_heads = kvw2 // (2 * ATT_HD)
    n_cache = cache_k.shape[1]
    assert n_cache == ATT_PREV * CHUNK

    def prev_step(i):
        inside = jnp.logical_and(i < nps, lax.rem(i, steps_per_seq) >= 1)
        return (jnp.where(inside, i - 1, i), 0)

    alibi = _alibi_table(n_kv_heads * ATT_GROUP)
    cmax = jnp.max(jnp.abs(cache_k), axis=(1, 2))
    smem = pl.BlockSpec(memory_space=pltpu.SMEM)
    cache_spec = pl.BlockSpec((PAIR, n_cache, kvw2 // 2), lambda i: (jnp.maximum(i - nps, 0), 0, 0))
    kern = functools.partial(_attn_kernel, n_prompt_steps=nps, steps_per_seq=steps_per_seq,
                             n_kv_heads=n_kv_heads)
    return pl.pallas_call(
        kern,
        grid=(n_steps,),
        in_specs=[smem, smem, smem,
                  pl.BlockSpec((rows, d_model), lambda i: (i, 0)),
                  pl.BlockSpec((rows, kvw2), lambda i: (i, 0)),
                  pl.BlockSpec((rows, kvw2), prev_step),
                  cache_spec, cache_spec, _resident(alibi.shape)],
        out_specs=pl.BlockSpec((rows, d_model), lambda i: (i, 0)),
        out_shape=jax.ShapeDtypeStruct((T, d_model), BF16),
        compiler_params=_params("parallel"),
        name="swa_attention",
    )(scalars, kmax, cmax, q, kv, kv, cache_k, cache_v, jnp.asarray(alibi))


def kernel(x_prompt, x_sample, state_hgrn, cache_k, cache_v, p_prompt, p_sample, norm_mix, norm_ffn,
           norm_ple, a_w_in, a_lb_logits, a_g_norm, a_w_o, b_w_qkv, b_q_norm, b_k_norm, b_sinks, b_w_o,
           f_w_gu, f_w_down, ple_w_proj, ple_w_gate):
    B, L, D = x_prompt.shape
    Bs, Ls, _ = x_sample.shape
    assert L % CHUNK == 0 and Ls == CHUNK
    tp, ts = B * L, Bs * Ls
    cps = L // CHUNK
    depth = norm_mix.shape[0]
    assert depth == 2

    xp, xs = x_prompt.reshape(tp, D), x_sample.reshape(ts, D)
    pp, ps = p_prompt.reshape(depth, tp, -1), p_sample.reshape(depth, ts, -1)
    vec = lambda w: w.reshape(1, -1).astype(F32)

    (q, k, v, lf, gs), (w_gu0, w_down0, w_o0, w_gate0, w_proj0) = _hgrn_in(
        xp, xs, vec(norm_mix[0]), a_w_in[0].astype(BF16), a_lb_logits.astype(F32),
        side=[(f_w_gu, 0), (f_w_down, 0), (a_w_o, 0), (ple_w_gate, 0), (ple_w_proj, 0)])

    o, state_p, state_s = _gla(q, k, v, lf, state_hgrn[0].astype(F32), B, cps)
    h1 = _out_proj((o, gs, vec(a_g_norm[0])), w_o0, (xp, xs), gated=True)
    h2, (w_gu1, w_down1, w_qkv, w_o1, w_gate1, w_proj1) = _ffn(
        h1, vec(norm_ffn[0]), w_gu0, w_down0,
        side=[(f_w_gu, 1), (f_w_down, 1), (b_w_qkv, 0), (b_w_o, 0), (ple_w_gate, 1), (ple_w_proj, 1)])
    h = _ple(h2, vec(norm_ple[0]), pp, ps, 0, w_gate0, w_proj0, split_output=False)

    qn, kvn, kmax = _qkv(h, vec(norm_mix[1]), w_qkv, b_q_norm[0], b_k_norm[0])
    n_kv = cache_k.shape[3]
    kvw = n_kv * ATT_HD
    scalars = jnp.concatenate([b_sinks[0].astype(F32), jnp.max(jnp.abs(b_q_norm[0])).reshape(1).astype(F32)])
    att = _attention(qn, kvn, kmax, cache_k[0].reshape(Bs, -1, kvw).astype(F32),
                     cache_v[0].reshape(Bs, -1, kvw).astype(F32), scalars, B, cps)
    h1 = _out_proj((att,), w_o1, (h,), gated=False)
    h2, _ = _ffn(h1, vec(norm_ffn[1]), w_gu1, w_down1)
    yp, ys = _ple(h2, vec(norm_ple[1]), pp, ps, 1, w_gate1, w_proj1, split_output=True)

    keep = min(ATT_PREV * CHUNK, L)
    kv_p = kvn[:tp].reshape(B, L, 2 * kvw)[:, L - keep:]
    kv_s = kvn[tp:].reshape(Bs, Ls, 2 * kvw)
    heads = lambda t: t.reshape(*t.shape[:2], n_kv, ATT_HD)[None]
    return (yp.reshape(B, L, D), ys.reshape(Bs, Ls, D), state_p[None], state_s[None],
            heads(kv_p[..., :kvw]), heads(kv_p[..., kvw:]), heads(kv_s[..., :kvw]), heads(kv_s[..., kvw:]))
```

```python
import functools

import numpy as np
import jax
import jax.numpy as jnp
from jax import lax
from jax.experimental import pallas as pl
from jax.experimental.pallas import tpu as pltpu

F32 = jnp.float32
BF16 = jnp.bfloat16

RMS_EPS = 1e-6
CHUNK = 64
PAIR = 2
HEAD_DK = 128
ATT_HD = 64
ATT_GROUP = 8
ATT_PREV = 2
GLA_LEVELS = (64, 32, 16, 8, 4, 2)
GLA_SAFE_EXP = 60.0
ATT_SAFE_EXP = 70.0

NT_DIMS = (((1,), (1,)), ((), ()))
TN_DIMS = (((0,), (0,)), ((), ()))

VMEM_LIMIT_BYTES = 60 * 1024 * 1024


def _params(*sem):
    return pltpu.CompilerParams(dimension_semantics=sem, vmem_limit_bytes=VMEM_LIMIT_BYTES)


def _tile(n, pref):
    t = pref
    while t > 8 and n % t:
        t //= 2
    assert n % t == 0, (n, pref)
    return t


def _split_rows(tm, n_p, width, sample_buffers=None):
    mode = {} if sample_buffers is None else dict(pipeline_mode=pl.Buffered(sample_buffers))
    return (pl.BlockSpec((tm, width), lambda i, *_: (jnp.minimum(i, n_p - 1), 0)),
            pl.BlockSpec((tm, width), lambda i, *_: (jnp.maximum(i - n_p, 0), 0), **mode))


def _resident(shape):
    return pl.BlockSpec(shape, lambda *_: (0,) * len(shape), pipeline_mode=pl.Buffered(1))


def _row_chunks(rows, size=256):
    size = min(size, rows)
    assert rows % size == 0
    return [slice(r, r + size) for r in range(0, rows, size)]


def _side_cast_specs(side, n_steps, step_of):
    in_specs, out_specs, out_shape = [], [], []
    for arr, idx in side:
        _, rows, cols = arr.shape
        rb = 16
        while rows // rb > n_steps:
            rb *= 2
        assert rows % rb == 0
        last = rows // rb - 1
        in_specs.append(pl.BlockSpec(
            (1, rb, cols), lambda *g, idx=idx, last=last: (idx, jnp.minimum(step_of(*g), last), 0)))
        out_specs.append(pl.BlockSpec((rb, cols), lambda *g, last=last: (jnp.minimum(step_of(*g), last), 0)))
        out_shape.append(jax.ShapeDtypeStruct((rows, cols), BF16))
    return in_specs, out_specs, out_shape


def _side_cast(in_refs, out_refs):
    for src, dst in zip(in_refs, out_refs):
        dst[...] = src[0].astype(BF16)


def _sigmoid(x):
    return 1.0 / (1.0 + jnp.exp(-x))


def _rms_rows(x, w):
    ms = jnp.mean(x * x, axis=-1, keepdims=True)
    return x * lax.rsqrt(ms + RMS_EPS) * w


def _dot(a, b):
    return jnp.dot(a, b, preferred_element_type=F32)


def _hgrn_in_kernel(xp_ref, xs_ref, nw_ref, wq_ref, wf_ref, wi_ref, wg_ref, lbl_ref, *refs, n_p, n_side):
    side_in, refs = refs[:n_side], refs[n_side:]
    q_ref, k_ref, v_ref, lf_ref, gs_ref = refs[:5]
    side_out, (xn_ref,) = refs[5:5 + n_side], refs[5 + n_side:]
    first = pl.program_id(1) == 0
    is_prompt = pl.program_id(0) < n_p

    @pl.when(jnp.logical_and(first, is_prompt))
    def _():
        xn_ref[...] = _rms_rows(xp_ref[...], nw_ref[...]).astype(BF16)

    @pl.when(jnp.logical_and(first, jnp.logical_not(is_prompt)))
    def _():
        xn_ref[...] = _rms_rows(xs_ref[...], nw_ref[...]).astype(BF16)

    xn = xn_ref[...]
    q = _dot(xn, wq_ref[...])
    q_ref[...] = (q * _sigmoid(q)).astype(BF16)
    logits = lbl_ref[...]
    ex = jnp.exp(logits - jnp.max(logits, axis=0, keepdims=True))
    lb = ex[0:1] / jnp.sum(ex, axis=0, keepdims=True)
    f = _dot(xn, wf_ref[...])
    forget = lb + (1.0 - lb) * _sigmoid(f)
    k_ref[...] = (1.0 - forget).astype(BF16)
    lf_ref[...] = jnp.log(forget)
    v_ref[...] = _dot(xn, wi_ref[...]).astype(BF16)
    g = _dot(xn, wg_ref[...])
    gs_ref[...] = (g * _sigmoid(g)).astype(BF16)
    _side_cast(side_in, side_out)


def _hgrn_in(xp, xs, norm_w, w_in, lb_logits, side):
    (tp, D), ts = xp.shape, xs.shape[0]
    T = tp + ts
    tm = min(_tile(tp, 1024), _tile(ts, 1024))
    tn = 256
    nb = D // tn
    out = pl.BlockSpec((tm, tn), lambda i, j: (i, j))

    def wspec(g):
        return pl.BlockSpec((D, tn), lambda i, j, g=g: (0, j + g * nb))

    side_in, side_out, side_shape = _side_cast_specs(side, (T // tm) * nb, lambda i, j: i * nb + j)
    res = pl.pallas_call(
        functools.partial(_hgrn_in_kernel, n_p=tp // tm, n_side=len(side)),
        grid=(T // tm, nb),
        in_specs=[*_split_rows(tm, tp // tm, D, sample_buffers=1), pl.BlockSpec((1, D), lambda i, j: (0, 0)),
                  wspec(0), wspec(1), wspec(2), wspec(3),
                  pl.BlockSpec((lb_logits.shape[0], tn), lambda i, j: (0, j)), *side_in],
        out_specs=[out] * 5 + side_out,
        out_shape=[jax.ShapeDtypeStruct((T, D), BF16)] * 2
        + [jax.ShapeDtypeStruct((T, D), BF16), jax.ShapeDtypeStruct((T, D), F32),
           jax.ShapeDtypeStruct((T, D), BF16)] + side_shape,
        scratch_shapes=[pltpu.VMEM((tm, D), BF16)],
        compiler_params=_params("arbitrary", "arbitrary"),
        name="hgrn_in",
    )(xp, xs, norm_w, w_in, w_in, w_in, w_in, lb_logits, *[a for a, _ in side])
    return res[:5], res[5:]


def _gla_constants():
    c = CHUNK
    t = np.arange(c)[:, None]
    s = np.arange(c)[None, :]
    zs = [s <= t, s > t]
    level = np.full((c, c), -1, np.int32)
    for li, p in enumerate(GLA_LEVELS, start=1):
        half = p // 2
        mid = (t // p) * p + half - 1
        upper = (t % p) >= half
        zs.append(np.where(upper, (s > mid) & (s <= t), (s > t) & (s <= mid)))
        level[(t // p == s // p) & upper & ((s % p) < half)] = li
    np.fill_diagonal(level, 0)
    z = np.concatenate(zs, axis=0).astype(np.float32)
    zf = np.concatenate([zs[0], zs[0]], axis=0).astype(np.float32)
    zf[c:] *= -1.0
    three = lambda m: np.concatenate([m, m, m], axis=1)
    return three(z), three(zf), level


def _gla_kernel(q_ref, k_ref, v_ref, lf_ref, lfn_ref, s0_ref, z_ref, zf_ref, lvl_ref, o_ref, sp_ref, ss_ref,
                st_ref, e_ref, dec_ref, *, n_prompt_steps, steps_per_seq, n_heads):
    m = pl.program_id(0)
    is_prompt = m < n_prompt_steps
    is_sample = jnp.logical_not(is_prompt)
    c = jnp.where(is_prompt, lax.rem(m, steps_per_seq), 0)
    chunk_rows = [slice(t * CHUNK, (t + 1) * CHUNK) for t in range(PAIR)]
    sls = [slice(h * HEAD_DK, (h + 1) * HEAD_DK) for h in range(n_heads)]
    lvl = lvl_ref[...]

    def exponents(z, rows):
        g = lf_ref[rows, :]
        g_hi = g.astype(BF16)
        r1 = g - g_hi.astype(F32)
        g_mid = r1.astype(BF16)
        g_lo = (r1 - g_mid.astype(F32)).astype(BF16)
        return _dot(z, jnp.concatenate([g_hi, g_mid, g_lo], axis=0))

    def note_next_decays(ref):
        for t, rows in enumerate(chunk_rows):
            dec_ref[t] = jnp.min(jnp.sum(ref[rows, :], axis=0, keepdims=True))

    def single_reference(chunks):
        for t in chunks:
            e_ref[2 * t * CHUNK:2 * (t + 1) * CHUNK, :] = jnp.exp(exponents(zf_ref[...], chunk_rows[t]))
        eg = lambda t, sl: e_ref[2 * t * CHUNK:(2 * t + 1) * CHUNK, sl]
        en = lambda t, sl: e_ref[(2 * t + 1) * CHUNK:(2 * t + 2) * CHUNK, sl]
        heads = [(t, h) for t in chunks for h in range(n_heads)]
        q_hat = {(t, h): (q_ref[chunk_rows[t], sls[h]].astype(F32) * eg(t, sls[h])).astype(BF16) for t, h in heads}
        k_bar = {(t, h): (k_ref[chunk_rows[t], sls[h]].astype(F32) * en(t, sls[h])).astype(BF16) for t, h in heads}
        a = {th: lax.dot_general(q_hat[th], k_bar[th], NT_DIMS, preferred_element_type=F32) for th in heads}
        av = {(t, h): _dot(jnp.where(lvl >= 0, a[t, h], 0.0).astype(BF16), v_ref[chunk_rows[t], sls[h]])
              for t, h in heads}
        for t in chunks:
            rows = chunk_rows[t]
            o = [lax.dot_general(q_hat[t, h], st_ref[h].astype(BF16), NT_DIMS, preferred_element_type=F32)
                 for h in range(n_heads)]
            for h, sl in enumerate(sls):
                o_ref[rows, sl] = (o[h] + av[t, h]).astype(BF16)
            for h, sl in enumerate(sls):
                decay_all = e_ref[(2 * t + 1) * CHUNK - 1:(2 * t + 1) * CHUNK, sl]
                k_hat = (k_ref[rows, sl].astype(F32) * (en(t, sl) * decay_all)).astype(BF16)
                st_ref[h] = st_ref[h] * decay_all + lax.dot_general(v_ref[rows, sl], k_hat, TN_DIMS,
                                                                     preferred_element_type=F32)

    def binary_splitting(rows):
        e_ref[...] = jnp.exp(exponents(z_ref[...], rows))
        row = lax.broadcasted_iota(jnp.int32, (CHUNK, HEAD_DK), 0)
        for h, sl in enumerate(sls):
            qb = q_ref[rows, sl]
            kb = k_ref[rows, sl]
            vb = v_ref[rows, sl]
            qf = qb.astype(F32)
            kf = kb.astype(F32)
            st = st_ref[h]
            q_hat = (qf * e_ref[0:CHUNK, sl]).astype(BF16)
            k_hat = (kf * e_ref[CHUNK:2 * CHUNK, sl]).astype(BF16)
            o = lax.dot_general(q_hat, st.astype(BF16), NT_DIMS, preferred_element_type=F32)
            a = lax.dot_general(qb, kb, NT_DIMS, preferred_element_type=F32)
            a = jnp.where(lvl == 0, a, 0.0)
            for li, p in enumerate(GLA_LEVELS, start=1):
                on_q_side = (row & (p // 2)) != 0
                x = (jnp.where(on_q_side, qf, kf)
                     * e_ref[(li + 1) * CHUNK:(li + 2) * CHUNK, sl]).astype(BF16)
                a = jnp.where(lvl == li, lax.dot_general(x, x, NT_DIMS, preferred_element_type=F32), a)
            o_ref[rows, sl] = (o + _dot(a.astype(BF16), vb)).astype(BF16)
            decay_all = e_ref[CHUNK - 1:CHUNK, sl]
            st_ref[h] = st * decay_all + lax.dot_general(vb, k_hat, TN_DIMS, preferred_element_type=F32)

    @pl.when(m == 0)
    def _():
        note_next_decays(lf_ref)

    mild = [dec_ref[t] >= -GLA_SAFE_EXP for t in range(PAIR)]

    @pl.when(jnp.logical_and(is_prompt, c == 0))
    def _():
        st_ref[...] = jnp.zeros_like(st_ref)

    together = functools.reduce(jnp.logical_and, mild, is_prompt)

    @pl.when(together)
    def _():
        single_reference(range(PAIR))
        note_next_decays(lfn_ref)

    @pl.when(jnp.logical_not(together))
    def _():
        for t, rows in enumerate(chunk_rows):
            last = t == PAIR - 1

            @pl.when(is_sample)
            def _():
                for h in range(n_heads):
                    st_ref[h] = s0_ref[t, h].T

            @pl.when(mild[t])
            def _():
                single_reference([t])
                if last:
                    note_next_decays(lfn_ref)

            @pl.when(jnp.logical_not(mild[t]))
            def _():
                binary_splitting(rows)
                if last:
                    note_next_decays(lfn_ref)

            @pl.when(is_sample)
            def _():
                for h in range(n_heads):
                    ss_ref[t, h] = st_ref[h].T

    @pl.when(jnp.logical_and(is_prompt, c == steps_per_seq - 1))
    def _():
        for h in range(n_heads):
            sp_ref[0, h] = st_ref[h].T


def _gla(q, k, v, lf, s0, n_prompt_seq, chunks_per_seq):
    T, D = q.shape
    n_heads = D // HEAD_DK
    rows = PAIR * CHUNK
    n_steps = T // rows
    assert chunks_per_seq % PAIR == 0 and s0.shape[0] % PAIR == 0
    steps_per_seq = chunks_per_seq // PAIR
    nps = n_prompt_seq * steps_per_seq
    zcat, zfast, level = _gla_constants()
    blk = pl.BlockSpec((rows, D), lambda i: (i, 0))
    h_blk = (n_heads, HEAD_DK, HEAD_DK)

    def prompt_seq(i):
        return (jnp.minimum(i // steps_per_seq, n_prompt_seq - 1), 0, 0, 0)

    def sample_pair(i):
        return (jnp.maximum(i - nps, 0), 0, 0, 0)

    kern = functools.partial(_gla_kernel, n_prompt_steps=nps, steps_per_seq=steps_per_seq, n_heads=n_heads)
    return pl.pallas_call(
        kern,
        grid=(n_steps,),
        in_specs=[blk, blk, blk, blk,
                  pl.BlockSpec((rows, D), lambda i: (jnp.minimum(i + 1, n_steps - 1), 0)),
                  pl.BlockSpec((PAIR, *h_blk), sample_pair),
                  pl.BlockSpec(zcat.shape, lambda i: (0, 0)),
                  pl.BlockSpec(zfast.shape, lambda i: (0, 0)),
                  pl.BlockSpec(level.shape, lambda i: (0, 0))],
        out_specs=[blk, pl.BlockSpec((1, *h_blk), prompt_seq), pl.BlockSpec((PAIR, *h_blk), sample_pair)],
        out_shape=[jax.ShapeDtypeStruct((T, D), BF16),
                   jax.ShapeDtypeStruct((n_prompt_seq, *h_blk), F32),
                   jax.ShapeDtypeStruct((s0.shape[0], *h_blk), F32)],
        scratch_shapes=[pltpu.VMEM(h_blk, F32),
                        pltpu.VMEM((zcat.shape[0], D), F32),
                        pltpu.SMEM((PAIR,), F32)],
        compiler_params=_params("arbitrary"),
        name="gla_scan",
    )(q, k, v, lf, lf, s0, jnp.asarray(zcat, BF16), jnp.asarray(zfast, BF16), jnp.asarray(level))


def _out_proj_kernel(*refs, gated, n_p):
    if gated:
        o_ref, gs_ref, gn_ref, w_ref, *h_refs, h1_ref = refs
    else:
        a_ref, w_ref, *h_refs, h1_ref = refs
    for r in _row_chunks(h1_ref.shape[0]):
        if gated:
            a = (_rms_rows(o_ref[r, :].astype(F32), gn_ref[...]) * gs_ref[r, :].astype(F32)).astype(BF16)
        else:
            a = a_ref[r, :]
        if len(h_refs) == 2:
            h = jnp.where(pl.program_id(0) < n_p, h_refs[0][r, :], h_refs[1][r, :])
        else:
            h = h_refs[0][r, :]
        h1_ref[r, :] = h + _dot(a, w_ref[...])


def _out_proj(a_inputs, w, hs, gated):
    T, D = a_inputs[0].shape
    tm = min(_tile(h.shape[0], 512) for h in hs)
    row = pl.BlockSpec((tm, D), lambda i: (i, 0))
    vec = pl.BlockSpec((1, D), lambda i: (0, 0))
    wspec = _resident((D, D))
    n_p = hs[0].shape[0] // tm
    h_specs = [row] if len(hs) == 1 else list(_split_rows(tm, n_p, D))
    if gated:
        in_specs = [row, row, vec, wspec, *h_specs]
    else:
        in_specs = [row, wspec, *h_specs]
    return pl.pallas_call(
        functools.partial(_out_proj_kernel, gated=gated, n_p=n_p),
        grid=(T // tm,),
        in_specs=in_specs,
        out_specs=row,
        out_shape=jax.ShapeDtypeStruct((T, D), F32),
        compiler_params=_params("parallel"),
        name="out_proj",
    )(*a_inputs, w, *hs)


def _ffn_kernel(h_ref, nf_ref, wg_ref, wu_ref, wd_ref, *refs, n_side):
    side_in, h2_ref = refs[:n_side], refs[n_side]
    side_out, (xn_ref,) = refs[n_side + 1:2 * n_side + 1], refs[2 * n_side + 1:]

    @pl.when(pl.program_id(1) == 0)
    def _():
        h = h_ref[...]
        xn_ref[...] = _rms_rows(h, nf_ref[...]).astype(BF16)
        h2_ref[...] = h

    xn = xn_ref[...]
    gate = _dot(xn, wg_ref[...])
    up = _dot(xn, wu_ref[...])
    h2_ref[...] += _dot((gate * _sigmoid(gate) * up).astype(BF16), wd_ref[...])
    _side_cast(side_in, side_out)


def _ffn(h, norm_ffn, w_gu, w_down, side=()):
    T, D = h.shape
    ff = w_down.shape[0]
    tm = _tile(T, 1024)
    tf = 512
    nf = ff // tf
    row = pl.BlockSpec((tm, D), lambda i, j: (i, 0))
    side_in, side_out, side_shape = _side_cast_specs(side, (T // tm) * nf, lambda i, j: i * nf + j)
    res = pl.pallas_call(
        functools.partial(_ffn_kernel, n_side=len(side)),
        grid=(T // tm, nf),
        in_specs=[row, pl.BlockSpec((1, D), lambda i, j: (0, 0)),
                  pl.BlockSpec((D, tf), lambda i, j: (0, j)),
                  pl.BlockSpec((D, tf), lambda i, j: (0, j + nf)),
                  pl.BlockSpec((tf, D), lambda i, j: (j, 0)), *side_in],
        out_specs=[row] + side_out,
        out_shape=[jax.ShapeDtypeStruct((T, D), F32)] + side_shape,
        scratch_shapes=[pltpu.VMEM((tm, D), BF16)],
        compiler_params=_params("arbitrary", "arbitrary"),
        name="ffn",
    )(h, norm_ffn, w_gu, w_gu, w_down, *[a for a, _ in side])
    return res[0], res[1:]


def _ple_kernel(h_ref, np_ref, *refs, n_p):
    *p_refs, wg_ref, wp_ref, out_ref = refs
    for r in _row_chunks(h_ref.shape[0]):
        h = h_ref[r, :]
        gate = _sigmoid(_dot(_rms_rows(h, np_ref[...]).astype(BF16), wg_ref[...]))
        if len(p_refs) == 2:
            p = jnp.where(pl.program_id(0) < n_p, p_refs[0][0, r, :], p_refs[1][0, r, :])
        else:
            p = p_refs[0][0, r, :]
        out_ref[r, :] = h + _dot(p.astype(BF16), wp_ref[...]) * gate


def _ple_call(h, norm_ple, ps, layer, w_gate, w_proj, tm, first_block, n_blocks, n_p):
    D = h.shape[1]
    pd = ps[0].shape[2]
    if len(ps) == 2:
        p_specs = [pl.BlockSpec((1, tm, pd), lambda i: (layer, jnp.minimum(i, n_p - 1), 0)),
                   pl.BlockSpec((1, tm, pd), lambda i: (layer, jnp.maximum(i - n_p, 0), 0))]
    else:
        p_specs = [pl.BlockSpec((1, tm, pd), lambda i: (layer, i, 0))]
    return pl.pallas_call(
        functools.partial(_ple_kernel, n_p=n_p),
        grid=(n_blocks,),
        in_specs=[pl.BlockSpec((tm, D), lambda i: (i + first_block, 0)),
                  pl.BlockSpec((1, D), lambda i: (0, 0)),
                  *p_specs, _resident((D, D)), _resident((pd, D))],
        out_specs=pl.BlockSpec((tm, D), lambda i: (i, 0)),
        out_shape=jax.ShapeDtypeStruct((n_blocks * tm, D), F32),
        compiler_params=_params("parallel"),
        name="ple",
    )(h, norm_ple, *ps, w_gate, w_proj)


def _ple(h, norm_ple, p_prompt, p_sample, layer, w_gate, w_proj, split_output):
    tp, ts = p_prompt.shape[1], p_sample.shape[1]
    tm = min(_tile(tp, 512), _tile(ts, 512))
    n_p, n_s = tp // tm, ts // tm
    args = (h, norm_ple)
    if split_output:
        return (_ple_call(*args, (p_prompt,), layer, w_gate, w_proj, tm, 0, n_p, n_p),
                _ple_call(*args, (p_sample,), layer, w_gate, w_proj, tm, n_p, n_s, n_p))
    return _ple_call(*args, (p_prompt, p_sample), layer, w_gate, w_proj, tm, 0, n_p + n_s, n_p)


def _qkv_kernel(h_ref, nm_ref, w_ref, seg_ref, gain_ref, q_ref, kv_ref, kmax_ref, xn_ref, *, k_width):
    j = pl.program_id(1)

    @pl.when(j == 0)
    def _():
        xn_ref[...] = _rms_rows(h_ref[...], nm_ref[...]).astype(BF16)

    x = _dot(xn_ref[...], w_ref[...])
    ms = _dot((x * x).astype(BF16), seg_ref[...]) * (1.0 / ATT_HD)
    normed = x * lax.rsqrt(ms + RMS_EPS) * gain_ref[0]
    q_ref[...] = normed.astype(BF16)

    @pl.when(j == 0)
    def _():
        is_k = lax.broadcasted_iota(jnp.int32, x.shape, 1) < k_width
        kv_ref[...] = jnp.where(is_k, normed, x)
        n_ch = x.shape[0] // CHUNK
        peak = jnp.max(jnp.where(is_k, jnp.abs(normed), 0.0).reshape(n_ch, CHUNK, x.shape[1]), axis=1)
        kmax_ref[0] = jnp.broadcast_to(jnp.max(peak, axis=1, keepdims=True), (n_ch, 128))


def _qkv(h, norm_mix, w_qkv, q_gain, k_gain):
    T, D = h.shape
    n_out = w_qkv.shape[1]
    tn = ATT_GROUP * ATT_HD
    n_q_tiles = D // tn
    assert n_out == D + tn
    kv_w = (n_out - D) // 2
    tm = _tile(T, 1024)
    n_ch = tm // CHUNK
    seg = np.kron(np.eye(tn // ATT_HD, dtype=np.float32), np.ones((ATT_HD, ATT_HD), np.float32))
    q_row = jnp.tile(q_gain.astype(F32), tn // ATT_HD) * (ATT_HD ** -0.5)
    kv_row = jnp.concatenate([jnp.tile(k_gain.astype(F32), kv_w // ATT_HD), jnp.ones((tn - kv_w,), F32)])
    gain = jnp.stack([kv_row] + [q_row] * n_q_tiles)[:, None, :]
    q, kv, kmax = pl.pallas_call(
        functools.partial(_qkv_kernel, k_width=kv_w),
        grid=(T // tm, n_q_tiles + 1),
        in_specs=[pl.BlockSpec((tm, D), lambda i, j: (i, 0)),
                  pl.BlockSpec((1, D), lambda i, j: (0, 0)),
                  pl.BlockSpec((D, tn), lambda i, j: (0, jnp.where(j == 0, n_q_tiles, j - 1))),
                  _resident((tn, tn)),
                  pl.BlockSpec((1, 1, tn), lambda i, j: (j, 0, 0))],
        out_specs=[pl.BlockSpec((tm, tn), lambda i, j: (i, jnp.maximum(j - 1, 0))),
                   pl.BlockSpec((tm, tn), lambda i, j: (i, 0)),
                   pl.BlockSpec((1, n_ch, 128), lambda i, j: (i, 0, 0))],
        out_shape=[jax.ShapeDtypeStruct((T, D), BF16), jax.ShapeDtypeStruct((T, tn), F32),
                   jax.ShapeDtypeStruct((T // tm, n_ch, 128), F32)],
        scratch_shapes=[pltpu.VMEM((tm, D), BF16)],
        compiler_params=_params("parallel", "arbitrary"),
        name="qkv_proj",
    )(h, norm_mix, w_qkv, jnp.asarray(seg, BF16), gain)
    return q, kv, kmax[:, :, 0].reshape(-1)


def _alibi_table(n_heads):
    slopes = np.float32(2.0) ** (np.float32(-8.0) * np.arange(1, n_heads + 1, dtype=np.float32) / np.float32(n_heads))
    qi = np.arange(CHUNK)[:, None]
    kj = np.arange((ATT_PREV + 1) * CHUNK)[None, :]
    dist = np.abs(ATT_PREV * CHUNK + qi - kj).astype(np.float32)
    return (-slopes[:, None, None] * dist).astype(np.float32)


def _attn_kernel(scal_ref, kmax_ref, cmax_ref, q_ref, kvo_ref, kvp_ref, ck_ref, cv_ref, alibi_ref, o_ref,
                 *, n_prompt_steps, steps_per_seq, n_kv_heads):
    m = pl.program_id(0)
    is_sample = m >= n_prompt_steps
    no_prev = jnp.logical_and(jnp.logical_not(is_sample), lax.rem(m, steps_per_seq) == 0)
    kvw = n_kv_heads * ATT_HD
    n_keys = (ATT_PREV + 1) * CHUNK
    n_heads = n_kv_heads * ATT_GROUP
    pairs = ATT_GROUP // 2
    qw = ATT_GROUP * ATT_HD
    lanes = 2 * ATT_HD

    kvo = kvo_ref[...]
    kvp = kvp_ref[...]
    low = lax.broadcasted_iota(jnp.int32, (n_keys, lanes), 1) < ATT_HD
    key_row = lax.broadcasted_iota(jnp.int32, (n_keys, kvw), 0)
    kj = lax.broadcasted_iota(jnp.int32, (CHUNK, n_keys), 1)

    def split_halves(t, odd):
        rolled = pltpu.roll(t, ATT_HD, 1)
        in_low, in_high = (rolled, t) if odd else (t, rolled)
        zero = jnp.zeros_like(t)
        return jnp.where(low, in_low, zero).astype(BF16), jnp.where(low, zero, in_high).astype(BF16)

    chunk = []
    for t in range(PAIR):
        rows = slice(t * CHUNK, (t + 1) * CHUNK)
        i = m * PAIR + t
        before = jnp.concatenate([kvp, kvo], axis=0)[t * CHUNK:(t + ATT_PREV) * CHUNK]
        n_missing = jnp.where(no_prev, ATT_PREV - t, 0)
        first_valid = n_missing * CHUNK
        k_prev = jnp.where(is_sample, ck_ref[t], before[:, :kvw])
        v_prev = jnp.where(is_sample, cv_ref[t], before[:, kvw:])
        k_all = jnp.concatenate([k_prev, kvo[rows, :kvw]], axis=0)
        key_ok = key_row >= first_valid
        v_all = jnp.where(key_ok, jnp.concatenate([v_prev, kvo[rows, kvw:]], axis=0), 0.0)
        ones = jnp.where(key_ok[:, :lanes], 1.0, 0.0).astype(BF16)
        k_prompt = jnp.maximum(kmax_ref[i], jnp.maximum(kmax_ref[i - jnp.minimum(1, ATT_PREV - n_missing)],
                                                        kmax_ref[i - (ATT_PREV - n_missing)]))
        k_sample = jnp.maximum(kmax_ref[i], cmax_ref[jnp.maximum(m - n_prompt_steps, 0) * PAIR + t])
        bound = scal_ref[n_heads] * (ATT_HD ** 0.5) * jnp.where(is_sample, k_sample, k_prompt)
        chunk.append(dict(rows=rows, k_all=k_all, v_all=v_all, ones=ones, valid=kj >= first_valid,
                          small=bound <= ATT_SAFE_EXP))

    def kv_head_operands(c, kh):
        col = (kh // 2) * lanes
        ks = split_halves(c["k_all"][:, col:col + lanes], kh % 2)
        vs = split_halves(c["v_all"][:, col:col + lanes], kh % 2)
        qs = jnp.concatenate([q_ref[c["rows"], kh * qw + p * lanes: kh * qw + (p + 1) * lanes]
                              for p in range(pairs)], axis=0)
        return qs, ks, vs

    def without_max(chunks):
        units = [(t, kh, parity) for t in chunks for kh in range(n_kv_heads) for parity in range(2)]
        ops = {(t, kh): kv_head_operands(chunk[t], kh) for t in chunks for kh in range(n_kv_heads)}
        scores = {u: lax.dot_general(ops[u[0], u[1]][0], ops[u[0], u[1]][1][u[2]], NT_DIMS,
                                     preferred_element_type=F32) for u in units}
        probs = {}
        for t, kh, parity in units:
            s_all = scores[t, kh, parity]
            blocks = []
            for p in range(pairs):
                head = kh * ATT_GROUP + 2 * p + parity
                blocks.append(jnp.exp(s_all[p * CHUNK:(p + 1) * CHUNK] + alibi_ref[head]).astype(BF16))
            probs[t, kh, parity] = jnp.concatenate(blocks, axis=0)
        both = {u: _dot(probs[u], jnp.concatenate([ops[u[0], u[1]][2][u[2]], chunk[u[0]]["ones"]], axis=1))
                for u in units}
        for t in chunks:
            for kh in range(n_kv_heads):
                for p in range(pairs):
                    blk = slice(p * CHUNK, (p + 1) * CHUNK)
                    out = None
                    for parity in range(2):
                        head = kh * ATT_GROUP + 2 * p + parity
                        b = both[t, kh, parity]
                        den = b[blk, lanes:] + jnp.exp(jnp.full((1, lanes), scal_ref[head], F32))
                        part = b[blk, :lanes] / den
                        out = part if out is None else out + part
                    o_ref[chunk[t]["rows"], kh * qw + p * lanes: kh * qw + (p + 1) * lanes] = out.astype(BF16)

    def with_max(t):
        c = chunk[t]
        for kh in range(n_kv_heads):
            qs, ks, vs = kv_head_operands(c, kh)
            acc = None
            for parity in range(2):
                s_all = lax.dot_general(qs, ks[parity], NT_DIMS, preferred_element_type=F32)
                probs, inv = [], []
                for p in range(pairs):
                    head = kh * ATT_GROUP + 2 * p + parity
                    s = jnp.where(c["valid"], s_all[p * CHUNK:(p + 1) * CHUNK] + alibi_ref[head], -jnp.inf)
                    sink = scal_ref[head]
                    mx = jnp.maximum(jnp.max(s, axis=-1, keepdims=True), sink)
                    e = jnp.exp(s - mx)
                    den = jnp.sum(e, axis=-1, keepdims=True) + jnp.exp(sink - mx)
                    probs.append(e.astype(BF16))
                    inv.append(1.0 / den)
                part = _dot(jnp.concatenate(probs, axis=0), vs[parity]) * jnp.concatenate(inv, axis=0)
                acc = part if acc is None else acc + part
            for p in range(pairs):
                o_ref[c["rows"], kh * qw + p * lanes: kh * qw + (p + 1) * lanes] = (
                    acc[p * CHUNK:(p + 1) * CHUNK].astype(BF16))

    together = functools.reduce(jnp.logical_and, [c["small"] for c in chunk])

    @pl.when(together)
    def _():
        without_max(range(PAIR))

    @pl.when(jnp.logical_not(together))
    def _():
        for t in range(PAIR):
            @pl.when(chunk[t]["small"])
            def _():
                without_max([t])

            @pl.when(jnp.logical_not(chunk[t]["small"]))
            def _():
                with_max(t)


def _attention(q, kv, kmax, cache_k, cache_v, scalars, n_prompt_seq, chunks_per_seq):
    T, d_model = q.shape
    rows = PAIR * CHUNK
    n_steps = T // rows
    assert chunks_per_seq % PAIR == 0 and cache_k.shape[0] % PAIR == 0 and PAIR == ATT_PREV
    steps_per_seq = chunks_per_seq // PAIR
    nps = n_prompt_seq * steps_per_seq
    kvw2 = kv.shape[1]
    n_kv_heads = kvw2 // (2 * ATT_HD)
    n_cache = cache_k.shape[1]
    assert n_cache == ATT_PREV * CHUNK

    def prev_step(i):
        inside = jnp.logical_and(i < nps, lax.rem(i, steps_per_seq) >= 1)
        return (jnp.where(inside, i - 1, i), 0)

    alibi = _alibi_table(n_kv_heads * ATT_GROUP)
    cmax = jnp.max(jnp.abs(cache_k), axis=(1, 2))
    smem = pl.BlockSpec(memory_space=pltpu.SMEM)
    cache_spec = pl.BlockSpec((PAIR, n_cache, kvw2 // 2), lambda i: (jnp.maximum(i - nps, 0), 0, 0))
    kern = functools.partial(_attn_kernel, n_prompt_steps=nps, steps_per_seq=steps_per_seq,
                             n_kv_heads=n_kv_heads)
    return pl.pallas_call(
        kern,
        grid=(n_steps,),
        in_specs=[smem, smem, smem,
                  pl.BlockSpec((rows, d_model), lambda i: (i, 0)),
                  pl.BlockSpec((rows, kvw2), lambda i: (i, 0)),
                  pl.BlockSpec((rows, kvw2), prev_step),
                  cache_spec, cache_spec, _resident(alibi.shape)],
        out_specs=pl.BlockSpec((rows, d_model), lambda i: (i, 0)),
        out_shape=jax.ShapeDtypeStruct((T, d_model), BF16),
        compiler_params=_params("parallel"),
        name="swa_attention",
    )(scalars, kmax, cmax, q, kv, kv, cache_k, cache_v, jnp.asarray(alibi))


def kernel(x_prompt, x_sample, state_hgrn, cache_k, cache_v, p_prompt, p_sample, norm_mix, norm_ffn,
           norm_ple, a_w_in, a_lb_logits, a_g_norm, a_w_o, b_w_qkv, b_q_norm, b_k_norm, b_sinks, b_w_o,
           f_w_gu, f_w_down, ple_w_proj, ple_w_gate):
    B, L, D = x_prompt.shape
    Bs, Ls, _ = x_sample.shape
    assert L % CHUNK == 0 and Ls == CHUNK
    tp, ts = B * L, Bs * Ls
    cps = L // CHUNK
    depth = norm_mix.shape[0]
    assert depth == 2

    xp, xs = x_prompt.reshape(tp, D), x_sample.reshape(ts, D)
    pp, ps = p_prompt.reshape(depth, tp, -1), p_sample.reshape(depth, ts, -1)
    vec = lambda w: w.reshape(1, -1).astype(F32)

    (q, k, v, lf, gs), (w_gu0, w_down0, w_o0, w_gate0, w_proj0) = _hgrn_in(
        xp, xs, vec(norm_mix[0]), a_w_in[0].astype(BF16), a_lb_logits.astype(F32),
        side=[(f_w_gu, 0), (f_w_down, 0), (a_w_o, 0), (ple_w_gate, 0), (ple_w_proj, 0)])

    o, state_p, state_s = _gla(q, k, v, lf, state_hgrn[0].astype(F32), B, cps)
    h1 = _out_proj((o, gs, vec(a_g_norm[0])), w_o0, (xp, xs), gated=True)
    h2, (w_gu1, w_down1, w_qkv, w_o1, w_gate1, w_proj1) = _ffn(
        h1, vec(norm_ffn[0]), w_gu0, w_down0,
        side=[(f_w_gu, 1), (f_w_down, 1), (b_w_qkv, 0), (b_w_o, 0), (ple_w_gate, 1), (ple_w_proj, 1)])
    h = _ple(h2, vec(norm_ple[0]), pp, ps, 0, w_gate0, w_proj0, split_output=False)

    qn, kvn, kmax = _qkv(h, vec(norm_mix[1]), w_qkv, b_q_norm[0], b_k_norm[0])
    n_kv = cache_k.shape[3]
    kvw = n_kv * ATT_HD
    scalars = jnp.concatenate([b_sinks[0].astype(F32), jnp.max(jnp.abs(b_q_norm[0])).reshape(1).astype(F32)])
    att = _attention(qn, kvn, kmax, cache_k[0].reshape(Bs, -1, kvw).astype(F32),
                     cache_v[0].reshape(Bs, -1, kvw).astype(F32), scalars, B, cps)
    h1 = _out_proj((att,), w_o1, (h,), gated=False)
    h2, _ = _ffn(h1, vec(norm_ffn[1]), w_gu1, w_down1)
    yp, ys = _ple(h2, vec(norm_ple[1]), pp, ps, 1, w_gate1, w_proj1, split_output=True)

    keep = min(ATT_PREV * CHUNK, L)
    kv_p = kvn[:tp].reshape(B, L, 2 * kvw)[:, L - keep:]
    kv_s = kvn[tp:].reshape(Bs, Ls, 2 * kvw)
    heads = lambda t: t.reshape(*t.shape[:2], n_kv, ATT_HD)[None]
    return (yp.reshape(B, L, D), ys.reshape(Bs, Ls, D), state_p[None], state_s[None],
            heads(kv_p[..., :kvw]), heads(kv_p[..., kvw:]), heads(kv_s[..., :kvw]), heads(kv_s[..., kvw:]))
```

```python
import functools

import numpy as np
import jax
import jax.numpy as jnp
from jax import lax
from jax.experimental import pallas as pl
from jax.experimental.pallas import tpu as pltpu

F32 = jnp.float32
BF16 = jnp.bfloat16

RMS_EPS = 1e-6
CHUNK = 64
PAIR = 4
HEAD_DK = 128
ATT_HD = 64
ATT_GROUP = 8
ATT_PREV = 2
GLA_LEVELS = (64, 32, 16, 8, 4, 2)
GLA_SAFE_EXP = 60.0
ATT_SAFE_EXP = 70.0

NT_DIMS = (((1,), (1,)), ((), ()))
TN_DIMS = (((0,), (0,)), ((), ()))

VMEM_LIMIT_BYTES = 60 * 1024 * 1024


def _params(*sem):
    return pltpu.CompilerParams(dimension_semantics=sem, vmem_limit_bytes=VMEM_LIMIT_BYTES)


def _tile(n, pref):
    t = pref
    while t > 8 and n % t:
        t //= 2
    assert n % t == 0, (n, pref)
    return t


def _split_rows(tm, n_p, width, sample_buffers=None):
    mode = {} if sample_buffers is None else dict(pipeline_mode=pl.Buffered(sample_buffers))
    return (pl.BlockSpec((tm, width), lambda i, *_: (jnp.minimum(i, n_p - 1), 0)),
            pl.BlockSpec((tm, width), lambda i, *_: (jnp.maximum(i - n_p, 0), 0), **mode))


def _resident(shape):
    return pl.BlockSpec(shape, lambda *_: (0,) * len(shape), pipeline_mode=pl.Buffered(1))


def _row_chunks(rows, size=256):
    size = min(size, rows)
    assert rows % size == 0
    return [slice(r, r + size) for r in range(0, rows, size)]


def _side_cast_specs(side, n_steps, step_of):
    in_specs, out_specs, out_shape = [], [], []
    for arr, idx in side:
        _, rows, cols = arr.shape
        rb = 16
        while rows // rb > n_steps:
            rb *= 2
        assert rows % rb == 0
        last = rows // rb - 1
        in_specs.append(pl.BlockSpec(
            (1, rb, cols), lambda *g, idx=idx, last=last: (idx, jnp.minimum(step_of(*g), last), 0)))
        out_specs.append(pl.BlockSpec((rb, cols), lambda *g, last=last: (jnp.minimum(step_of(*g), last), 0)))
        out_shape.append(jax.ShapeDtypeStruct((rows, cols), BF16))
    return in_specs, out_specs, out_shape


def _side_cast(in_refs, out_refs):
    for src, dst in zip(in_refs, out_refs):
        dst[...] = src[0].astype(BF16)


def _sigmoid(x):
    return 1.0 / (1.0 + jnp.exp(-x))


def _rms_rows(x, w):
    ms = jnp.mean(x * x, axis=-1, keepdims=True)
    return x * lax.rsqrt(ms + RMS_EPS) * w


def _dot(a, b):
    return jnp.dot(a, b, preferred_element_type=F32)


def _hgrn_in_kernel(xp_ref, xs_ref, nw_ref, wq_ref, wf_ref, wi_ref, wg_ref, lbl_ref, *refs, n_p, n_side):
    side_in, refs = refs[:n_side], refs[n_side:]
    q_ref, k_ref, v_ref, lf_ref, gs_ref = refs[:5]
    side_out, (xn_ref,) = refs[5:5 + n_side], refs[5 + n_side:]
    first = pl.program_id(1) == 0
    is_prompt = pl.program_id(0) < n_p

    @pl.when(jnp.logical_and(first, is_prompt))
    def _():
        xn_ref[...] = _rms_rows(xp_ref[...], nw_ref[...]).astype(BF16)

    @pl.when(jnp.logical_and(first, jnp.logical_not(is_prompt)))
    def _():
        xn_ref[...] = _rms_rows(xs_ref[...], nw_ref[...]).astype(BF16)

    xn = xn_ref[...]
    q = _dot(xn, wq_ref[...])
    q_ref[...] = (q * _sigmoid(q)).astype(BF16)
    logits = lbl_ref[...]
    ex = jnp.exp(logits - jnp.max(logits, axis=0, keepdims=True))
    lb = ex[0:1] / jnp.sum(ex, axis=0, keepdims=True)
    f = _dot(xn, wf_ref[...])
    forget = lb + (1.0 - lb) * _sigmoid(f)
    k_ref[...] = (1.0 - forget).astype(BF16)
    lf_ref[...] = jnp.log(forget)
    v_ref[...] = _dot(xn, wi_ref[...]).astype(BF16)
    g = _dot(xn, wg_ref[...])
    gs_ref[...] = (g * _sigmoid(g)).astype(BF16)
    _side_cast(side_in, side_out)


def _hgrn_in(xp, xs, norm_w, w_in, lb_logits, side):
    (tp, D), ts = xp.shape, xs.shape[0]
    T = tp + ts
    tm = min(_tile(tp, 1024), _tile(ts, 1024))
    tn = 256
    nb = D // tn
    out = pl.BlockSpec((tm, tn), lambda i, j: (i, j))

    def wspec(g):
        return pl.BlockSpec((D, tn), lambda i, j, g=g: (0, j + g * nb))

    side_in, side_out, side_shape = _side_cast_specs(side, (T // tm) * nb, lambda i, j: i * nb + j)
    res = pl.pallas_call(
        functools.partial(_hgrn_in_kernel, n_p=tp // tm, n_side=len(side)),
        grid=(T // tm, nb),
        in_specs=[*_split_rows(tm, tp // tm, D, sample_buffers=1), pl.BlockSpec((1, D), lambda i, j: (0, 0)),
                  wspec(0), wspec(1), wspec(2), wspec(3),
                  pl.BlockSpec((lb_logits.shape[0], tn), lambda i, j: (0, j)), *side_in],
        out_specs=[out] * 5 + side_out,
        out_shape=[jax.ShapeDtypeStruct((T, D), BF16)] * 2
        + [jax.ShapeDtypeStruct((T, D), BF16), jax.ShapeDtypeStruct((T, D), F32),
           jax.ShapeDtypeStruct((T, D), BF16)] + side_shape,
        scratch_shapes=[pltpu.VMEM((tm, D), BF16)],
        compiler_params=_params("arbitrary", "arbitrary"),
        name="hgrn_in",
    )(xp, xs, norm_w, w_in, w_in, w_in, w_in, lb_logits, *[a for a, _ in side])
    return res[:5], res[5:]


def _gla_constants():
    c = CHUNK
    t = np.arange(c)[:, None]
    s = np.arange(c)[None, :]
    zs = [s <= t, s > t]
    level = np.full((c, c), -1, np.int32)
    for li, p in enumerate(GLA_LEVELS, start=1):
        half = p // 2
        mid = (t // p) * p + half - 1
        upper = (t % p) >= half
        zs.append(np.where(upper, (s > mid) & (s <= t), (s > t) & (s <= mid)))
        level[(t // p == s // p) & upper & ((s % p) < half)] = li
    np.fill_diagonal(level, 0)
    z = np.concatenate(zs, axis=0).astype(np.float32)
    zf = np.concatenate([zs[0], zs[0]], axis=0).astype(np.float32)
    zf[c:] *= -1.0
    three = lambda m: np.concatenate([m, m, m], axis=1)
    return three(z), three(zf), level


def _gla_kernel(q_ref, k_ref, v_ref, lf_ref, lfn_ref, s0_ref, z_ref, zf_ref, lvl_ref, o_ref, sp_ref, ss_ref,
                st_ref, e_ref, dec_ref, *, n_prompt_steps, steps_per_seq, n_heads):
    m = pl.program_id(0)
    is_prompt = m < n_prompt_steps
    is_sample = jnp.logical_not(is_prompt)
    c = jnp.where(is_prompt, lax.rem(m, steps_per_seq), 0)
    chunk_rows = [slice(t * CHUNK, (t + 1) * CHUNK) for t in range(PAIR)]
    sls = [slice(h * HEAD_DK, (h + 1) * HEAD_DK) for h in range(n_heads)]
    lvl = lvl_ref[...]

    def exponents(z, rows):
        g = lf_ref[rows, :]
        g_hi = g.astype(BF16)
        r1 = g - g_hi.astype(F32)
        g_mid = r1.astype(BF16)
        g_lo = (r1 - g_mid.astype(F32)).astype(BF16)
        return _dot(z, jnp.concatenate([g_hi, g_mid, g_lo], axis=0))

    def note_next_decays(ref):
        for t, rows in enumerate(chunk_rows):
            dec_ref[t] = jnp.min(jnp.sum(ref[rows, :], axis=0, keepdims=True))

    def single_reference(chunks):
        for t in chunks:
            e_ref[2 * t * CHUNK:2 * (t + 1) * CHUNK, :] = jnp.exp(exponents(zf_ref[...], chunk_rows[t]))
        eg = lambda t, sl: e_ref[2 * t * CHUNK:(2 * t + 1) * CHUNK, sl]
        en = lambda t, sl: e_ref[(2 * t + 1) * CHUNK:(2 * t + 2) * CHUNK, sl]
        heads = [(t, h) for t in chunks for h in range(n_heads)]
        q_hat = {(t, h): (q_ref[chunk_rows[t], sls[h]].astype(F32) * eg(t, sls[h])).astype(BF16) for t, h in heads}
        k_bar = {(t, h): (k_ref[chunk_rows[t], sls[h]].astype(F32) * en(t, sls[h])).astype(BF16) for t, h in heads}
        a = {th: lax.dot_general(q_hat[th], k_bar[th], NT_DIMS, preferred_element_type=F32) for th in heads}
        av = {(t, h): _dot(jnp.where(lvl >= 0, a[t, h], 0.0).astype(BF16), v_ref[chunk_rows[t], sls[h]])
              for t, h in heads}
        for t in chunks:
            rows = chunk_rows[t]
            o = [lax.dot_general(q_hat[t, h], st_ref[h].astype(BF16), NT_DIMS, preferred_element_type=F32)
                 for h in range(n_heads)]
            for h, sl in enumerate(sls):
                o_ref[rows, sl] = (o[h] + av[t, h]).astype(BF16)
            for h, sl in enumerate(sls):
                decay_all = e_ref[(2 * t + 1) * CHUNK - 1:(2 * t + 1) * CHUNK, sl]
                k_hat = (k_ref[rows, sl].astype(F32) * (en(t, sl) * decay_all)).astype(BF16)
                st_ref[h] = st_ref[h] * decay_all + lax.dot_general(v_ref[rows, sl], k_hat, TN_DIMS,
                                                                     preferred_element_type=F32)

    def binary_splitting(rows):
        e_ref[0:z_ref.shape[0], :] = jnp.exp(exponents(z_ref[...], rows))
        row = lax.broadcasted_iota(jnp.int32, (CHUNK, HEAD_DK), 0)
        for h, sl in enumerate(sls):
            qb = q_ref[rows, sl]
            kb = k_ref[rows, sl]
            vb = v_ref[rows, sl]
            qf = qb.astype(F32)
            kf = kb.astype(F32)
            st = st_ref[h]
            q_hat = (qf * e_ref[0:CHUNK, sl]).astype(BF16)
            k_hat = (kf * e_ref[CHUNK:2 * CHUNK, sl]).astype(BF16)
            o = lax.dot_general(q_hat, st.astype(BF16), NT_DIMS, preferred_element_type=F32)
            a = lax.dot_general(qb, kb, NT_DIMS, preferred_element_type=F32)
            a = jnp.where(lvl == 0, a, 0.0)
            for li, p in enumerate(GLA_LEVELS, start=1):
                on_q_side = (row & (p // 2)) != 0
                x = (jnp.where(on_q_side, qf, kf)
                     * e_ref[(li + 1) * CHUNK:(li + 2) * CHUNK, sl]).astype(BF16)
                a = jnp.where(lvl == li, lax.dot_general(x, x, NT_DIMS, preferred_element_type=F32), a)
            o_ref[rows, sl] = (o + _dot(a.astype(BF16), vb)).astype(BF16)
            decay_all = e_ref[CHUNK - 1:CHUNK, sl]
            st_ref[h] = st * decay_all + lax.dot_general(vb, k_hat, TN_DIMS, preferred_element_type=F32)

    @pl.when(m == 0)
    def _():
        note_next_decays(lf_ref)

    mild = [dec_ref[t] >= -GLA_SAFE_EXP for t in range(PAIR)]

    @pl.when(jnp.logical_and(is_prompt, c == 0))
    def _():
        st_ref[...] = jnp.zeros_like(st_ref)

    together = functools.reduce(jnp.logical_and, mild, is_prompt)

    @pl.when(together)
    def _():
        single_reference(range(PAIR))
        note_next_decays(lfn_ref)

    @pl.when(jnp.logical_not(together))
    def _():
        for t, rows in enumerate(chunk_rows):
            last = t == PAIR - 1

            @pl.when(is_sample)
            def _():
                for h in range(n_heads):
                    st_ref[h] = s0_ref[t, h].T

            @pl.when(mild[t])
            def _():
                single_reference([t])
                if last:
                    note_next_decays(lfn_ref)

            @pl.when(jnp.logical_not(mild[t]))
            def _():
                binary_splitting(rows)
                if last:
                    note_next_decays(lfn_ref)

            @pl.when(is_sample)
            def _():
                for h in range(n_heads):
                    ss_ref[t, h] = st_ref[h].T

    @pl.when(jnp.logical_and(is_prompt, c == steps_per_seq - 1))
    def _():
        for h in range(n_heads):
            sp_ref[0, h] = st_ref[h].T


def _gla(q, k, v, lf, s0, n_prompt_seq, chunks_per_seq):
    T, D = q.shape
    n_heads = D // HEAD_DK
    rows = PAIR * CHUNK
    n_steps = T // rows
    assert chunks_per_seq % PAIR == 0 and s0.shape[0] % PAIR == 0
    steps_per_seq = chunks_per_seq // PAIR
    nps = n_prompt_seq * steps_per_seq
    zcat, zfast, level = _gla_constants()
    blk = pl.BlockSpec((rows, D), lambda i: (i, 0))
    h_blk = (n_heads, HEAD_DK, HEAD_DK)

    def prompt_seq(i):
        return (jnp.minimum(i // steps_per_seq, n_prompt_seq - 1), 0, 0, 0)

    def sample_pair(i):
        return (jnp.maximum(i - nps, 0), 0, 0, 0)

    kern = functools.partial(_gla_kernel, n_prompt_steps=nps, steps_per_seq=steps_per_seq, n_heads=n_heads)
    return pl.pallas_call(
        kern,
        grid=(n_steps,),
        in_specs=[blk, blk, blk, blk,
                  pl.BlockSpec((rows, D), lambda i: (jnp.minimum(i + 1, n_steps - 1), 0)),
                  pl.BlockSpec((PAIR, *h_blk), sample_pair),
                  pl.BlockSpec(zcat.shape, lambda i: (0, 0)),
                  pl.BlockSpec(zfast.shape, lambda i: (0, 0)),
                  pl.BlockSpec(level.shape, lambda i: (0, 0))],
        out_specs=[blk, pl.BlockSpec((1, *h_blk), prompt_seq), pl.BlockSpec((PAIR, *h_blk), sample_pair)],
        out_shape=[jax.ShapeDtypeStruct((T, D), BF16),
                   jax.ShapeDtypeStruct((n_prompt_seq, *h_blk), F32),
                   jax.ShapeDtypeStruct((s0.shape[0], *h_blk), F32)],
        scratch_shapes=[pltpu.VMEM(h_blk, F32),
                        pltpu.VMEM((max(zcat.shape[0], 2 * PAIR * CHUNK), D), F32),
                        pltpu.SMEM((PAIR,), F32)],
        compiler_params=_params("arbitrary"),
        name="gla_scan",
    )(q, k, v, lf, lf, s0, jnp.asarray(zcat, BF16), jnp.asarray(zfast, BF16), jnp.asarray(level))


def _out_proj_kernel(*refs, gated, n_p):
    if gated:
        o_ref, gs_ref, gn_ref, w_ref, *h_refs, h1_ref = refs
    else:
        a_ref, w_ref, *h_refs, h1_ref = refs
    for r in _row_chunks(h1_ref.shape[0]):
        if gated:
            a = (_rms_rows(o_ref[r, :].astype(F32), gn_ref[...]) * gs_ref[r, :].astype(F32)).astype(BF16)
        else:
            a = a_ref[r, :]
        if len(h_refs) == 2:
            h = jnp.where(pl.program_id(0) < n_p, h_refs[0][r, :], h_refs[1][r, :])
        else:
            h = h_refs[0][r, :]
        h1_ref[r, :] = h + _dot(a, w_ref[...])


def _out_proj(a_inputs, w, hs, gated):
    T, D = a_inputs[0].shape
    tm = min(_tile(h.shape[0], 512) for h in hs)
    row = pl.BlockSpec((tm, D), lambda i: (i, 0))
    vec = pl.BlockSpec((1, D), lambda i: (0, 0))
    wspec = _resident((D, D))
    n_p = hs[0].shape[0] // tm
    h_specs = [row] if len(hs) == 1 else list(_split_rows(tm, n_p, D))
    if gated:
        in_specs = [row, row, vec, wspec, *h_specs]
    else:
        in_specs = [row, wspec, *h_specs]
    return pl.pallas_call(
        functools.partial(_out_proj_kernel, gated=gated, n_p=n_p),
        grid=(T // tm,),
        in_specs=in_specs,
        out_specs=row,
        out_shape=jax.ShapeDtypeStruct((T, D), F32),
        compiler_params=_params("parallel"),
        name="out_proj",
    )(*a_inputs, w, *hs)


def _ffn_kernel(h_ref, nf_ref, wg_ref, wu_ref, wd_ref, *refs, n_side):
    side_in, h2_ref = refs[:n_side], refs[n_side]
    side_out, (xn_ref,) = refs[n_side + 1:2 * n_side + 1], refs[2 * n_side + 1:]

    @pl.when(pl.program_id(1) == 0)
    def _():
        h = h_ref[...]
        xn_ref[...] = _rms_rows(h, nf_ref[...]).astype(BF16)
        h2_ref[...] = h

    xn = xn_ref[...]
    gate = _dot(xn, wg_ref[...])
    up = _dot(xn, wu_ref[...])
    h2_ref[...] += _dot((gate * _sigmoid(gate) * up).astype(BF16), wd_ref[...])
    _side_cast(side_in, side_out)


def _ffn(h, norm_ffn, w_gu, w_down, side=()):
    T, D = h.shape
    ff = w_down.shape[0]
    tm = _tile(T, 1024)
    tf = 512
    nf = ff // tf
    row = pl.BlockSpec((tm, D), lambda i, j: (i, 0))
    side_in, side_out, side_shape = _side_cast_specs(side, (T // tm) * nf, lambda i, j: i * nf + j)
    res = pl.pallas_call(
        functools.partial(_ffn_kernel, n_side=len(side)),
        grid=(T // tm, nf),
        in_specs=[row, pl.BlockSpec((1, D), lambda i, j: (0, 0)),
                  pl.BlockSpec((D, tf), lambda i, j: (0, j)),
                  pl.BlockSpec((D, tf), lambda i, j: (0, j + nf)),
                  pl.BlockSpec((tf, D), lambda i, j: (j, 0)), *side_in],
        out_specs=[row] + side_out,
        out_shape=[jax.ShapeDtypeStruct((T, D), F32)] + side_shape,
        scratch_shapes=[pltpu.VMEM((tm, D), BF16)],
        compiler_params=_params("arbitrary", "arbitrary"),
        name="ffn",
    )(h, norm_ffn, w_gu, w_gu, w_down, *[a for a, _ in side])
    return res[0], res[1:]


def _ple_kernel(h_ref, np_ref, *refs, n_p):
    *p_refs, wg_ref, wp_ref, out_ref = refs
    for r in _row_chunks(h_ref.shape[0]):
        h = h_ref[r, :]
        gate = _sigmoid(_dot(_rms_rows(h, np_ref[...]).astype(BF16), wg_ref[...]))
        if len(p_refs) == 2:
            p = jnp.where(pl.program_id(0) < n_p, p_refs[0][0, r, :], p_refs[1][0, r, :])
        else:
            p = p_refs[0][0, r, :]
        out_ref[r, :] = h + _dot(p.astype(BF16), wp_ref[...]) * gate


def _ple_call(h, norm_ple, ps, layer, w_gate, w_proj, tm, first_block, n_blocks, n_p):
    D = h.shape[1]
    pd = ps[0].shape[2]
    if len(ps) == 2:
        p_specs = [pl.BlockSpec((1, tm, pd), lambda i: (layer, jnp.minimum(i, n_p - 1), 0)),
                   pl.BlockSpec((1, tm, pd), lambda i: (layer, jnp.maximum(i - n_p, 0), 0))]
    else:
        p_specs = [pl.BlockSpec((1, tm, pd), lambda i: (layer, i, 0))]
    return pl.pallas_call(
        functools.partial(_ple_kernel, n_p=n_p),
        grid=(n_blocks,),
        in_specs=[pl.BlockSpec((tm, D), lambda i: (i + first_block, 0)),
                  pl.BlockSpec((1, D), lambda i: (0, 0)),
                  *p_specs, _resident((D, D)), _resident((pd, D))],
        out_specs=pl.BlockSpec((tm, D), lambda i: (i, 0)),
        out_shape=jax.ShapeDtypeStruct((n_blocks * tm, D), F32),
        compiler_params=_params("parallel"),
        name="ple",
    )(h, norm_ple, *ps, w_gate, w_proj)


def _ple(h, norm_ple, p_prompt, p_sample, layer, w_gate, w_proj, split_output):
    tp, ts = p_prompt.shape[1], p_sample.shape[1]
    tm = min(_tile(tp, 512), _tile(ts, 512))
    n_p, n_s = tp // tm, ts // tm
    args = (h, norm_ple)
    if split_output:
        return (_ple_call(*args, (p_prompt,), layer, w_gate, w_proj, tm, 0, n_p, n_p),
                _ple_call(*args, (p_sample,), layer, w_gate, w_proj, tm, n_p, n_s, n_p))
    return _ple_call(*args, (p_prompt, p_sample), layer, w_gate, w_proj, tm, 0, n_p + n_s, n_p)


def _qkv_kernel(h_ref, nm_ref, w_ref, seg_ref, gain_ref, q_ref, kv_ref, kmax_ref, xn_ref, *, k_width):
    j = pl.program_id(1)

    @pl.when(j == 0)
    def _():
        xn_ref[...] = _rms_rows(h_ref[...], nm_ref[...]).astype(BF16)

    x = _dot(xn_ref[...], w_ref[...])
    ms = _dot((x * x).astype(BF16), seg_ref[...]) * (1.0 / ATT_HD)
    normed = x * lax.rsqrt(ms + RMS_EPS) * gain_ref[0]
    q_ref[...] = normed.astype(BF16)

    @pl.when(j == 0)
    def _():
        is_k = lax.broadcasted_iota(jnp.int32, x.shape, 1) < k_width
        kv_ref[...] = jnp.where(is_k, normed, x)
        n_ch = x.shape[0] // CHUNK
        peak = jnp.max(jnp.where(is_k, jnp.abs(normed), 0.0).reshape(n_ch, CHUNK, x.shape[1]), axis=1)
        kmax_ref[0] = jnp.broadcast_to(jnp.max(peak, axis=1, keepdims=True), (n_ch, 128))


def _qkv(h, norm_mix, w_qkv, q_gain, k_gain):
    T, D = h.shape
    n_out = w_qkv.shape[1]
    tn = ATT_GROUP * ATT_HD
    n_q_tiles = D // tn
    assert n_out == D + tn
    kv_w = (n_out - D) // 2
    tm = _tile(T, 1024)
    n_ch = tm // CHUNK
    seg = np.kron(np.eye(tn // ATT_HD, dtype=np.float32), np.ones((ATT_HD, ATT_HD), np.float32))
    q_row = jnp.tile(q_gain.astype(F32), tn // ATT_HD) * (ATT_HD ** -0.5)
    kv_row = jnp.concatenate([jnp.tile(k_gain.astype(F32), kv_w // ATT_HD), jnp.ones((tn - kv_w,), F32)])
    gain = jnp.stack([kv_row] + [q_row] * n_q_tiles)[:, None, :]
    q, kv, kmax = pl.pallas_call(
        functools.partial(_qkv_kernel, k_width=kv_w),
        grid=(T // tm, n_q_tiles + 1),
        in_specs=[pl.BlockSpec((tm, D), lambda i, j: (i, 0)),
                  pl.BlockSpec((1, D), lambda i, j: (0, 0)),
                  pl.BlockSpec((D, tn), lambda i, j: (0, jnp.where(j == 0, n_q_tiles, j - 1))),
                  _resident((tn, tn)),
                  pl.BlockSpec((1, 1, tn), lambda i, j: (j, 0, 0))],
        out_specs=[pl.BlockSpec((tm, tn), lambda i, j: (i, jnp.maximum(j - 1, 0))),
                   pl.BlockSpec((tm, tn), lambda i, j: (i, 0)),
                   pl.BlockSpec((1, n_ch, 128), lambda i, j: (i, 0, 0))],
        out_shape=[jax.ShapeDtypeStruct((T, D), BF16), jax.ShapeDtypeStruct((T, tn), F32),
                   jax.ShapeDtypeStruct((T // tm, n_ch, 128), F32)],
        scratch_shapes=[pltpu.VMEM((tm, D), BF16)],
        compiler_params=_params("parallel", "arbitrary"),
        name="qkv_proj",
    )(h, norm_mix, w_qkv, jnp.asarray(seg, BF16), gain)
    return q, kv, kmax[:, :, 0].reshape(-1)


def _alibi_table(n_heads):
    slopes = np.float32(2.0) ** (np.float32(-8.0) * np.arange(1, n_heads + 1, dtype=np.float32) / np.float32(n_heads))
    qi = np.arange(CHUNK)[:, None]
    kj = np.arange((ATT_PREV + 1) * CHUNK)[None, :]
    dist = np.abs(ATT_PREV * CHUNK + qi - kj).astype(np.float32)
    return (-slopes[:, None, None] * dist).astype(np.float32)


def _attn_kernel(scal_ref, kmax_ref, cmax_ref, q_ref, kvo_ref, kvp_ref, ck_ref, cv_ref, alibi_ref, o_ref,
                 *, n_prompt_steps, steps_per_seq, n_kv_heads):
    m = pl.program_id(0)
    is_sample = m >= n_prompt_steps
    no_prev = jnp.logical_and(jnp.logical_not(is_sample), lax.rem(m, steps_per_seq) == 0)
    kvw = n_kv_heads * ATT_HD
    n_keys = (ATT_PREV + 1) * CHUNK
    n_heads = n_kv_heads * ATT_GROUP
    pairs = ATT_GROUP // 2
    qw = ATT_GROUP * ATT_HD
    lanes = 2 * ATT_HD

    kvo = kvo_ref[...]
    window = jnp.concatenate([kvp_ref[(PAIR - ATT_PREV) * CHUNK:, :], kvo], axis=0)
    low = lax.broadcasted_iota(jnp.int32, (n_keys, lanes), 1) < ATT_HD
    key_row = lax.broadcasted_iota(jnp.int32, (n_keys, kvw), 0)
    kj = lax.broadcasted_iota(jnp.int32, (CHUNK, n_keys), 1)

    def split_halves(t, odd):
        rolled = pltpu.roll(t, ATT_HD, 1)
        in_low, in_high = (rolled, t) if odd else (t, rolled)
        zero = jnp.zeros_like(t)
        return jnp.where(low, in_low, zero).astype(BF16), jnp.where(low, zero, in_high).astype(BF16)

    chunk = []
    for t in range(PAIR):
        rows = slice(t * CHUNK, (t + 1) * CHUNK)
        i = m * PAIR + t
        before = window[t * CHUNK:(t + ATT_PREV) * CHUNK]
        n_missing = jnp.where(no_prev, max(ATT_PREV - t, 0), 0)
        first_valid = n_missing * CHUNK
        k_prev = jnp.where(is_sample, ck_ref[t], before[:, :kvw])
        v_prev = jnp.where(is_sample, cv_ref[t], before[:, kvw:])
        k_all = jnp.concatenate([k_prev, kvo[rows, :kvw]], axis=0)
        key_ok = key_row >= first_valid
        v_all = jnp.where(key_ok, jnp.concatenate([v_prev, kvo[rows, kvw:]], axis=0), 0.0)
        ones = jnp.where(key_ok[:, :lanes], 1.0, 0.0).astype(BF16)
        k_prompt = jnp.maximum(kmax_ref[i], jnp.maximum(kmax_ref[i - jnp.minimum(1, ATT_PREV - n_missing)],
                                                        kmax_ref[i - (ATT_PREV - n_missing)]))
        k_sample = jnp.maximum(kmax_ref[i], cmax_ref[jnp.maximum(m - n_prompt_steps, 0) * PAIR + t])
        bound = scal_ref[n_heads] * (ATT_HD ** 0.5) * jnp.where(is_sample, k_sample, k_prompt)
        chunk.append(dict(rows=rows, k_all=k_all, v_all=v_all, ones=ones, valid=kj >= first_valid,
                          small=bound <= ATT_SAFE_EXP))

    def kv_head_operands(c, kh):
        col = (kh // 2) * lanes
        ks = split_halves(c["k_all"][:, col:col + lanes], kh % 2)
        vs = split_halves(c["v_all"][:, col:col + lanes], kh % 2)
        qs = jnp.concatenate([q_ref[c["rows"], kh * qw + p * lanes: kh * qw + (p + 1) * lanes]
                              for p in range(pairs)], axis=0)
        return qs, ks, vs

    def without_max(chunks):
        units = [(t, kh, parity) for t in chunks for kh in range(n_kv_heads) for parity in range(2)]
        ops = {(t, kh): kv_head_operands(chunk[t], kh) for t in chunks for kh in range(n_kv_heads)}
        scores = {u: lax.dot_general(ops[u[0], u[1]][0], ops[u[0], u[1]][1][u[2]], NT_DIMS,
                                     preferred_element_type=F32) for u in units}
        probs = {}
        for t, kh, parity in units:
            s_all = scores[t, kh, parity]
            blocks = []
            for p in range(pairs):
                head = kh * ATT_GROUP + 2 * p + parity
                blocks.append(jnp.exp(s_all[p * CHUNK:(p + 1) * CHUNK] + alibi_ref[head]).astype(BF16))
            probs[t, kh, parity] = jnp.concatenate(blocks, axis=0)
        both = {u: _dot(probs[u], jnp.concatenate([ops[u[0], u[1]][2][u[2]], chunk[u[0]]["ones"]], axis=1))
                for u in units}
        for t in chunks:
            for kh in range(n_kv_heads):
                for p in range(pairs):
                    blk = slice(p * CHUNK, (p + 1) * CHUNK)
                    out = None
                    for parity in range(2):
                        head = kh * ATT_GROUP + 2 * p + parity
                        b = both[t, kh, parity]
                        den = b[blk, lanes:] + jnp.exp(jnp.full((1, lanes), scal_ref[head], F32))
                        part = b[blk, :lanes] / den
                        out = part if out is None else out + part
                    o_ref[chunk[t]["rows"], kh * qw + p * lanes: kh * qw + (p + 1) * lanes] = out.astype(BF16)

    def with_max(t):
        c = chunk[t]
        for kh in range(n_kv_heads):
            qs, ks, vs = kv_head_operands(c, kh)
            acc = None
            for parity in range(2):
                s_all = lax.dot_general(qs, ks[parity], NT_DIMS, preferred_element_type=F32)
                probs, inv = [], []
                for p in range(pairs):
                    head = kh * ATT_GROUP + 2 * p + parity
                    s = jnp.where(c["valid"], s_all[p * CHUNK:(p + 1) * CHUNK] + alibi_ref[head], -jnp.inf)
                    sink = scal_ref[head]
                    mx = jnp.maximum(jnp.max(s, axis=-1, keepdims=True), sink)
                    e = jnp.exp(s - mx)
                    den = jnp.sum(e, axis=-1, keepdims=True) + jnp.exp(sink - mx)
                    probs.append(e.astype(BF16))
                    inv.append(1.0 / den)
                part = _dot(jnp.concatenate(probs, axis=0), vs[parity]) * jnp.concatenate(inv, axis=0)
                acc = part if acc is None else acc + part
            for p in range(pairs):
                o_ref[c["rows"], kh * qw + p * lanes: kh * qw + (p + 1) * lanes] = (
                    acc[p * CHUNK:(p + 1) * CHUNK].astype(BF16))

    together = functools.reduce(jnp.logical_and, [c["small"] for c in chunk])

    @pl.when(together)
    def _():
        without_max(range(PAIR))

    @pl.when(jnp.logical_not(together))
    def _():
        for t in range(PAIR):
            @pl.when(chunk[t]["small"])
            def _():
                without_max([t])

            @pl.when(jnp.logical_not(chunk[t]["small"]))
            def _():
                with_max(t)


def _attention(q, kv, kmax, cache_k, cache_v, scalars, n_prompt_seq, chunks_per_seq):
    T, d_model = q.shape
    rows = PAIR * CHUNK
    n_steps = T // rows
    assert chunks_per_seq % PAIR == 0 and cache_k.shape[0] % PAIR == 0 and PAIR >= ATT_PREV
    steps_per_seq = chunks_per_seq // PAIR
    nps = n_prompt_seq * steps_per_seq
    kvw2 = kv.shape[1]
    n_kv_heads = kvw2 // (2 * ATT_HD)
    n_cache = cache_k.shape[1]
    assert n_cache == ATT_PREV * CHUNK

    def prev_step(i):
        inside = jnp.logical_and(i < nps, lax.rem(i, steps_per_seq) >= 1)
        return (jnp.where(inside, i - 1, i), 0)

    alibi = _alibi_table(n_kv_heads * ATT_GROUP)
    cmax = jnp.max(jnp.abs(cache_k), axis=(1, 2))
    smem = pl.BlockSpec(memory_space=pltpu.SMEM)
    cache_spec = pl.BlockSpec((PAIR, n_cache, kvw2 // 2), lambda i: (jnp.maximum(i - nps, 0), 0, 0))
    kern = functools.partial(_attn_kernel, n_prompt_steps=nps, steps_per_seq=steps_per_seq,
                             n_kv_heads=n_kv_heads)
    return pl.pallas_call(
        kern,
        grid=(n_steps,),
        in_specs=[smem, smem, smem,
                  pl.BlockSpec((rows, d_model), lambda i: (i, 0)),
                  pl.BlockSpec((rows, kvw2), lambda i: (i, 0)),
                  pl.BlockSpec((rows, kvw2), prev_step),
                  cache_spec, cache_spec, _resident(alibi.shape)],
        out_specs=pl.BlockSpec((rows, d_model), lambda i: (i, 0)),
        out_shape=jax.ShapeDtypeStruct((T, d_model), BF16),
        compiler_params=_params("parallel"),
        name="swa_attention",
    )(scalars, kmax, cmax, q, kv, kv, cache_k, cache_v, jnp.asarray(alibi))


def kernel(x_prompt, x_sample, state_hgrn, cache_k, cache_v, p_prompt, p_sample, norm_mix, norm_ffn,
           norm_ple, a_w_in, a_lb_logits, a_g_norm, a_w_o, b_w_qkv, b_q_norm, b_k_norm, b_sinks, b_w_o,
           f_w_gu, f_w_down, ple_w_proj, ple_w_gate):
    B, L, D = x_prompt.shape
    Bs, Ls, _ = x_sample.shape
    assert L % CHUNK == 0 and Ls == CHUNK
    tp, ts = B * L, Bs * Ls
    cps = L // CHUNK
    depth = norm_mix.shape[0]
    assert depth == 2

    xp, xs = x_prompt.reshape(tp, D), x_sample.reshape(ts, D)
    pp, ps = p_prompt.reshape(depth, tp, -1), p_sample.reshape(depth, ts, -1)
    vec = lambda w: w.reshape(1, -1).astype(F32)

    (q, k, v, lf, gs), (w_gu0, w_down0, w_o0, w_gate0, w_proj0) = _hgrn_in(
        xp, xs, vec(norm_mix[0]), a_w_in[0].astype(BF16), a_lb_logits.astype(F32),
        side=[(f_w_gu, 0), (f_w_down, 0), (a_w_o, 0), (ple_w_gate, 0), (ple_w_proj, 0)])

    o, state_p, state_s = _gla(q, k, v, lf, state_hgrn[0].astype(F32), B, cps)
    h1 = _out_proj((o, gs, vec(a_g_norm[0])), w_o0, (xp, xs), gated=True)
    h2, (w_gu1, w_down1, w_qkv, w_o1, w_gate1, w_proj1) = _ffn(
        h1, vec(norm_ffn[0]), w_gu0, w_down0,
        side=[(f_w_gu, 1), (f_w_down, 1), (b_w_qkv, 0), (b_w_o, 0), (ple_w_gate, 1), (ple_w_proj, 1)])
    h = _ple(h2, vec(norm_ple[0]), pp, ps, 0, w_gate0, w_proj0, split_output=False)

    qn, kvn, kmax = _qkv(h, vec(norm_mix[1]), w_qkv, b_q_norm[0], b_k_norm[0])
    n_kv = cache_k.shape[3]
    kvw = n_kv * ATT_HD
    scalars = jnp.concatenate([b_sinks[0].astype(F32), jnp.max(jnp.abs(b_q_norm[0])).reshape(1).astype(F32)])
    att = _attention(qn, kvn, kmax, cache_k[0].reshape(Bs, -1, kvw).astype(F32),
                     cache_v[0].reshape(Bs, -1, kvw).astype(F32), scalars, B, cps)
    h1 = _out_proj((att,), w_o1, (h,), gated=False)
    h2, _ = _ffn(h1, vec(norm_ffn[1]), w_gu1, w_down1)
    yp, ys = _ple(h2, vec(norm_ple[1]), pp, ps, 1, w_gate1, w_proj1, split_output=True)

    keep = min(ATT_PREV * CHUNK, L)
    kv_p = kvn[:tp].reshape(B, L, 2 * kvw)[:, L - keep:]
    kv_s = kvn[tp:].reshape(Bs, Ls, 2 * kvw)
    heads = lambda t: t.reshape(*t.shape[:2], n_kv, ATT_HD)[None]
    return (yp.reshape(B, L, D), ys.reshape(Bs, Ls, D), state_p[None], state_s[None],
            heads(kv_p[..., :kvw]), heads(kv_p[..., kvw:]), heads(kv_s[..., :kvw]), heads(kv_s[..., kvw:]))
```
